```python
import jax, jax.numpy as jnp
from jax import lax
import numpy as np

D_MODEL = 1024
BATCH = 8
SEQ = 4096
DEPTH = 1

HEAD_DIM = 64
RWKV_HEADS = 8
RWKV_WIDTH = RWKV_HEADS * HEAD_DIM
DECAY_LORA = 64
ICLR_LORA = 64
GATE_LORA = 128
ATTN_Q_HEADS = 8
ATTN_KV_HEADS = 2
ATTN_GROUPS = ATTN_Q_HEADS // ATTN_KV_HEADS
ATTN_Q_WIDTH = ATTN_Q_HEADS * HEAD_DIM
ATTN_KV_WIDTH = ATTN_KV_HEADS * HEAD_DIM
WINDOW = 128
BLOCK = 128
ROPE_THETA = 500000.0
ROPE_DIM = HEAD_DIM // 4
D_FF = -(-8 * D_MODEL // (3 * 256)) * 256
N_BRANCH = 2
RMS_EPS = 1e-6
GN_EPS = 64e-5
NEG_INF = -1e30
RWKV_SHIFT_WIDTH = 3 * RWKV_WIDTH + DECAY_LORA + ICLR_LORA + GATE_LORA
IN_WIDTH = RWKV_SHIFT_WIDTH + ATTN_Q_WIDTH + 2 * ATTN_KV_WIDTH + N_BRANCH * D_MODEL

kernel_name = "hybrid_rwkv7_swa_sink_adaln_block"


def rms_norm(x, gain, eps=RMS_EPS):
    x32 = x.astype(jnp.float32)
    inv = lax.rsqrt(jnp.mean(x32 * x32, axis=-1, keepdims=True) + eps)
    return (x32 * inv).astype(x.dtype) * gain


def token_shift(p):
    return jnp.pad(p, ((0, 0), (1, 0), (0, 0)))[:, :-1]


def partial_rope(x, positions):
    half = ROPE_DIM // 2
    inv_freq = ROPE_THETA ** (-jnp.arange(half, dtype=jnp.float32) / half)
    ang = positions.astype(jnp.float32)[..., None] * inv_freq
    cos = jnp.cos(ang)[:, :, None, :]
    sin = jnp.sin(ang)[:, :, None, :]
    xr = x[..., :ROPE_DIM].astype(jnp.float32)
    x1, x2 = xr[..., :half], xr[..., half:]
    rot = jnp.concatenate([x1 * cos - x2 * sin, x2 * cos + x1 * sin], axis=-1).astype(x.dtype)
    return jnp.concatenate([rot, x[..., ROPE_DIM:]], axis=-1)


def wkv7_scan(r, w, k, v, a, b):
    B_, S_, H, N = r.shape

    def step(state, inp):
        r_t, w_t, k_t, v_t, a_t, b_t = inp
        sa = jnp.einsum('bhvk,bhk->bhv', state, a_t)
        state = state * w_t[:, :, None, :] + sa[..., None] * b_t[:, :, None, :] + v_t[..., None] * k_t[:, :, None, :]
        return state, jnp.einsum('bhvk,bhk->bhv', state, r_t)

    xs = tuple(jnp.moveaxis(t.astype(jnp.float32), 1, 0) for t in (r, w, k, v, a, b))
    init = jnp.zeros((B_, H, N, N), jnp.float32)
    _, ys = lax.scan(step, init, xs)
    return jnp.moveaxis(ys, 0, 1)


def rwkv7_time_mix(cols, decay_w0, decay_up, iclr_a0, iclr_up, gate_up, k_k, k_a, r_k, lnx_gain, lnx_bias):
    B_, S_, _ = cols.shape
    W = RWKV_WIDTH
    f32 = jnp.float32
    r = cols[..., :W]
    k = cols[..., W:2 * W]
    v = cols[..., 2 * W:3 * W]
    o = 3 * W
    xw = cols[..., o:o + DECAY_LORA]
    xa = cols[..., o + DECAY_LORA:o + DECAY_LORA + ICLR_LORA]
    xg = cols[..., o + DECAY_LORA + ICLR_LORA:]
    w_log = -jax.nn.softplus(-(decay_w0 + jnp.tanh(xw) @ decay_up).astype(f32)) - 0.5
    decay = jnp.exp(-jnp.exp(w_log))
    a = jax.nn.sigmoid(iclr_a0 + xa @ iclr_up)
    g = jax.nn.sigmoid(xg) @ gate_up
    heads = lambda t: t.reshape(B_, S_, RWKV_HEADS, HEAD_DIM)
    kk = heads(k * k_k).astype(f32)
    kk = kk / jnp.maximum(jnp.sqrt(jnp.sum(kk * kk, axis=-1, keepdims=True)), 1e-12)
    k = k * (1 + (a - 1) * k_a)
    rh, kh, vh = heads(r), heads(k), heads(v)
    ah = heads(a).astype(f32)
    y = wkv7_scan(rh, heads(decay), kh, vh, -kk, kk * ah)
    mu = jnp.mean(y, axis=-1, keepdims=True)
    var = jnp.mean(jnp.square(y - mu), axis=-1, keepdims=True)
    yn = (y - mu) * lax.rsqrt(var + GN_EPS)
    yn = yn * lnx_gain.reshape(RWKV_HEADS, HEAD_DIM).astype(f32) + lnx_bias.reshape(RWKV_HEADS, HEAD_DIM).astype(f32)
    bonus = jnp.sum((rh * kh * r_k).astype(f32), axis=-1, keepdims=True) * vh.astype(f32)
    return (yn + bonus).reshape(B_, S_, W).astype(cols.dtype) * g


def sliding_window_sink_attention(q, k, v, positions, q_norm_gain, k_norm_gain, sinks):
    B_, S_, _ = q.shape
    nblk = S_ // BLOCK
    q = partial_rope(rms_norm(q.reshape(B_, S_, ATTN_Q_HEADS, HEAD_DIM), q_norm_gain), positions)
    k = partial_rope(rms_norm(k.reshape(B_, S_, ATTN_KV_HEADS, HEAD_DIM), k_norm_gain), positions)
    v = v.reshape(B_, S_, ATTN_KV_HEADS, HEAD_DIM)
    qb = q.reshape(B_, nblk, BLOCK, ATTN_KV_HEADS, ATTN_GROUPS, HEAD_DIM)

    def band(t):
        tb = t.reshape(B_, nblk, BLOCK, ATTN_KV_HEADS, HEAD_DIM)
        prev = jnp.pad(tb, ((0, 0), (1, 0), (0, 0), (0, 0), (0, 0)))[:, :-1]
        return jnp.concatenate([prev, tb], axis=2)

    kband, vband = band(k), band(v)
    scores = jnp.einsum('bnqhgd,bnkhd->bnhgqk', qb, kband).astype(jnp.float32) * (HEAD_DIM ** -0.5)
    q_idx = jnp.arange(BLOCK)[:, None]
    k_idx = jnp.arange(2 * BLOCK)[None, :]
    dist = q_idx + BLOCK - k_idx
    in_band = (dist >= 0) & (dist < WINDOW)
    blk = jnp.arange(nblk)[:, None, None]
    valid = in_band[None] & ((blk > 0) | (k_idx >= BLOCK)[None])
    scores = jnp.where(valid[None, :, None, None], scores, NEG_INF)
    sink = sinks.astype(jnp.float32).reshape(ATTN_KV_HEADS, ATTN_GROUPS)[None, None, :, :, None, None]
    m = jnp.maximum(jnp.max(scores, axis=-1, keepdims=True), sink)
    e = jnp.exp(scores - m)
    probs = e / (jnp.sum(e, axis=-1, keepdims=True) + jnp.exp(sink - m))
    out = jnp.einsum('bnhgqk,bnkhd->bnqhgd', probs.astype(v.dtype), vband)
    return out.reshape(B_, S_, ATTN_Q_WIDTH)


def setup_inputs(seed: int = 0) -> dict:
    key = jax.random.key(seed)
    ks = jax.random.split(key, 32)
    f32 = jnp.float32
    nrm = lambda i, shape, s: jax.random.normal(ks[i], shape, f32) * s
    L = DEPTH
    offsets = jax.random.randint(ks[2], (BATCH, 1), 0, 2048, dtype=jnp.int32)
    positions = offsets + jnp.arange(SEQ, dtype=jnp.int32)[None, :]
    return {
        "x": nrm(0, (BATCH, SEQ, D_MODEL), 1.0),
        "c": nrm(1, (BATCH, D_MODEL), 1.0),
        "positions": positions,
        "ada_w": nrm(3, (L, D_MODEL, 6 * D_MODEL), 0.2 * D_MODEL ** -0.5),
        "ada_b": nrm(4, (L, 6 * D_MODEL), 0.01),
        "norm1_gain": 1.0 + nrm(5, (L, D_MODEL), 0.02),
        "norm2_gain": 1.0 + nrm(6, (L, D_MODEL), 0.02),
        "w_in": nrm(7, (L, D_MODEL, IN_WIDTH), D_MODEL ** -0.5),
        "tshift_mu": jax.random.uniform(ks[8], (L, RWKV_SHIFT_WIDTH), f32),
        "decay_w0": jax.random.uniform(ks[9], (L, RWKV_WIDTH), f32, -6.0, 1.0),
        "decay_up": nrm(10, (L, DECAY_LORA, RWKV_WIDTH), 0.5 * DECAY_LORA ** -0.5),
        "iclr_a0": nrm(11, (L, RWKV_WIDTH), 0.5),
        "iclr_up": nrm(12, (L, ICLR_LORA, RWKV_WIDTH), 0.5 * ICLR_LORA ** -0.5),
        "gate_up": nrm(13, (L, GATE_LORA, RWKV_WIDTH), GATE_LORA ** -0.5),
        "k_k": 0.85 + nrm(14, (L, RWKV_WIDTH), 0.05),
        "k_a": 1.0 + nrm(15, (L, RWKV_WIDTH), 0.05),
        "r_k": nrm(16, (L, RWKV_HEADS, HEAD_DIM), 0.1),
        "lnx_gain": 1.0 + nrm(17, (L, RWKV_WIDTH), 0.02),
        "lnx_bias": nrm(18, (L, RWKV_WIDTH), 0.01),
        "q_norm_gain": 1.0 + nrm(19, (L, HEAD_DIM), 0.02),
        "k_norm_gain": 1.0 + nrm(20, (L, HEAD_DIM), 0.02),
        "attn_sinks": nrm(21, (L, ATTN_Q_HEADS), 1.0),
        "branch_gate_b": nrm(22, (L, N_BRANCH * D_MODEL), 0.1),
        "w_branch_a": nrm(23, (L, RWKV_WIDTH, D_MODEL), RWKV_WIDTH ** -0.5),
        "w_branch_b": nrm(24, (L, ATTN_Q_WIDTH, D_MODEL), ATTN_Q_WIDTH ** -0.5),
        "w_out": nrm(25, (L, D_MODEL, D_MODEL), D_MODEL ** -0.5),
        "ffn_w1": nrm(26, (L, D_MODEL, D_FF), D_MODEL ** -0.5),
        "ffn_w3": nrm(27, (L, D_MODEL, D_FF), D_MODEL ** -0.5),
        "ffn_w2": nrm(28, (L, D_FF, D_MODEL), D_FF ** -0.5),
    }


def reference(x, c, positions, ada_w, ada_b, norm1_gain, norm2_gain, w_in, tshift_mu,
              decay_w0, decay_up, iclr_a0, iclr_up, gate_up, k_k, k_a, r_k, lnx_gain, lnx_bias,
              q_norm_gain, k_norm_gain, attn_sinks, branch_gate_b, w_branch_a, w_branch_b, w_out,
              ffn_w1, ffn_w3, ffn_w2):
    q_lo = RWKV_SHIFT_WIDTH
    k_lo = q_lo + ATTN_Q_WIDTH
    v_lo = k_lo + ATTN_KV_WIDTH
    g_lo = v_lo + ATTN_KV_WIDTH
    for l in range(DEPTH):
        ada = (c @ ada_w[l] + ada_b[l])[:, None, :]
        shift1, scale1, gate1, shift2, scale2, gate2 = jnp.split(ada, 6, axis=-1)

        h = rms_norm(x, norm1_gain[l]) * (1 + scale1) + shift1
        proj = jnp.einsum('bsd,de->bse', h, w_in[l])
        rwkv_cols = proj[..., :q_lo]
        rwkv_cols = rwkv_cols + (token_shift(rwkv_cols) - rwkv_cols) * tshift_mu[l]
        y_a = rwkv7_time_mix(rwkv_cols, decay_w0[l], decay_up[l], iclr_a0[l], iclr_up[l], gate_up[l],
                             k_k[l], k_a[l], r_k[l], lnx_gain[l], lnx_bias[l])
        y_b = sliding_window_sink_attention(proj[..., q_lo:k_lo], proj[..., k_lo:v_lo], proj[..., v_lo:g_lo],
                                            positions, q_norm_gain[l], k_norm_gain[l], attn_sinks[l])
        gates = jax.nn.sigmoid(proj[..., g_lo:] + branch_gate_b[l])
        gate_a, gate_b = gates[..., :D_MODEL], gates[..., D_MODEL:]
        merged = gate_a * (y_a @ w_branch_a[l]) + gate_b * (y_b @ w_branch_b[l])
        x = x + gate1 * (merged @ w_out[l])

        h2 = rms_norm(x, norm2_gain[l]) * (1 + scale2) + shift2
        ffn = (jax.nn.silu(h2 @ ffn_w1[l]) * (h2 @ ffn_w3[l])) @ ffn_w2[l]
        x = x + gate2 * ffn
    return x
```

```python
import functools

import jax
import jax.numpy as jnp
from jax import lax
from jax.experimental import pallas as pl
from jax.experimental.pallas import tpu as pltpu

F32 = jnp.float32
BF16 = jnp.bfloat16

D_MODEL = 1024
HEAD_DIM = 64
RWKV_HEADS = 8
RWKV_WIDTH = RWKV_HEADS * HEAD_DIM
DECAY_LORA = 64
ICLR_LORA = 64
GATE_LORA = 128
ATTN_Q_HEADS = 8
ATTN_KV_HEADS = 2
ATTN_GROUPS = ATTN_Q_HEADS // ATTN_KV_HEADS
ATTN_Q_WIDTH = ATTN_Q_HEADS * HEAD_DIM
ATTN_KV_WIDTH = ATTN_KV_HEADS * HEAD_DIM
WINDOW = 128
BLOCK = 128
ROPE_THETA = 500000.0
ROPE_DIM = HEAD_DIM // 4
ROPE_HALF = ROPE_DIM // 2
RMS_EPS = 1e-6
GN_EPS = 64e-5
NEG_INF = -1e30
RWKV_SHIFT_WIDTH = 3 * RWKV_WIDTH + DECAY_LORA + ICLR_LORA + GATE_LORA
QKV_WIDTH = ATTN_Q_WIDTH + 2 * ATTN_KV_WIDTH
GATE_WIDTH = 2 * D_MODEL

V7X_LANES = 128
V7X_VMEM_LIMIT_BYTES = 56 * 1024 * 1024

INPROJ_ROWS = 256
WKV_CHUNK = 64
TAIL_ROWS = 256
ROPE_ROWS = 512


def _mm(a, b):
    return jnp.dot(a.astype(BF16), b.astype(BF16), preferred_element_type=F32)


def _mm_nt(a, b):
    return lax.dot_general(a.astype(BF16), b.astype(BF16), (((1,), (1,)), ((), ())),
                           preferred_element_type=F32)


def _mm_tn(a, b):
    return lax.dot_general(a.astype(BF16), b.astype(BF16), (((0,), (0,)), ((), ())),
                           preferred_element_type=F32)


def _ada_kernel(c_ref, w_ref, b_ref, o_ref):
    o_ref[...] = jnp.dot(c_ref[...], w_ref[...], precision=lax.Precision.HIGHEST,
                         preferred_element_type=F32) + b_ref[...]


def _ada(c, ada_w, ada_b):
    batch = c.shape[0]
    n_out = ada_w.shape[1]
    return pl.pallas_call(
        _ada_kernel,
        grid=(n_out // D_MODEL,),
        in_specs=[
            pl.BlockSpec((batch, D_MODEL), lambda j: (0, 0)),
            pl.BlockSpec((D_MODEL, D_MODEL), lambda j: (0, j)),
            pl.BlockSpec((1, D_MODEL), lambda j: (0, j)),
        ],
        out_specs=pl.BlockSpec((batch, D_MODEL), lambda j: (0, j)),
        out_shape=jax.ShapeDtypeStruct((batch, n_out), F32),
        name="ada",
    )(c, ada_w, ada_b.reshape(1, n_out))


def _rope_kernel(pos_ref, freq_ref, sgn_ref, cos_ref, sin_ref):
    ang = pos_ref[0].astype(F32) * freq_ref[...]
    sgn = sgn_ref[...]
    cos_ref[0] = jnp.where(sgn == 0.0, 1.0, jnp.cos(ang))
    sin_ref[0] = jnp.sin(ang) * sgn


def _rope_tables(positions):
    batch, seq = positions.shape
    inv_freq = ROPE_THETA ** (-jnp.arange(ROPE_HALF, dtype=F32) / ROPE_HALF)
    lane = jnp.arange(V7X_LANES) % HEAD_DIM
    freq = jnp.where(lane < ROPE_DIM, inv_freq[lane % ROPE_HALF], 0.0).reshape(1, V7X_LANES)
    sgn = jnp.where(lane < ROPE_HALF, -1.0, jnp.where(lane < ROPE_DIM, 1.0, 0.0))
    sgn = sgn.astype(F32).reshape(1, V7X_LANES)
    tab = jax.ShapeDtypeStruct((batch, seq, V7X_LANES), F32)
    return pl.pallas_call(
        _rope_kernel,
        grid=(batch, seq // ROPE_ROWS),
        in_specs=[
            pl.BlockSpec((1, ROPE_ROWS, 1), lambda b, j: (b, j, 0)),
            pl.BlockSpec((1, V7X_LANES), lambda b, j: (0, 0)),
            pl.BlockSpec((1, V7X_LANES), lambda b, j: (0, 0)),
        ],
        out_specs=[
            pl.BlockSpec((1, ROPE_ROWS, V7X_LANES), lambda b, j: (b, j, 0)),
            pl.BlockSpec((1, ROPE_ROWS, V7X_LANES), lambda b, j: (b, j, 0)),
        ],
        out_shape=[tab, tab],
        name="rope",
    )(positions.reshape(batch, seq, 1), freq, sgn)


def _inproj_kernel(x_ref, ada_ref, gain_ref, w_ref, mu_ref, gb_ref,
                   rw_ref, qkv_ref, gt_ref, carry_ref):
    rows = x_ref.shape[1]
    x = x_ref[0]
    ada = ada_ref[0]
    shift1 = ada[:, 0:D_MODEL]
    scale1 = ada[:, D_MODEL:2 * D_MODEL]
    inv = lax.rsqrt(jnp.mean(x * x, axis=-1, keepdims=True) + RMS_EPS)
    h = ((x * inv) * gain_ref[...] * (1.0 + scale1) + shift1).astype(BF16)

    @pl.when(pl.program_id(1) == 0)
    def _():
        carry_ref[...] = jnp.zeros_like(carry_ref)

    p = jnp.dot(h, w_ref[:, 0:RWKV_SHIFT_WIDTH], preferred_element_type=F32)
    prev = pltpu.roll(p, 1, axis=0)
    row = lax.broadcasted_iota(jnp.int32, p.shape, 0)
    prev = jnp.where(row == 0, carry_ref[...], prev)
    carry_ref[...] = p[rows - 1:rows, :]
    rw_ref[0] = p + (prev - p) * mu_ref[...]

    qkv_ref[0] = jnp.dot(h, w_ref[:, RWKV_SHIFT_WIDTH:RWKV_SHIFT_WIDTH + QKV_WIDTH],
                         preferred_element_type=F32).astype(BF16)
    gl = jnp.dot(h, w_ref[:, RWKV_SHIFT_WIDTH + QKV_WIDTH:], preferred_element_type=F32)
    gt_ref[0] = jax.nn.sigmoid(gl + gb_ref[...]).astype(BF16)


def _inproj(x, ada3, gain, w_in_bf, mu, gate_b):
    batch, seq, _ = x.shape
    in_width = w_in_bf.shape[1]
    tm = INPROJ_ROWS
    const = lambda b, j: (0, 0)
    return pl.pallas_call(
        _inproj_kernel,
        grid=(batch, seq // tm),
        in_specs=[
            pl.BlockSpec((1, tm, D_MODEL), lambda b, j: (b, j, 0)),
            pl.BlockSpec((1, 1, 6 * D_MODEL), lambda b, j: (b, 0, 0)),
            pl.BlockSpec((1, D_MODEL), const),
            pl.BlockSpec((D_MODEL, in_width), const, pipeline_mode=pl.Buffered(1)),
            pl.BlockSpec((1, RWKV_SHIFT_WIDTH), const),
            pl.BlockSpec((1, GATE_WIDTH), const),
        ],
        out_specs=[
            pl.BlockSpec((1, tm, RWKV_SHIFT_WIDTH), lambda b, j: (b, j, 0)),
            pl.BlockSpec((1, tm, QKV_WIDTH), lambda b, j: (b, j, 0)),
            pl.BlockSpec((1, tm, GATE_WIDTH), lambda b, j: (b, j, 0)),
        ],
        out_shape=[
            jax.ShapeDtypeStruct((batch, seq, RWKV_SHIFT_WIDTH), F32),
            jax.ShapeDtypeStruct((batch, seq, QKV_WIDTH), BF16),
            jax.ShapeDtypeStruct((batch, seq, GATE_WIDTH), BF16),
        ],
        scratch_shapes=[pltpu.VMEM((1, RWKV_SHIFT_WIDTH), F32)],
        compiler_params=pltpu.CompilerParams(
            dimension_semantics=("parallel", "arbitrary"),
            vmem_limit_bytes=V7X_VMEM_LIMIT_BYTES),
        name="inproj",
    )(x, ada3, gain, w_in_bf, mu, gate_b)


def _cumsum_rows(x):
    n = x.shape[0]
    row = lax.broadcasted_iota(jnp.int32, x.shape, 0)
    s = 1
    while s < n:
        x = x + jnp.where(row >= s, pltpu.roll(x, s, axis=0), 0.0)
        s *= 2
    return x


def _unit_lower_inverse(n_mat):
    size = n_mat.shape[0]
    row = lax.broadcasted_iota(jnp.int32, n_mat.shape, 0)
    col = lax.broadcasted_iota(jnp.int32, n_mat.shape, 1)
    p = jnp.where(row == col, 1.0, 0.0) + n_mat
    m = n_mat
    power = 2
    while power < size:
        m = _mm(m, m)
        p = p + _mm(p, m)
        power *= 2
    return p


def _wkv_kernel(rw_ref, w0_ref, dup_ref, a0_ref, aup_ref, gup_ref, kk_ref, ka_ref, rk_ref,
                lng_ref, lnb_ref, y_ref, state_ref, yacc_ref):
    L = rw_ref.shape[1]
    W = RWKV_WIDTH

    @pl.when(pl.program_id(1) == 0)
    def _():
        state_ref[...] = jnp.zeros_like(state_ref)

    cols = rw_ref[0]
    r = cols[:, 0:W]
    k = cols[:, W:2 * W]
    v = cols[:, 2 * W:3 * W]
    o = 3 * W
    xw = cols[:, o:o + DECAY_LORA]
    xa = cols[:, o + DECAY_LORA:o + DECAY_LORA + ICLR_LORA]
    xg = cols[:, o + DECAY_LORA + ICLR_LORA:]

    w_log = -jax.nn.softplus(-(w0_ref[...] + _mm(jnp.tanh(xw), dup_ref[...]))) - 0.5
    lw = -jnp.exp(w_log)
    a = jax.nn.sigmoid(a0_ref[...] + _mm(xa, aup_ref[...]))
    g = _mm(jax.nn.sigmoid(xg), gup_ref[...])
    kkp = k * kk_ref[...]
    k_mod = k * (1.0 + (a - 1.0) * ka_ref[...])
    rkk = r * k_mod * rk_ref[...]

    cum = _cumsum_rows(lw)
    e_in = jnp.exp(cum)
    e_ex = jnp.exp(cum - lw)
    e_neg = jnp.exp(-cum)
    w_last = e_in[L - 1:L, :]

    rt_all = r * e_in
    kt_all = k_mod * e_neg

    row2 = lax.broadcasted_iota(jnp.int32, (2 * L, L), 0)
    col2 = lax.broadcasted_iota(jnp.int32, (2 * L, L), 1)
    lower2 = col2 < jnp.where(row2 < L, row2, row2 - L + 1)

    for h in range(RWKV_HEADS):
        sl = slice(h * HEAD_DIM, (h + 1) * HEAD_DIM)
        kk_h = kkp[:, sl]
        kk_h = kk_h / jnp.maximum(jnp.sqrt(jnp.sum(kk_h * kk_h, axis=-1, keepdims=True)), 1e-12)
        a_t = -kk_h * e_ex[:, sl]
        b_t = kk_h * a[:, sl] * e_neg[:, sl]
        r_t = rt_all[:, sl]
        k_t = kt_all[:, sl]
        v_h = v[:, sl]
        wl = w_last[:, sl]

        lhs2 = jnp.concatenate([a_t, r_t], axis=0)
        sb = jnp.where(lower2, _mm_nt(lhs2, b_t), 0.0)
        sk = jnp.where(lower2, _mm_nt(lhs2, k_t), 0.0)
        a_ab, a_rb = sb[0:L], sb[L:2 * L]
        a_ak, a_rk = sk[0:L], sk[L:2 * L]

        t_inv = _unit_lower_inverse(a_ab)
        a_hat = _mm(t_inv, a_t)
        u_loc = _mm(t_inv, _mm(a_ak, v_h))
        r_hat = r_t + _mm(a_rb, a_hat)
        y_loc = _mm(a_rb, u_loc) + _mm(a_rk, v_h)
        b_hat = b_t * wl
        k_hat = k_t * wl
        g_mat = _mm_tn(a_hat, b_hat)
        s_loc = _mm_tn(u_loc, b_hat) + _mm_tn(v_h, k_hat)

        s0 = state_ref[h]
        y = _mm_nt(r_hat, s0) + y_loc
        state_ref[h] = s0 * wl + _mm(s0, g_mat) + s_loc

        mu = jnp.mean(y, axis=-1, keepdims=True)
        yc = y - mu
        var = jnp.mean(yc * yc, axis=-1, keepdims=True)
        yn = yc * lax.rsqrt(var + GN_EPS) * lng_ref[:, sl] + lnb_ref[:, sl]
        bonus = jnp.sum(rkk[:, sl], axis=-1, keepdims=True) * v_h
        yacc_ref[:, sl] = yn + bonus

    y_ref[0] = (yacc_ref[...] * g).astype(BF16)


def _wkv(rw, decay_w0, decay_up, iclr_a0, iclr_up, gate_up, k_k, k_a, r_k, lnx_gain, lnx_bias):
    batch, seq, _ = rw.shape
    L = WKV_CHUNK
    W = RWKV_WIDTH
    const = lambda b, j: (0, 0)
    vec = pl.BlockSpec((1, W), const)
    return pl.pallas_call(
        _wkv_kernel,
        grid=(batch, seq // L),
        in_specs=[
            pl.BlockSpec((1, L, RWKV_SHIFT_WIDTH), lambda b, j: (b, j, 0)),
            vec,
            pl.BlockSpec((DECAY_LORA, W), const),
            vec,
            pl.BlockSpec((ICLR_LORA, W), const),
            pl.BlockSpec((GATE_LORA, W), const),
            vec, vec, vec, vec, vec,
        ],
        out_specs=pl.BlockSpec((1, L, W), lambda b, j: (b, j, 0)),
        out_shape=jax.ShapeDtypeStruct((batch, seq, W), BF16),
        scratch_shapes=[
            pltpu.VMEM((RWKV_HEADS, HEAD_DIM, HEAD_DIM), F32),
            pltpu.VMEM((L, W), F32),
        ],
        compiler_params=pltpu.CompilerParams(
            dimension_semantics=("parallel", "arbitrary")),
        name="wkv",
    )(rw, decay_w0.reshape(1, W), decay_up.astype(BF16), iclr_a0.reshape(1, W),
      iclr_up.astype(BF16), gate_up.astype(BF16), k_k.reshape(1, W), k_a.reshape(1, W),
      r_k.reshape(1, W), lnx_gain.reshape(1, W), lnx_bias.reshape(1, W))


def _head_pair_norm_rope(x, gain, cos, sin):
    lane = lax.broadcasted_iota(jnp.int32, x.shape, 1)
    lo = lane < HEAD_DIM
    sq = x * x
    ms_lo = jnp.sum(jnp.where(lo, sq, 0.0), axis=-1, keepdims=True)
    ms_hi = jnp.sum(jnp.where(lo, 0.0, sq), axis=-1, keepdims=True)
    ms = jnp.where(lo, ms_lo, ms_hi) * (1.0 / HEAD_DIM)
    xn = (x * lax.rsqrt(ms + RMS_EPS)) * gain
    first = (lane % HEAD_DIM) < ROPE_HALF
    partner = jnp.where(first,
                        pltpu.roll(xn, V7X_LANES - ROPE_HALF, axis=1),
                        pltpu.roll(xn, ROPE_HALF, axis=1))
    return xn * cos + partner * sin


def _attn_kernel(sink_ref, q_ref, kvc_ref, kvp_ref, cc_ref, sc_ref, cp_ref, sp_ref,
                 qg_ref, kg_ref, o_ref):
    blk = pl.program_id(1)
    cos_c, sin_c = cc_ref[0], sc_ref[0]
    cos_p, sin_p = cp_ref[0], sp_ref[0]

    kvc = kvc_ref[0].astype(F32)
    kvp = kvp_ref[0].astype(F32)
    k_cur = _head_pair_norm_rope(kvc[:, 0:ATTN_KV_WIDTH], kg_ref[...], cos_c, sin_c)
    k_prev = _head_pair_norm_rope(kvp[:, 0:ATTN_KV_WIDTH], kg_ref[...], cos_p, sin_p)
    k_band = jnp.concatenate([k_prev, k_cur], axis=0).astype(BF16)
    v_band = jnp.concatenate([kvp_ref[0][:, ATTN_KV_WIDTH:], kvc_ref[0][:, ATTN_KV_WIDTH:]],
                             axis=0)

    q_idx = lax.broadcasted_iota(jnp.int32, (BLOCK, 2 * BLOCK), 0)
    k_idx = lax.broadcasted_iota(jnp.int32, (BLOCK, 2 * BLOCK), 1)
    dist = q_idx + BLOCK - k_idx
    first_key = jnp.where(blk > 0, 0, BLOCK)
    valid = (dist >= 0) & (dist < WINDOW) & (k_idx >= first_key)

    q_all = q_ref[0].astype(F32)
    for pair in range(ATTN_Q_HEADS // 2):
        qn = _head_pair_norm_rope(q_all[:, pair * V7X_LANES:(pair + 1) * V7X_LANES],
                                  qg_ref[...], cos_c, sin_c)
        for sub in range(2):
            h = 2 * pair + sub
            hk = h // ATTN_GROUPS
            q_h = qn[:, sub * HEAD_DIM:(sub + 1) * HEAD_DIM]
            k_h = k_band[:, hk * HEAD_DIM:(hk + 1) * HEAD_DIM]
            v_h = v_band[:, hk * HEAD_DIM:(hk + 1) * HEAD_DIM]
            s = _mm_nt(q_h, k_h) * (HEAD_DIM ** -0.5)
            s = jnp.where(valid, s, NEG_INF)
            sink = sink_ref[h]
            m = jnp.maximum(jnp.max(s, axis=-1, keepdims=True), sink)
            e = jnp.exp(s - m)
            denom = jnp.sum(e, axis=-1, keepdims=True) + jnp.exp(sink - m)
            probs = e / denom
            o_ref[0, :, h * HEAD_DIM:(h + 1) * HEAD_DIM] = _mm(probs, v_h).astype(BF16)


def _attn(qkv, cos_t, sin_t, q_gain, k_gain, sinks):
    batch, seq, _ = qkv.shape
    nblk = seq // BLOCK
    cur = lambda b, n: (b, n, 0)
    prev = lambda b, n: (b, jnp.maximum(n - 1, 0), 0)
    kv_blk = ATTN_Q_WIDTH // (2 * ATTN_KV_WIDTH)
    gain2 = lambda gn: jnp.tile(gn.reshape(1, HEAD_DIM), (1, V7X_LANES // HEAD_DIM))
    return pl.pallas_call(
        _attn_kernel,
        grid=(batch, nblk),
        in_specs=[
            pl.BlockSpec(memory_space=pltpu.SMEM),
            pl.BlockSpec((1, BLOCK, ATTN_Q_WIDTH), cur),
            pl.BlockSpec((1, BLOCK, 2 * ATTN_KV_WIDTH), lambda b, n: (b, n, kv_blk)),
            pl.BlockSpec((1, BLOCK, 2 * ATTN_KV_WIDTH),
                         lambda b, n: (b, jnp.maximum(n - 1, 0), kv_blk)),
            pl.BlockSpec((1, BLOCK, V7X_LANES), cur),
            pl.BlockSpec((1, BLOCK, V7X_LANES), cur),
            pl.BlockSpec((1, BLOCK, V7X_LANES), prev),
            pl.BlockSpec((1, BLOCK, V7X_LANES), prev),
            pl.BlockSpec((1, V7X_LANES), lambda b, n: (0, 0)),
            pl.BlockSpec((1, V7X_LANES), lambda b, n: (0, 0)),
        ],
        out_specs=pl.BlockSpec((1, BLOCK, ATTN_Q_WIDTH), cur),
        out_shape=jax.ShapeDtypeStruct((batch, seq, ATTN_Q_WIDTH), BF16),
        compiler_params=pltpu.CompilerParams(
            dimension_semantics=("parallel", "arbitrary")),
        name="attn",
    )(sinks, qkv, qkv, qkv, cos_t, sin_t, cos_t, sin_t, gain2(q_gain), gain2(k_gain))


def _tail_kernel(x_ref, ada_ref, ya_ref, yb_ref, gt_ref, wa_ref, wb_ref, wo_ref, gain_ref,
                 w1_ref, w3_ref, w2_ref, o_ref):
    ada = ada_ref[0]
    gate1 = ada[:, 2 * D_MODEL:3 * D_MODEL]
    shift2 = ada[:, 3 * D_MODEL:4 * D_MODEL]
    scale2 = ada[:, 4 * D_MODEL:5 * D_MODEL]
    gate2 = ada[:, 5 * D_MODEL:6 * D_MODEL]

    ma = jnp.dot(ya_ref[0], wa_ref[...], preferred_element_type=F32)
    mb = jnp.dot(yb_ref[0], wb_ref[...], preferred_element_type=F32)
    gates = gt_ref[0].astype(F32)
    merged = gates[:, 0:D_MODEL] * ma + gates[:, D_MODEL:] * mb
    x1 = x_ref[0] + gate1 * jnp.dot(merged.astype(BF16), wo_ref[...], preferred_element_type=F32)

    inv = lax.rsqrt(jnp.mean(x1 * x1, axis=-1, keepdims=True) + RMS_EPS)
    h2 = ((x1 * inv) * gain_ref[...] * (1.0 + scale2) + shift2).astype(BF16)
    a1 = jnp.dot(h2, w1_ref[...], preferred_element_type=F32)
    a3 = jnp.dot(h2, w3_ref[...], preferred_element_type=F32)
    z = (jax.nn.silu(a1) * a3).astype(BF16)
    o_ref[0] = x1 + gate2 * jnp.dot(z, w2_ref[...], preferred_element_type=F32)


def _tail(x, ada3, ya, yb, gt, wa, wb, wo, gain2, w1, w3, w2):
    batch, seq, _ = x.shape
    tm = TAIL_ROWS
    d_ff = w1.shape[1]
    const = lambda b, j: (0, 0)
    rows = lambda width: pl.BlockSpec((1, tm, width), lambda b, j: (b, j, 0))
    weight = lambda shape: pl.BlockSpec(shape, const, pipeline_mode=pl.Buffered(1))
    return pl.pallas_call(
        _tail_kernel,
        grid=(batch, seq // tm),
        in_specs=[
            rows(D_MODEL),
            pl.BlockSpec((1, 1, 6 * D_MODEL), lambda b, j: (b, 0, 0)),
            rows(RWKV_WIDTH),
            rows(ATTN_Q_WIDTH),
            rows(GATE_WIDTH),
            weight((RWKV_WIDTH, D_MODEL)),
            weight((ATTN_Q_WIDTH, D_MODEL)),
            weight((D_MODEL, D_MODEL)),
            pl.BlockSpec((1, D_MODEL), const),
            weight((D_MODEL, d_ff)),
            weight((D_MODEL, d_ff)),
            weight((d_ff, D_MODEL)),
        ],
        out_specs=rows(D_MODEL),
        out_shape=jax.ShapeDtypeStruct((batch, seq, D_MODEL), F32),
        compiler_params=pltpu.CompilerParams(
            dimension_semantics=("parallel", "parallel"),
            vmem_limit_bytes=V7X_VMEM_LIMIT_BYTES),
        name="tail",
    )(x, ada3, ya, yb, gt, wa, wb, wo, gain2, w1, w3, w2)


def kernel(x, c, positions, ada_w, ada_b, norm1_gain, norm2_gain, w_in, tshift_mu, decay_w0,
           decay_up, iclr_a0, iclr_up, gate_up, k_k, k_a, r_k, lnx_gain, lnx_bias, q_norm_gain,
           k_norm_gain, attn_sinks, branch_gate_b, w_branch_a, w_branch_b, w_out, ffn_w1, ffn_w3,
           ffn_w2):
    depth = ada_w.shape[0]
    batch = x.shape[0]
    cos_t, sin_t = _rope_tables(positions)
    for l in range(depth):
        ada3 = _ada(c, ada_w[l], ada_b[l]).reshape(batch, 1, 6 * D_MODEL)
        rw, qkv, gt = _inproj(x, ada3, norm1_gain[l].reshape(1, D_MODEL), w_in[l].astype(BF16),
                              tshift_mu[l].reshape(1, RWKV_SHIFT_WIDTH),
                              branch_gate_b[l].reshape(1, GATE_WIDTH))
        ya = _wkv(rw, decay_w0[l], decay_up[l], iclr_a0[l], iclr_up[l], gate_up[l], k_k[l],
                  k_a[l], r_k[l], lnx_gain[l], lnx_bias[l])
        yb = _attn(qkv, cos_t, sin_t, q_norm_gain[l], k_norm_gain[l], attn_sinks[l])
        x = _tail(x, ada3, ya, yb, gt, w_branch_a[l].astype(BF16), w_branch_b[l].astype(BF16),
                  w_out[l].astype(BF16), norm2_gain[l].reshape(1, D_MODEL),
                  ffn_w1[l].astype(BF16), ffn_w3[l].astype(BF16), ffn_w2[l].astype(BF16))
    return x
```

```python
import functools

import jax
import jax.numpy as jnp
from jax import lax
from jax.experimental import pallas as pl
from jax.experimental.pallas import tpu as pltpu

F32 = jnp.float32
BF16 = jnp.bfloat16

D_MODEL = 1024
HEAD_DIM = 64
RWKV_HEADS = 8
RWKV_WIDTH = RWKV_HEADS * HEAD_DIM
DECAY_LORA = 64
ICLR_LORA = 64
GATE_LORA = 128
ATTN_Q_HEADS = 8
ATTN_KV_HEADS = 2
ATTN_GROUPS = ATTN_Q_HEADS // ATTN_KV_HEADS
ATTN_Q_WIDTH = ATTN_Q_HEADS * HEAD_DIM
ATTN_KV_WIDTH = ATTN_KV_HEADS * HEAD_DIM
WINDOW = 128
BLOCK = 128
ROPE_THETA = 500000.0
ROPE_DIM = HEAD_DIM // 4
ROPE_HALF = ROPE_DIM // 2
RMS_EPS = 1e-6
GN_EPS = 64e-5
NEG_INF = -1e30
RWKV_SHIFT_WIDTH = 3 * RWKV_WIDTH + DECAY_LORA + ICLR_LORA + GATE_LORA
QKV_WIDTH = ATTN_Q_WIDTH + 2 * ATTN_KV_WIDTH
GATE_WIDTH = 2 * D_MODEL

V7X_LANES = 128
V7X_VMEM_LIMIT_BYTES = 56 * 1024 * 1024

INPROJ_ROWS = 256
WKV_CHUNK = 64
WKV_ROWS = 128
TAIL_ROWS = 256
ROPE_ROWS = 512


def _mm(a, b):
    return jnp.dot(a.astype(BF16), b.astype(BF16), preferred_element_type=F32)


def _mm_nt(a, b):
    return lax.dot_general(a.astype(BF16), b.astype(BF16), (((1,), (1,)), ((), ())),
                           preferred_element_type=F32)


def _mm_tn(a, b):
    return lax.dot_general(a.astype(BF16), b.astype(BF16), (((0,), (0,)), ((), ())),
                           preferred_element_type=F32)


def _ada_kernel(c_ref, w_ref, b_ref, o_ref):
    o_ref[...] = jnp.dot(c_ref[...], w_ref[...], precision=lax.Precision.HIGHEST,
                         preferred_element_type=F32) + b_ref[...]


def _ada(c, ada_w, ada_b):
    batch = c.shape[0]
    n_out = ada_w.shape[1]
    return pl.pallas_call(
        _ada_kernel,
        grid=(n_out // D_MODEL,),
        in_specs=[
            pl.BlockSpec((batch, D_MODEL), lambda j: (0, 0)),
            pl.BlockSpec((D_MODEL, D_MODEL), lambda j: (0, j)),
            pl.BlockSpec((1, D_MODEL), lambda j: (0, j)),
        ],
        out_specs=pl.BlockSpec((batch, D_MODEL), lambda j: (0, j)),
        out_shape=jax.ShapeDtypeStruct((batch, n_out), F32),
        name="ada",
    )(c, ada_w, ada_b.reshape(1, n_out))


def _rope_kernel(pos_ref, freq_ref, sgn_ref, cos_ref, sin_ref):
    ang = pos_ref[0].astype(F32) * freq_ref[...]
    sgn = sgn_ref[...]
    cos_ref[0] = jnp.where(sgn == 0.0, 1.0, jnp.cos(ang))
    sin_ref[0] = jnp.sin(ang) * sgn


def _rope_tables(positions):
    batch, seq = positions.shape
    inv_freq = ROPE_THETA ** (-jnp.arange(ROPE_HALF, dtype=F32) / ROPE_HALF)
    lane = jnp.arange(V7X_LANES) % HEAD_DIM
    freq = jnp.where(lane < ROPE_DIM, inv_freq[lane % ROPE_HALF], 0.0).reshape(1, V7X_LANES)
    sgn = jnp.where(lane < ROPE_HALF, -1.0, jnp.where(lane < ROPE_DIM, 1.0, 0.0))
    sgn = sgn.astype(F32).reshape(1, V7X_LANES)
    tab = jax.ShapeDtypeStruct((batch, seq, V7X_LANES), F32)
    return pl.pallas_call(
        _rope_kernel,
        grid=(batch, seq // ROPE_ROWS),
        in_specs=[
            pl.BlockSpec((1, ROPE_ROWS, 1), lambda b, j: (b, j, 0)),
            pl.BlockSpec((1, V7X_LANES), lambda b, j: (0, 0)),
            pl.BlockSpec((1, V7X_LANES), lambda b, j: (0, 0)),
        ],
        out_specs=[
            pl.BlockSpec((1, ROPE_ROWS, V7X_LANES), lambda b, j: (b, j, 0)),
            pl.BlockSpec((1, ROPE_ROWS, V7X_LANES), lambda b, j: (b, j, 0)),
        ],
        out_shape=[tab, tab],
        name="rope",
    )(positions.reshape(batch, seq, 1), freq, sgn)


def _inproj_kernel(x_ref, ada_ref, gain_ref, w_ref, mu_ref, gb_ref,
                   rw_ref, qkv_ref, gt_ref, carry_ref):
    rows = x_ref.shape[1]
    x = x_ref[0]
    ada = ada_ref[0]
    shift1 = ada[:, 0:D_MODEL]
    scale1 = ada[:, D_MODEL:2 * D_MODEL]
    inv = lax.rsqrt(jnp.mean(x * x, axis=-1, keepdims=True) + RMS_EPS)
    h = ((x * inv) * gain_ref[...] * (1.0 + scale1) + shift1).astype(BF16)

    @pl.when(pl.program_id(1) == 0)
    def _():
        carry_ref[...] = jnp.zeros_like(carry_ref)

    p = jnp.dot(h, w_ref[:, 0:RWKV_SHIFT_WIDTH], preferred_element_type=F32)
    prev = pltpu.roll(p, 1, axis=0)
    row = lax.broadcasted_iota(jnp.int32, p.shape, 0)
    prev = jnp.where(row == 0, carry_ref[...], prev)
    carry_ref[...] = p[rows - 1:rows, :]
    rw_ref[0] = p + (prev - p) * mu_ref[...]

    qkv_ref[0] = jnp.dot(h, w_ref[:, RWKV_SHIFT_WIDTH:RWKV_SHIFT_WIDTH + QKV_WIDTH],
                         preferred_element_type=F32).astype(BF16)
    gl = jnp.dot(h, w_ref[:, RWKV_SHIFT_WIDTH + QKV_WIDTH:], preferred_element_type=F32)
    gt_ref[0] = jax.nn.sigmoid(gl + gb_ref[...]).astype(BF16)


def _inproj(x, ada3, gain, w_in_bf, mu, gate_b):
    batch, seq, _ = x.shape
    in_width = w_in_bf.shape[1]
    tm = INPROJ_ROWS
    const = lambda b, j: (0, 0)
    return pl.pallas_call(
        _inproj_kernel,
        grid=(batch, seq // tm),
        in_specs=[
            pl.BlockSpec((1, tm, D_MODEL), lambda b, j: (b, j, 0)),
            pl.BlockSpec((1, 1, 6 * D_MODEL), lambda b, j: (b, 0, 0)),
            pl.BlockSpec((1, D_MODEL), const),
            pl.BlockSpec((D_MODEL, in_width), const, pipeline_mode=pl.Buffered(1)),
            pl.BlockSpec((1, RWKV_SHIFT_WIDTH), const),
            pl.BlockSpec((1, GATE_WIDTH), const),
        ],
        out_specs=[
            pl.BlockSpec((1, tm, RWKV_SHIFT_WIDTH), lambda b, j: (b, j, 0)),
            pl.BlockSpec((1, tm, QKV_WIDTH), lambda b, j: (b, j, 0)),
            pl.BlockSpec((1, tm, GATE_WIDTH), lambda b, j: (b, j, 0)),
        ],
        out_shape=[
            jax.ShapeDtypeStruct((batch, seq, RWKV_SHIFT_WIDTH), F32),
            jax.ShapeDtypeStruct((batch, seq, QKV_WIDTH), BF16),
            jax.ShapeDtypeStruct((batch, seq, GATE_WIDTH), BF16),
        ],
        scratch_shapes=[pltpu.VMEM((1, RWKV_SHIFT_WIDTH), F32)],
        compiler_params=pltpu.CompilerParams(
            dimension_semantics=("parallel", "arbitrary"),
            vmem_limit_bytes=V7X_VMEM_LIMIT_BYTES),
        name="inproj",
    )(x, ada3, gain, w_in_bf, mu, gate_b)


def _cumsum_rows(x):
    n = x.shape[0]
    row = lax.broadcasted_iota(jnp.int32, x.shape, 0)
    s = 1
    while s < n:
        x = x + jnp.where(row >= s, pltpu.roll(x, s, axis=0), 0.0)
        s *= 2
    return x


def _wkv_kernel(rw_ref, w0_ref, dup_ref, a0_ref, aup_ref, gup_ref, kk_ref, ka_ref, rk_ref,
                lng_ref, lnb_ref, hsum_ref, y_ref, state_ref):
    L = WKV_CHUNK
    W = RWKV_WIDTH
    HD = HEAD_DIM
    n_chunks = rw_ref.shape[1] // L

    @pl.when(pl.program_id(1) == 0)
    def _():
        state_ref[...] = jnp.zeros_like(state_ref)

    cols = rw_ref[0]
    r = cols[:, 0:W]
    k = cols[:, W:2 * W]
    v = cols[:, 2 * W:3 * W]
    o = 3 * W
    xw = cols[:, o:o + DECAY_LORA]
    xa = cols[:, o + DECAY_LORA:o + DECAY_LORA + ICLR_LORA]
    xg = cols[:, o + DECAY_LORA + ICLR_LORA:]

    hsum = hsum_ref[...]
    w_log = -jax.nn.softplus(-(w0_ref[...] + _mm(jnp.tanh(xw), dup_ref[...]))) - 0.5
    lw = -jnp.exp(w_log)
    a = jax.nn.sigmoid(a0_ref[...] + _mm(xa, aup_ref[...]))
    g = _mm(jax.nn.sigmoid(xg), gup_ref[...])
    kkp = k * kk_ref[...]
    kk = kkp / jnp.maximum(jnp.sqrt(_mm(kkp * kkp, hsum) * float(HD)), 1e-12)
    k_mod = k * (1.0 + (a - 1.0) * ka_ref[...])
    bonus = _mm(r * k_mod * rk_ref[...], hsum) * float(HD) * v

    cum = jnp.concatenate([_cumsum_rows(lw[c * L:(c + 1) * L]) for c in range(n_chunks)], axis=0)
    e_in = jnp.exp(cum)
    e_neg = jnp.exp(-cum)
    at_all = -kk * jnp.exp(cum - lw)
    bt_all = kk * a * e_neg
    rt_all = r * e_in
    kt_all = k_mod * e_neg

    row2 = lax.broadcasted_iota(jnp.int32, (2 * L, 2 * L), 0)
    col2 = lax.broadcasted_iota(jnp.int32, (2 * L, 2 * L), 1) % L
    lower2 = col2 < jnp.where(row2 < L, row2, row2 - L + 1)
    lane3 = lax.broadcasted_iota(jnp.int32, (L, 3 * HD), 1)
    zeros_ll = jnp.zeros((L, HD), F32)

    units = [(c, h) for c in range(n_chunks) for h in range(RWKV_HEADS)]

    def pick(arr, c, h):
        return arr[c * L:(c + 1) * L, h * HD:(h + 1) * HD]

    a_t = {u: pick(at_all, *u) for u in units}
    r_t = {u: pick(rt_all, *u) for u in units}
    v_h = {u: pick(v, *u) for u in units}
    bk_t = {u: jnp.concatenate([pick(bt_all, *u), pick(kt_all, *u)], axis=0) for u in units}
    w_l = {(c, h): e_in[(c + 1) * L - 1:(c + 1) * L, h * HD:(h + 1) * HD] for c, h in units}

    sc = {u: jnp.where(lower2, _mm_nt(jnp.concatenate([a_t[u], r_t[u]], axis=0), bk_t[u]), 0.0)
          for u in units}
    top = {u: sc[u][0:L] for u in units}
    bot = {u: sc[u][L:2 * L] for u in units}
    akv = {u: _mm(top[u], jnp.concatenate([zeros_ll, v_h[u]], axis=0)) for u in units}

    wx = {u: jnp.concatenate([a_t[u], akv[u], top[u][:, 0:HD]], axis=1) for u in units}
    levels = L.bit_length() - 1
    for _ in range(levels):
        wx = {u: _mm(wx[u][:, 2 * HD:3 * HD], wx[u]) + jnp.where(lane3 < 2 * HD, wx[u], 0.0)
              for u in units}
    x2 = {u: jnp.concatenate([wx[u][:, 0:2 * HD],
                              jnp.concatenate([zeros_ll, v_h[u]], axis=1)], axis=0) for u in units}
    ry = {u: _mm(bot[u], x2[u]) for u in units}
    gs = {u: _mm_tn(x2[u], bk_t[u] * w_l[u]) for u in units}

    y_rows = []
    for c in range(n_chunks):
        y_heads = []
        for h in range(RWKV_HEADS):
            u = (c, h)
            s0 = state_ref[h]
            y_heads.append(_mm_nt(r_t[u] + ry[u][:, 0:HD], s0) + ry[u][:, HD:2 * HD])
            state_ref[h] = s0 * w_l[u] + _mm(s0, gs[u][0:HD]) + gs[u][HD:2 * HD]
        y_rows.append(jnp.concatenate(y_heads, axis=1))
    y = jnp.concatenate(y_rows, axis=0)

    yc = y - _mm(y, hsum)
    yn = yc * lax.rsqrt(_mm(yc * yc, hsum) + GN_EPS) * lng_ref[...] + lnb_ref[...]
    y_ref[0] = ((yn + bonus) * g).astype(BF16)


def _wkv(rw, decay_w0, decay_up, iclr_a0, iclr_up, gate_up, k_k, k_a, r_k, lnx_gain, lnx_bias):
    batch, seq, _ = rw.shape
    rows = WKV_ROWS
    W = RWKV_WIDTH
    const = lambda b, j: (0, 0)
    vec = pl.BlockSpec((1, W), const)
    head = jnp.arange(W) // HEAD_DIM
    hsum = jnp.where(head[:, None] == head[None, :], 1.0 / HEAD_DIM, 0.0).astype(BF16)
    return pl.pallas_call(
        _wkv_kernel,
        grid=(batch, seq // rows),
        in_specs=[
            pl.BlockSpec((1, rows, RWKV_SHIFT_WIDTH), lambda b, j: (b, j, 0)),
            vec,
            pl.BlockSpec((DECAY_LORA, W), const),
            vec,
            pl.BlockSpec((ICLR_LORA, W), const),
            pl.BlockSpec((GATE_LORA, W), const),
            vec, vec, vec, vec, vec,
            pl.BlockSpec((W, W), const),
        ],
        out_specs=pl.BlockSpec((1, rows, W), lambda b, j: (b, j, 0)),
        out_shape=jax.ShapeDtypeStruct((batch, seq, W), BF16),
        scratch_shapes=[pltpu.VMEM((RWKV_HEADS, HEAD_DIM, HEAD_DIM), F32)],
        compiler_params=pltpu.CompilerParams(
            dimension_semantics=("parallel", "arbitrary")),
        name="wkv",
    )(rw, decay_w0.reshape(1, W), decay_up.astype(BF16), iclr_a0.reshape(1, W),
      iclr_up.astype(BF16), gate_up.astype(BF16), k_k.reshape(1, W), k_a.reshape(1, W),
      r_k.reshape(1, W), lnx_gain.reshape(1, W), lnx_bias.reshape(1, W), hsum)


def _head_pair_norm_rope(x, gain, cos, sin):
    lane = lax.broadcasted_iota(jnp.int32, x.shape, 1)
    lo = lane < HEAD_DIM
    sq = x * x
    ms_lo = jnp.sum(jnp.where(lo, sq, 0.0), axis=-1, keepdims=True)
    ms_hi = jnp.sum(jnp.where(lo, 0.0, sq), axis=-1, keepdims=True)
    ms = jnp.where(lo, ms_lo, ms_hi) * (1.0 / HEAD_DIM)
    xn = (x * lax.rsqrt(ms + RMS_EPS)) * gain
    first = (lane % HEAD_DIM) < ROPE_HALF
    partner = jnp.where(first,
                        pltpu.roll(xn, V7X_LANES - ROPE_HALF, axis=1),
                        pltpu.roll(xn, ROPE_HALF, axis=1))
    return xn * cos + partner * sin


def _attn_kernel(sink_ref, q_ref, kvc_ref, kvp_ref, cc_ref, sc_ref, cp_ref, sp_ref,
                 qg_ref, kg_ref, o_ref):
    blk = pl.program_id(1)
    cos_c, sin_c = cc_ref[0], sc_ref[0]
    cos_p, sin_p = cp_ref[0], sp_ref[0]

    kvc = kvc_ref[0].astype(F32)
    kvp = kvp_ref[0].astype(F32)
    k_cur = _head_pair_norm_rope(kvc[:, 0:ATTN_KV_WIDTH], kg_ref[...], cos_c, sin_c)
    k_prev = _head_pair_norm_rope(kvp[:, 0:ATTN_KV_WIDTH], kg_ref[...], cos_p, sin_p)
    k_band = jnp.concatenate([k_prev, k_cur], axis=0).astype(BF16)
    v_band = jnp.concatenate([kvp_ref[0][:, ATTN_KV_WIDTH:], kvc_ref[0][:, ATTN_KV_WIDTH:]],
                             axis=0)

    q_idx = lax.broadcasted_iota(jnp.int32, (BLOCK, 2 * BLOCK), 0)
    k_idx = lax.broadcasted_iota(jnp.int32, (BLOCK, 2 * BLOCK), 1)
    dist = q_idx + BLOCK - k_idx
    first_key = jnp.where(blk > 0, 0, BLOCK)
    valid = (dist >= 0) & (dist < WINDOW) & (k_idx >= first_key)

    q_all = q_ref[0].astype(F32)
    for pair in range(ATTN_Q_HEADS // 2):
        qn = _head_pair_norm_rope(q_all[:, pair * V7X_LANES:(pair + 1) * V7X_LANES],
                                  qg_ref[...], cos_c, sin_c)
        for sub in range(2):
            h = 2 * pair + sub
            hk = h // ATTN_GROUPS
            q_h = qn[:, sub * HEAD_DIM:(sub + 1) * HEAD_DIM]
            k_h = k_band[:, hk * HEAD_DIM:(hk + 1) * HEAD_DIM]
            v_h = v_band[:, hk * HEAD_DIM:(hk + 1) * HEAD_DIM]
            s = _mm_nt(q_h, k_h) * (HEAD_DIM ** -0.5)
            s = jnp.where(valid, s, NEG_INF)
            sink = sink_ref[h]
            m = jnp.maximum(jnp.max(s, axis=-1, keepdims=True), sink)
            e = jnp.exp(s - m)
            denom = jnp.sum(e, axis=-1, keepdims=True) + jnp.exp(sink - m)
            probs = e / denom
            o_ref[0, :, h * HEAD_DIM:(h + 1) * HEAD_DIM] = _mm(probs, v_h).astype(BF16)


def _attn(qkv, cos_t, sin_t, q_gain, k_gain, sinks):
    batch, seq, _ = qkv.shape
    nblk = seq // BLOCK
    cur = lambda b, n: (b, n, 0)
    prev = lambda b, n: (b, jnp.maximum(n - 1, 0), 0)
    kv_blk = ATTN_Q_WIDTH // (2 * ATTN_KV_WIDTH)
    gain2 = lambda gn: jnp.tile(gn.reshape(1, HEAD_DIM), (1, V7X_LANES // HEAD_DIM))
    return pl.pallas_call(
        _attn_kernel,
        grid=(batch, nblk),
        in_specs=[
            pl.BlockSpec(memory_space=pltpu.SMEM),
            pl.BlockSpec((1, BLOCK, ATTN_Q_WIDTH), cur),
            pl.BlockSpec((1, BLOCK, 2 * ATTN_KV_WIDTH), lambda b, n: (b, n, kv_blk)),
            pl.BlockSpec((1, BLOCK, 2 * ATTN_KV_WIDTH),
                         lambda b, n: (b, jnp.maximum(n - 1, 0), kv_blk)),
            pl.BlockSpec((1, BLOCK, V7X_LANES), cur),
            pl.BlockSpec((1, BLOCK, V7X_LANES), cur),
            pl.BlockSpec((1, BLOCK, V7X_LANES), prev),
            pl.BlockSpec((1, BLOCK, V7X_LANES), prev),
            pl.BlockSpec((1, V7X_LANES), lambda b, n: (0, 0)),
            pl.BlockSpec((1, V7X_LANES), lambda b, n: (0, 0)),
        ],
        out_specs=pl.BlockSpec((1, BLOCK, ATTN_Q_WIDTH), cur),
        out_shape=jax.ShapeDtypeStruct((batch, seq, ATTN_Q_WIDTH), BF16),
        compiler_params=pltpu.CompilerParams(
            dimension_semantics=("parallel", "arbitrary")),
        name="attn",
    )(sinks, qkv, qkv, qkv, cos_t, sin_t, cos_t, sin_t, gain2(q_gain), gain2(k_gain))


def _tail_kernel(x_ref, ada_ref, ya_ref, yb_ref, gt_ref, wa_ref, wb_ref, wo_ref, gain_ref,
                 w1_ref, w3_ref, w2_ref, o_ref):
    ada = ada_ref[0]
    gate1 = ada[:, 2 * D_MODEL:3 * D_MODEL]
    shift2 = ada[:, 3 * D_MODEL:4 * D_MODEL]
    scale2 = ada[:, 4 * D_MODEL:5 * D_MODEL]
    gate2 = ada[:, 5 * D_MODEL:6 * D_MODEL]

    ma = jnp.dot(ya_ref[0], wa_ref[...], preferred_element_type=F32)
    mb = jnp.dot(yb_ref[0], wb_ref[...], preferred_element_type=F32)
    gates = gt_ref[0].astype(F32)
    merged = gates[:, 0:D_MODEL] * ma + gates[:, D_MODEL:] * mb
    x1 = x_ref[0] + gate1 * jnp.dot(merged.astype(BF16), wo_ref[...], preferred_element_type=F32)

    inv = lax.rsqrt(jnp.mean(x1 * x1, axis=-1, keepdims=True) + RMS_EPS)
    h2 = ((x1 * inv) * gain_ref[...] * (1.0 + scale2) + shift2).astype(BF16)
    a1 = jnp.dot(h2, w1_ref[...], preferred_element_type=F32)
    a3 = jnp.dot(h2, w3_ref[...], preferred_element_type=F32)
    z = (jax.nn.silu(a1) * a3).astype(BF16)
    o_ref[0] = x1 + gate2 * jnp.dot(z, w2_ref[...], preferred_element_type=F32)


def _tail(x, ada3, ya, yb, gt, wa, wb, wo, gain2, w1, w3, w2):
    batch, seq, _ = x.shape
    tm = TAIL_ROWS
    d_ff = w1.shape[1]
    const = lambda b, j: (0, 0)
    rows = lambda width: pl.BlockSpec((1, tm, width), lambda b, j: (b, j, 0))
    weight = lambda shape: pl.BlockSpec(shape, const, pipeline_mode=pl.Buffered(1))
    return pl.pallas_call(
        _tail_kernel,
        grid=(batch, seq // tm),
        in_specs=[
            rows(D_MODEL),
            pl.BlockSpec((1, 1, 6 * D_MODEL), lambda b, j: (b, 0, 0)),
            rows(RWKV_WIDTH),
            rows(ATTN_Q_WIDTH),
            rows(GATE_WIDTH),
            weight((RWKV_WIDTH, D_MODEL)),
            weight((ATTN_Q_WIDTH, D_MODEL)),
            weight((D_MODEL, D_MODEL)),
            pl.BlockSpec((1, D_MODEL), const),
            weight((D_MODEL, d_ff)),
            weight((D_MODEL, d_ff)),
            weight((d_ff, D_MODEL)),
        ],
        out_specs=rows(D_MODEL),
        out_shape=jax.ShapeDtypeStruct((batch, seq, D_MODEL), F32),
        compiler_params=pltpu.CompilerParams(
            dimension_semantics=("parallel", "parallel"),
            vmem_limit_bytes=V7X_VMEM_LIMIT_BYTES),
        name="tail",
    )(x, ada3, ya, yb, gt, wa, wb, wo, gain2, w1, w3, w2)


def kernel(x, c, positions, ada_w, ada_b, norm1_gain, norm2_gain, w_in, tshift_mu, decay_w0,
           decay_up, iclr_a0, iclr_up, gate_up, k_k, k_a, r_k, lnx_gain, lnx_bias, q_norm_gain,
           k_norm_gain, attn_sinks, branch_gate_b, w_branch_a, w_branch_b, w_out, ffn_w1, ffn_w3,
           ffn_w2):
    depth = ada_w.shape[0]
    batch = x.shape[0]
    cos_t, sin_t = _rope_tables(positions)
    for l in range(depth):
        ada3 = _ada(c, ada_w[l], ada_b[l]).reshape(batch, 1, 6 * D_MODEL)
        rw, qkv, gt = _inproj(x, ada3, norm1_gain[l].reshape(1, D_MODEL), w_in[l].astype(BF16),
                              tshift_mu[l].reshape(1, RWKV_SHIFT_WIDTH),
                              branch_gate_b[l].reshape(1, GATE_WIDTH))
        ya = _wkv(rw, decay_w0[l], decay_up[l], iclr_a0[l], iclr_up[l], gate_up[l], k_k[l],
                  k_a[l], r_k[l], lnx_gain[l], lnx_bias[l])
        yb = _attn(qkv, cos_t, sin_t, q_norm_gain[l], k_norm_gain[l], attn_sinks[l])
        x = _tail(x, ada3, ya, yb, gt, w_branch_a[l].astype(BF16), w_branch_b[l].astype(BF16),
                  w_out[l].astype(BF16), norm2_gain[l].reshape(1, D_MODEL),
                  ffn_w1[l].astype(BF16), ffn_w3[l].astype(BF16), ffn_w2[l].astype(BF16))
    return x
```

```python
import functools

import jax
import jax.numpy as jnp
from jax import lax
from jax.experimental import pallas as pl
from jax.experimental.pallas import tpu as pltpu

F32 = jnp.float32
BF16 = jnp.bfloat16

D_MODEL = 1024
HEAD_DIM = 64
RWKV_HEADS = 8
RWKV_WIDTH = RWKV_HEADS * HEAD_DIM
DECAY_LORA = 64
ICLR_LORA = 64
GATE_LORA = 128
ATTN_Q_HEADS = 8
ATTN_KV_HEADS = 2
ATTN_GROUPS = ATTN_Q_HEADS // ATTN_KV_HEADS
ATTN_Q_WIDTH = ATTN_Q_HEADS * HEAD_DIM
ATTN_KV_WIDTH = ATTN_KV_HEADS * HEAD_DIM
WINDOW = 128
BLOCK = 128
ROPE_THETA = 500000.0
ROPE_DIM = HEAD_DIM // 4
ROPE_HALF = ROPE_DIM // 2
RMS_EPS = 1e-6
GN_EPS = 64e-5
NEG_INF = -1e30
RWKV_SHIFT_WIDTH = 3 * RWKV_WIDTH + DECAY_LORA + ICLR_LORA + GATE_LORA
QKV_WIDTH = ATTN_Q_WIDTH + 2 * ATTN_KV_WIDTH
GATE_WIDTH = 2 * D_MODEL

V7X_LANES = 128
V7X_VMEM_LIMIT_BYTES = 56 * 1024 * 1024

INPROJ_ROWS = 256
WKV_CHUNK = 64
WKV_ROWS = 128
ATTN_ROWS = 256
TAIL_ROWS = 256


def _mm(a, b):
    return jnp.dot(a.astype(BF16), b.astype(BF16), preferred_element_type=F32)


def _mm_nt(a, b):
    return lax.dot_general(a.astype(BF16), b.astype(BF16), (((1,), (1,)), ((), ())),
                           preferred_element_type=F32)


def _mm_tn(a, b):
    return lax.dot_general(a.astype(BF16), b.astype(BF16), (((0,), (0,)), ((), ())),
                           preferred_element_type=F32)


def _ada_kernel(c_ref, w_ref, b_ref, o_ref):
    o_ref[...] = jnp.dot(c_ref[...], w_ref[...], precision=lax.Precision.HIGHEST,
                         preferred_element_type=F32) + b_ref[...]


def _ada(c, ada_w, ada_b):
    batch = c.shape[0]
    n_out = ada_w.shape[1]
    return pl.pallas_call(
        _ada_kernel,
        grid=(n_out // D_MODEL,),
        in_specs=[
            pl.BlockSpec((batch, D_MODEL), lambda j: (0, 0)),
            pl.BlockSpec((D_MODEL, D_MODEL), lambda j: (0, j)),
            pl.BlockSpec((1, D_MODEL), lambda j: (0, j)),
        ],
        out_specs=pl.BlockSpec((batch, D_MODEL), lambda j: (0, j)),
        out_shape=jax.ShapeDtypeStruct((batch, n_out), F32),
        name="ada",
    )(c, ada_w, ada_b.reshape(1, n_out))


def _rope_kernel(pos_ref, freq_ref, sgn_ref, cos_ref, sin_ref):
    ang = pos_ref[0].astype(F32) * freq_ref[...]
    cos_ref[0] = jnp.cos(ang)
    sin_ref[0] = jnp.sin(ang) * sgn_ref[...]


def _rope_tables(positions):
    batch, seq = positions.shape
    per_row = V7X_LANES // ROPE_DIM
    inv_freq = ROPE_THETA ** (-jnp.arange(ROPE_HALF, dtype=F32) / ROPE_HALF)
    dim = jnp.arange(V7X_LANES) % ROPE_DIM
    freq = inv_freq[dim % ROPE_HALF].reshape(1, V7X_LANES)
    sgn = jnp.where(dim < ROPE_HALF, -1.0, 1.0).astype(F32).reshape(1, V7X_LANES)
    pos = jnp.repeat(positions.reshape(batch, seq // per_row, per_row), ROPE_DIM, axis=-1)
    dense = jax.ShapeDtypeStruct((batch, seq // per_row, V7X_LANES), F32)
    row_spec = pl.BlockSpec((1, seq // per_row, V7X_LANES), lambda b: (b, 0, 0))
    vec_spec = pl.BlockSpec((1, V7X_LANES), lambda b: (0, 0))
    cos_d, sin_d = pl.pallas_call(
        _rope_kernel,
        grid=(batch,),
        in_specs=[row_spec, vec_spec, vec_spec],
        out_specs=[row_spec, row_spec],
        out_shape=[dense, dense],
        name="rope",
    )(pos, freq, sgn)
    cos_r = cos_d.reshape(batch, seq, ROPE_DIM)
    sin_r = sin_d.reshape(batch, seq, ROPE_DIM)
    rest = HEAD_DIM - ROPE_DIM
    one = jnp.ones((batch, seq, rest), F32)
    zero = jnp.zeros((batch, seq, rest), F32)
    heads = V7X_LANES // HEAD_DIM
    return jnp.concatenate([cos_r, one] * heads + [sin_r, zero] * heads, axis=-1)


def _inproj_kernel(x_ref, ada_ref, gain_ref, w_ref, mu_ref, gb_ref,
                   rw_ref, qkv_ref, gt_ref, carry_ref):
    rows = x_ref.shape[1]
    x = x_ref[0]
    ada = ada_ref[0]
    shift1 = ada[:, 0:D_MODEL]
    scale1 = ada[:, D_MODEL:2 * D_MODEL]
    inv = lax.rsqrt(jnp.mean(x * x, axis=-1, keepdims=True) + RMS_EPS)
    h = ((x * inv) * gain_ref[...] * (1.0 + scale1) + shift1).astype(BF16)

    @pl.when(pl.program_id(1) == 0)
    def _():
        carry_ref[...] = jnp.zeros_like(carry_ref)

    p = jnp.dot(h, w_ref[:, 0:RWKV_SHIFT_WIDTH], preferred_element_type=F32)
    prev = pltpu.roll(p, 1, axis=0)
    row = lax.broadcasted_iota(jnp.int32, p.shape, 0)
    prev = jnp.where(row == 0, carry_ref[...], prev)
    carry_ref[...] = p[rows - 1:rows, :]
    rw_ref[0] = p + (prev - p) * mu_ref[...]

    qkv_ref[0] = jnp.dot(h, w_ref[:, RWKV_SHIFT_WIDTH:RWKV_SHIFT_WIDTH + QKV_WIDTH],
                         preferred_element_type=F32).astype(BF16)
    gl = jnp.dot(h, w_ref[:, RWKV_SHIFT_WIDTH + QKV_WIDTH:], preferred_element_type=F32)
    gt_ref[0] = jax.nn.sigmoid(gl + gb_ref[...]).astype(BF16)


def _inproj(x, ada3, gain, w_in_bf, mu, gate_b):
    batch, seq, _ = x.shape
    in_width = w_in_bf.shape[1]
    tm = INPROJ_ROWS
    const = lambda b, j: (0, 0)
    return pl.pallas_call(
        _inproj_kernel,
        grid=(batch, seq // tm),
        in_specs=[
            pl.BlockSpec((1, tm, D_MODEL), lambda b, j: (b, j, 0)),
            pl.BlockSpec((1, 1, 6 * D_MODEL), lambda b, j: (b, 0, 0)),
            pl.BlockSpec((1, D_MODEL), const),
            pl.BlockSpec((D_MODEL, in_width), const, pipeline_mode=pl.Buffered(1)),
            pl.BlockSpec((1, RWKV_SHIFT_WIDTH), const),
            pl.BlockSpec((1, GATE_WIDTH), const),
        ],
        out_specs=[
            pl.BlockSpec((1, tm, RWKV_SHIFT_WIDTH), lambda b, j: (b, j, 0)),
            pl.BlockSpec((1, tm, QKV_WIDTH), lambda b, j: (b, j, 0)),
            pl.BlockSpec((1, tm, GATE_WIDTH), lambda b, j: (b, j, 0)),
        ],
        out_shape=[
            jax.ShapeDtypeStruct((batch, seq, RWKV_SHIFT_WIDTH), F32),
            jax.ShapeDtypeStruct((batch, seq, QKV_WIDTH), BF16),
            jax.ShapeDtypeStruct((batch, seq, GATE_WIDTH), BF16),
        ],
        scratch_shapes=[pltpu.VMEM((1, RWKV_SHIFT_WIDTH), F32)],
        compiler_params=pltpu.CompilerParams(
            dimension_semantics=("parallel", "arbitrary"),
            vmem_limit_bytes=V7X_VMEM_LIMIT_BYTES),
        name="inproj",
    )(x, ada3, gain, w_in_bf, mu, gate_b)


def _cumsum_rows(x):
    n = x.shape[0]
    row = lax.broadcasted_iota(jnp.int32, x.shape, 0)
    s = 1
    while s < n:
        x = x + jnp.where(row >= s, pltpu.roll(x, s, axis=0), 0.0)
        s *= 2
    return x


def _wkv_kernel(rw_ref, w0_ref, dup_ref, a0_ref, aup_ref, gup_ref, kk_ref, ka_ref, rk_ref,
                lng_ref, lnb_ref, hsum_ref, y_ref, state_ref):
    L = WKV_CHUNK
    W = RWKV_WIDTH
    HD = HEAD_DIM
    n_chunks = rw_ref.shape[1] // L

    @pl.when(pl.program_id(1) == 0)
    def _():
        state_ref[...] = jnp.zeros_like(state_ref)

    cols = rw_ref[0]
    r = cols[:, 0:W]
    k = cols[:, W:2 * W]
    v = cols[:, 2 * W:3 * W]
    o = 3 * W
    xw = cols[:, o:o + DECAY_LORA]
    xa = cols[:, o + DECAY_LORA:o + DECAY_LORA + ICLR_LORA]
    xg = cols[:, o + DECAY_LORA + ICLR_LORA:]

    hsum = hsum_ref[...]
    w_log = -jax.nn.softplus(-(w0_ref[...] + _mm(jnp.tanh(xw), dup_ref[...]))) - 0.5
    lw = -jnp.exp(w_log)
    a = jax.nn.sigmoid(a0_ref[...] + _mm(xa, aup_ref[...]))
    g = _mm(jax.nn.sigmoid(xg), gup_ref[...])
    kkp = k * kk_ref[...]
    kk = kkp / jnp.maximum(jnp.sqrt(_mm(kkp * kkp, hsum) * float(HD)), 1e-12)
    k_mod = k * (1.0 + (a - 1.0) * ka_ref[...])
    bonus = _mm(r * k_mod * rk_ref[...], hsum) * float(HD) * v

    cum = jnp.concatenate([_cumsum_rows(lw[c * L:(c + 1) * L]) for c in range(n_chunks)], axis=0)
    e_in = jnp.exp(cum)
    e_neg = jnp.exp(-cum)
    at_all = -kk * jnp.exp(cum - lw)
    bt_all = kk * a * e_neg
    rt_all = r * e_in
    kt_all = k_mod * e_neg

    row2 = lax.broadcasted_iota(jnp.int32, (2 * L, 2 * L), 0)
    col2 = lax.broadcasted_iota(jnp.int32, (2 * L, 2 * L), 1) % L
    lower2 = col2 < jnp.where(row2 < L, row2, row2 - L + 1)
    lane3 = lax.broadcasted_iota(jnp.int32, (L, 3 * HD), 1)
    zeros_ll = jnp.zeros((L, HD), F32)

    units = [(c, h) for c in range(n_chunks) for h in range(RWKV_HEADS)]

    def pick(arr, c, h):
        return arr[c * L:(c + 1) * L, h * HD:(h + 1) * HD]

    a_t = {u: pick(at_all, *u) for u in units}
    r_t = {u: pick(rt_all, *u) for u in units}
    v_h = {u: pick(v, *u) for u in units}
    bk_t = {u: jnp.concatenate([pick(bt_all, *u), pick(kt_all, *u)], axis=0) for u in units}
    w_l = {(c, h): e_in[(c + 1) * L - 1:(c + 1) * L, h * HD:(h + 1) * HD] for c, h in units}

    sc = {u: jnp.where(lower2, _mm_nt(jnp.concatenate([a_t[u], r_t[u]], axis=0), bk_t[u]), 0.0)
          for u in units}
    top = {u: sc[u][0:L] for u in units}
    bot = {u: sc[u][L:2 * L] for u in units}
    akv = {u: _mm(top[u], jnp.concatenate([zeros_ll, v_h[u]], axis=0)) for u in units}

    wx = {u: jnp.concatenate([a_t[u], akv[u], top[u][:, 0:HD]], axis=1) for u in units}
    levels = L.bit_length() - 1
    for _ in range(levels):
        wx = {u: _mm(wx[u][:, 2 * HD:3 * HD], wx[u]) + jnp.where(lane3 < 2 * HD, wx[u], 0.0)
              for u in units}
    x2 = {u: jnp.concatenate([wx[u][:, 0:2 * HD],
                              jnp.concatenate([zeros_ll, v_h[u]], axis=1)], axis=0) for u in units}
    ry = {u: _mm(bot[u], x2[u]) for u in units}
    gs = {u: _mm_tn(x2[u], bk_t[u] * w_l[u]) for u in units}

    y_rows = []
    for c in range(n_chunks):
        y_heads = []
        for h in range(RWKV_HEADS):
            u = (c, h)
            s0 = state_ref[h]
            y_heads.append(_mm_nt(r_t[u] + ry[u][:, 0:HD], s0) + ry[u][:, HD:2 * HD])
            state_ref[h] = s0 * w_l[u] + _mm(s0, gs[u][0:HD]) + gs[u][HD:2 * HD]
        y_rows.append(jnp.concatenate(y_heads, axis=1))
    y = jnp.concatenate(y_rows, axis=0)

    yc = y - _mm(y, hsum)
    yn = yc * lax.rsqrt(_mm(yc * yc, hsum) + GN_EPS) * lng_ref[...] + lnb_ref[...]
    y_ref[0] = ((yn + bonus) * g).astype(BF16)


def _wkv(rw, decay_w0, decay_up, iclr_a0, iclr_up, gate_up, k_k, k_a, r_k, lnx_gain, lnx_bias):
    batch, seq, _ = rw.shape
    rows = WKV_ROWS
    W = RWKV_WIDTH
    const = lambda b, j: (0, 0)
    vec = pl.BlockSpec((1, W), const)
    head = jnp.arange(W) // HEAD_DIM
    hsum = jnp.where(head[:, None] == head[None, :], 1.0 / HEAD_DIM, 0.0).astype(BF16)
    return pl.pallas_call(
        _wkv_kernel,
        grid=(batch, seq // rows),
        in_specs=[
            pl.BlockSpec((1, rows, RWKV_SHIFT_WIDTH), lambda b, j: (b, j, 0)),
            vec,
            pl.BlockSpec((DECAY_LORA, W), const),
            vec,
            pl.BlockSpec((ICLR_LORA, W), const),
            pl.BlockSpec((GATE_LORA, W), const),
            vec, vec, vec, vec, vec,
            pl.BlockSpec((W, W), const),
        ],
        out_specs=pl.BlockSpec((1, rows, W), lambda b, j: (b, j, 0)),
        out_shape=jax.ShapeDtypeStruct((batch, seq, W), BF16),
        scratch_shapes=[pltpu.VMEM((RWKV_HEADS, HEAD_DIM, HEAD_DIM), F32)],
        compiler_params=pltpu.CompilerParams(
            dimension_semantics=("parallel", "arbitrary")),
        name="wkv",
    )(rw, decay_w0.reshape(1, W), decay_up.astype(BF16), iclr_a0.reshape(1, W),
      iclr_up.astype(BF16), gate_up.astype(BF16), k_k.reshape(1, W), k_a.reshape(1, W),
      r_k.reshape(1, W), lnx_gain.reshape(1, W), lnx_bias.reshape(1, W), hsum)


def _attn_kernel(sink_ref, q_ref, kv_ref, tab_ref, bias_ref, qg_ref, kg_ref, hmean_ref, perm_ref,
                 o_ref, kprev_ref, vprev_ref):
    n_blk = q_ref.shape[1] // BLOCK
    n_slab = ATTN_Q_WIDTH // V7X_LANES
    slab_per_kv = n_slab // ATTN_KV_HEADS
    kvs = range(ATTN_KV_HEADS)
    pars = range(2 * ATTN_KV_HEADS)

    @pl.when(pl.program_id(1) == 0)
    def _():
        kprev_ref[...] = jnp.zeros_like(kprev_ref)
        vprev_ref[...] = jnp.zeros_like(vprev_ref)

    cos, sin = tab_ref[0, :, 0:V7X_LANES], tab_ref[0, :, V7X_LANES:]
    q_gain = qg_ref[...] * (HEAD_DIM ** -0.5)
    q_all = q_ref[0].astype(F32)
    kv = kv_ref[0].astype(F32)
    slabs = [q_all[:, s * V7X_LANES:(s + 1) * V7X_LANES] for s in range(n_slab)]
    slabs.append(kv[:, 0:ATTN_KV_WIDTH])
    gains = [q_gain] * n_slab + [kg_ref[...]]
    n_rows = n_blk * BLOCK

    ms_all = _mm(jnp.concatenate([x * x for x in slabs], axis=0), hmean_ref[...])
    xn = [x * lax.rsqrt(ms_all[i * n_rows:(i + 1) * n_rows] + RMS_EPS) * gains[i]
          for i, x in enumerate(slabs)]
    partner = _mm(jnp.concatenate(xn, axis=0), perm_ref[0])
    normed = [xn[i] * cos + partner[i * n_rows:(i + 1) * n_rows] * sin for i in range(len(slabs))]
    qn, k_cur = normed[:n_slab], normed[n_slab]
    v_cur = kv[:, ATTN_KV_WIDTH:]
    swapped = _mm(jnp.concatenate([k_cur, v_cur], axis=0), perm_ref[1])
    k_swap, v_swap = swapped[0:n_rows], swapped[n_rows:]
    lo = lax.broadcasted_iota(jnp.int32, (n_rows, V7X_LANES), 1) < HEAD_DIM
    kdup_cur = [jnp.where(lo, k_cur, k_swap).astype(BF16), jnp.where(lo, k_swap, k_cur).astype(BF16)]
    vpar_cur = [jnp.where(lo, v_cur, 0.0).astype(BF16), jnp.where(lo, 0.0, v_swap).astype(BF16),
                jnp.where(lo, v_swap, 0.0).astype(BF16), jnp.where(lo, 0.0, v_cur).astype(BF16)]

    def band(prev_ref, cur, j, i):
        if i == 0:
            return jnp.concatenate([prev_ref[j], cur[j][0:BLOCK]], axis=0)
        return cur[j][(i - 1) * BLOCK:(i + 1) * BLOCK]

    units = [(i, hk) for i in range(n_blk) for hk in kvs]
    kband = {(i, hk): band(kprev_ref, kdup_cur, hk, i) for i, hk in units}
    ones_b = jnp.ones((2 * BLOCK, V7X_LANES), BF16)
    vaug = {(i, j): jnp.concatenate([band(vprev_ref, vpar_cur, j, i), ones_b], axis=1)
            for i in range(n_blk) for j in pars}
    for hk in kvs:
        kprev_ref[hk] = kdup_cur[hk][n_rows - BLOCK:]
    for j in pars:
        vprev_ref[j] = vpar_cur[j][n_rows - BLOCK:]

    lo1 = lax.broadcasted_iota(jnp.int32, (BLOCK, V7X_LANES), 1) < HEAD_DIM
    stack = 2 * slab_per_kv
    first_bias = bias_ref[jnp.minimum(pl.program_id(1), 1)]
    bias = [jnp.concatenate([first_bias if i == 0 else bias_ref[1]] * stack, axis=0)
            for i in range(n_blk)]
    heads = [[2 * (hk * slab_per_kv + j) + p for p in range(2) for j in range(slab_per_kv)]
             for hk in kvs]
    lhs = {}
    for i, hk in units:
        mine = [qn[hk * slab_per_kv + j][i * BLOCK:(i + 1) * BLOCK] for j in range(slab_per_kv)]
        lhs[(i, hk)] = jnp.concatenate([jnp.where(lo1, x, 0.0) for x in mine]
                                       + [jnp.where(lo1, 0.0, x) for x in mine], axis=0).astype(BF16)
    s = {u: _mm_nt(lhs[u], kband[u]) + bias[u[0]] for u in units}
    rmax = {u: jnp.max(s[u], axis=-1, keepdims=True) for u in units}
    m = {(u, t): jnp.maximum(rmax[u][t * BLOCK:(t + 1) * BLOCK], sink_ref[heads[u[1]][t]])
         for u in units for t in range(stack)}
    e = {u: jnp.concatenate([jnp.exp(s[u][t * BLOCK:(t + 1) * BLOCK] - m[(u, t)])
                             for t in range(stack)], axis=0).astype(BF16) for u in units}
    half = stack * BLOCK // 2
    pv = {(i, hk, p): _mm(e[(i, hk)][p * half:(p + 1) * half], vaug[(i, 2 * hk + p)])
          for i, hk in units for p in range(2)}
    extra = {(u, t): jnp.exp(sink_ref[heads[u[1]][t]] - m[(u, t)]) for u in units for t in range(stack)}
    for i, hk in units:
        for j in range(slab_per_kv):
            r0 = slice(j * BLOCK, (j + 1) * BLOCK)
            even, odd = pv[(i, hk, 0)][r0], pv[(i, hk, 1)][r0]
            num = even[:, 0:V7X_LANES] + odd[:, 0:V7X_LANES]
            den = jnp.where(lo1, even[:, V7X_LANES:] + extra[((i, hk), j)],
                            odd[:, V7X_LANES:] + extra[((i, hk), slab_per_kv + j)])
            slab = hk * slab_per_kv + j
            o_ref[0, i * BLOCK:(i + 1) * BLOCK, slab * V7X_LANES:(slab + 1) * V7X_LANES] = (
                num / den).astype(BF16)


def _attn(qkv, rope_tab, q_gain, k_gain, sinks):
    batch, seq, _ = qkv.shape
    rows = ATTN_ROWS
    cur = lambda b, n: (b, n, 0)
    const = lambda b, n: (0, 0)
    const3 = lambda b, n: (0, 0, 0)
    kv_blk = ATTN_Q_WIDTH // (2 * ATTN_KV_WIDTH)
    gain2 = lambda gn: jnp.tile(gn.reshape(1, HEAD_DIM), (1, V7X_LANES // HEAD_DIM))
    lane = jnp.arange(V7X_LANES)
    head = lane // HEAD_DIM
    hmean = jnp.where(head[:, None] == head[None, :], 1.0 / HEAD_DIM, 0.0).astype(BF16)
    dim = lane % HEAD_DIM
    src = jnp.where(dim < ROPE_HALF, lane + ROPE_HALF, jnp.where(dim < ROPE_DIM, lane - ROPE_HALF, -1))
    partner_p = lane[:, None] == src[None, :]
    swap_p = lane[:, None] == ((lane + HEAD_DIM) % V7X_LANES)[None, :]
    perm = jnp.stack([partner_p, swap_p]).astype(BF16)
    dist = jnp.arange(BLOCK)[:, None] + BLOCK - jnp.arange(2 * BLOCK)[None, :]
    in_band = (dist >= 0) & (dist < WINDOW)
    own = (jnp.arange(2 * BLOCK) >= BLOCK)[None, :]
    bias = jnp.where(jnp.stack([in_band & own, in_band]), 0.0, NEG_INF).astype(F32)
    return pl.pallas_call(
        _attn_kernel,
        grid=(batch, seq // rows),
        in_specs=[
            pl.BlockSpec(memory_space=pltpu.SMEM),
            pl.BlockSpec((1, rows, ATTN_Q_WIDTH), cur),
            pl.BlockSpec((1, rows, 2 * ATTN_KV_WIDTH), lambda b, n: (b, n, kv_blk)),
            pl.BlockSpec((1, rows, 2 * V7X_LANES), cur),
            pl.BlockSpec((2, BLOCK, 2 * BLOCK), const3),
            pl.BlockSpec((1, V7X_LANES), const),
            pl.BlockSpec((1, V7X_LANES), const),
            pl.BlockSpec((V7X_LANES, V7X_LANES), const),
            pl.BlockSpec((2, V7X_LANES, V7X_LANES), const3),
        ],
        out_specs=pl.BlockSpec((1, rows, ATTN_Q_WIDTH), cur),
        out_shape=jax.ShapeDtypeStruct((batch, seq, ATTN_Q_WIDTH), BF16),
        scratch_shapes=[
            pltpu.VMEM((ATTN_KV_HEADS, BLOCK, V7X_LANES), BF16),
            pltpu.VMEM((2 * ATTN_KV_HEADS, BLOCK, V7X_LANES), BF16),
        ],
        compiler_params=pltpu.CompilerParams(
            dimension_semantics=("parallel", "arbitrary")),
        name="attn",
    )(sinks, qkv, qkv, rope_tab, bias, gain2(q_gain), gain2(k_gain), hmean, perm)


def _tail_kernel(x_ref, ada_ref, ya_ref, yb_ref, gt_ref, wa_ref, wb_ref, wo_ref, gain_ref,
                 w1_ref, w3_ref, w2_ref, o_ref):
    ada = ada_ref[0]
    gate1 = ada[:, 2 * D_MODEL:3 * D_MODEL]
    shift2 = ada[:, 3 * D_MODEL:4 * D_MODEL]
    scale2 = ada[:, 4 * D_MODEL:5 * D_MODEL]
    gate2 = ada[:, 5 * D_MODEL:6 * D_MODEL]

    ma = jnp.dot(ya_ref[0], wa_ref[...], preferred_element_type=F32)
    mb = jnp.dot(yb_ref[0], wb_ref[...], preferred_element_type=F32)
    gates = gt_ref[0].astype(F32)
    merged = gates[:, 0:D_MODEL] * ma + gates[:, D_MODEL:] * mb
    x1 = x_ref[0] + gate1 * jnp.dot(merged.astype(BF16), wo_ref[...], preferred_element_type=F32)

    inv = lax.rsqrt(jnp.mean(x1 * x1, axis=-1, keepdims=True) + RMS_EPS)
    h2 = ((x1 * inv) * gain_ref[...] * (1.0 + scale2) + shift2).astype(BF16)
    a1 = jnp.dot(h2, w1_ref[...], preferred_element_type=F32)
    a3 = jnp.dot(h2, w3_ref[...], preferred_element_type=F32)
    z = (jax.nn.silu(a1) * a3).astype(BF16)
    o_ref[0] = x1 + gate2 * jnp.dot(z, w2_ref[...], preferred_element_type=F32)


def _tail(x, ada3, ya, yb, gt, wa, wb, wo, gain2, w1, w3, w2):
    batch, seq, _ = x.shape
    tm = TAIL_ROWS
    d_ff = w1.shape[1]
    const = lambda b, j: (0, 0)
    rows = lambda width: pl.BlockSpec((1, tm, width), lambda b, j: (b, j, 0))
    weight = lambda shape: pl.BlockSpec(shape, const, pipeline_mode=pl.Buffered(1))
    return pl.pallas_call(
        _tail_kernel,
        grid=(batch, seq // tm),
        in_specs=[
            rows(D_MODEL),
            pl.BlockSpec((1, 1, 6 * D_MODEL), lambda b, j: (b, 0, 0)),
            rows(RWKV_WIDTH),
            rows(ATTN_Q_WIDTH),
            rows(GATE_WIDTH),
            weight((RWKV_WIDTH, D_MODEL)),
            weight((ATTN_Q_WIDTH, D_MODEL)),
            weight((D_MODEL, D_MODEL)),
            pl.BlockSpec((1, D_MODEL), const),
            weight((D_MODEL, d_ff)),
            weight((D_MODEL, d_ff)),
            weight((d_ff, D_MODEL)),
        ],
        out_specs=rows(D_MODEL),
        out_shape=jax.ShapeDtypeStruct((batch, seq, D_MODEL), F32),
        compiler_params=pltpu.CompilerParams(
            dimension_semantics=("parallel", "parallel"),
            vmem_limit_bytes=V7X_VMEM_LIMIT_BYTES),
        name="tail",
    )(x, ada3, ya, yb, gt, wa, wb, wo, gain2, w1, w3, w2)


def kernel(x, c, positions, ada_w, ada_b, norm1_gain, norm2_gain, w_in, tshift_mu, decay_w0,
           decay_up, iclr_a0, iclr_up, gate_up, k_k, k_a, r_k, lnx_gain, lnx_bias, q_norm_gain,
           k_norm_gain, attn_sinks, branch_gate_b, w_branch_a, w_branch_b, w_out, ffn_w1, ffn_w3,
           ffn_w2):
    depth = ada_w.shape[0]
    batch = x.shape[0]
    rope_tab = _rope_tables(positions)
    for l in range(depth):
        ada3 = _ada(c, ada_w[l], ada_b[l]).reshape(batch, 1, 6 * D_MODEL)
        rw, qkv, gt = _inproj(x, ada3, norm1_gain[l].reshape(1, D_MODEL), w_in[l].astype(BF16),
                              tshift_mu[l].reshape(1, RWKV_SHIFT_WIDTH),
                              branch_gate_b[l].reshape(1, GATE_WIDTH))
        ya = _wkv(rw, decay_w0[l], decay_up[l], iclr_a0[l], iclr_up[l], gate_up[l], k_k[l],
                  k_a[l], r_k[l], lnx_gain[l], lnx_bias[l])
        yb = _attn(qkv, rope_tab, q_norm_gain[l], k_norm_gain[l], attn_sinks[l])
        x = _tail(x, ada3, ya, yb, gt, w_branch_a[l].astype(BF16), w_branch_b[l].astype(BF16),
                  w_out[l].astype(BF16), norm2_gain[l].reshape(1, D_MODEL),
                  ffn_w1[l].astype(BF16), ffn_w3[l].astype(BF16), ffn_w2[l].astype(BF16))
    return x
```

```python
import functools

import jax
import jax.numpy as jnp
from jax import lax
from jax.experimental import pallas as pl
from jax.experimental.pallas import tpu as pltpu

F32 = jnp.float32
BF16 = jnp.bfloat16

D_MODEL = 1024
HEAD_DIM = 64
RWKV_HEADS = 8
RWKV_WIDTH = RWKV_HEADS * HEAD_DIM
DECAY_LORA = 64
ICLR_LORA = 64
GATE_LORA = 128
ATTN_Q_HEADS = 8
ATTN_KV_HEADS = 2
ATTN_GROUPS = ATTN_Q_HEADS // ATTN_KV_HEADS
ATTN_Q_WIDTH = ATTN_Q_HEADS * HEAD_DIM
ATTN_KV_WIDTH = ATTN_KV_HEADS * HEAD_DIM
WINDOW = 128
BLOCK = 128
ROPE_THETA = 500000.0
ROPE_DIM = HEAD_DIM // 4
ROPE_HALF = ROPE_DIM // 2
RMS_EPS = 1e-6
GN_EPS = 64e-5
NEG_INF = -1e30
RWKV_SHIFT_WIDTH = 3 * RWKV_WIDTH + DECAY_LORA + ICLR_LORA + GATE_LORA
QKV_WIDTH = ATTN_Q_WIDTH + 2 * ATTN_KV_WIDTH
GATE_WIDTH = 2 * D_MODEL

V7X_LANES = 128
V7X_VMEM_LIMIT_BYTES = 56 * 1024 * 1024

INPROJ_ROWS = 512
WKV_CHUNK = 64
WKV_GROUP_ROWS = 128
WKV_ROWS = 256
ATTN_ROWS = 256
TAIL_ROWS = 512


def _dot(a, b):
    return jnp.dot(a, b, preferred_element_type=F32)


def _dot_nt(a, b):
    return lax.dot_general(a, b, (((1,), (1,)), ((), ())), preferred_element_type=F32)


def _dot_tn(a, b):
    return lax.dot_general(a, b, (((0,), (0,)), ((), ())), preferred_element_type=F32)


def _mm(a, b):
    return _dot(a.astype(BF16), b.astype(BF16))


def _mm_nt(a, b):
    return _dot_nt(a.astype(BF16), b.astype(BF16))


def _ada_kernel(c_ref, w_ref, b_ref, o_ref):
    o_ref[...] = jnp.dot(c_ref[...], w_ref[...], precision=lax.Precision.HIGHEST,
                         preferred_element_type=F32) + b_ref[...]


def _ada(c, ada_w, ada_b):
    batch = c.shape[0]
    n_out = ada_w.shape[1]
    return pl.pallas_call(
        _ada_kernel,
        grid=(n_out // D_MODEL,),
        in_specs=[
            pl.BlockSpec((batch, D_MODEL), lambda j: (0, 0)),
            pl.BlockSpec((D_MODEL, D_MODEL), lambda j: (0, j)),
            pl.BlockSpec((1, D_MODEL), lambda j: (0, j)),
        ],
        out_specs=pl.BlockSpec((batch, D_MODEL), lambda j: (0, j)),
        out_shape=jax.ShapeDtypeStruct((batch, n_out), F32),
        name="ada",
    )(c, ada_w, ada_b.reshape(1, n_out))


def _rope_kernel(pos_ref, freq_ref, sgn_ref, cos_ref, sin_ref):
    per_row = V7X_LANES // ROPE_DIM
    dense_rows = pos_ref.shape[1]
    ang = pos_ref[0].astype(F32) * freq_ref[...]
    cos_d = jnp.cos(ang)
    sin_d = jnp.sin(ang) * sgn_ref[...]
    lane = lax.broadcasted_iota(jnp.int32, ang.shape, 1)
    rotary0 = lane < ROPE_DIM
    rotary1 = (lane >= HEAD_DIM) & (lane < HEAD_DIM + ROPE_DIM)
    for i in range(per_row):
        shift = (V7X_LANES - ROPE_DIM * i) % V7X_LANES
        for dense, fill, out_ref in ((cos_d, 1.0, cos_ref), (sin_d, 0.0, sin_ref)):
            head0 = pltpu.roll(dense, shift, axis=1) if shift else dense
            head1 = pltpu.roll(head0, HEAD_DIM, axis=1)
            row = jnp.where(rotary0, head0, jnp.where(rotary1, head1, fill))
            out_ref[0, pl.ds(i, dense_rows, stride=per_row), :] = row


def _rope_tables(positions):
    batch, seq = positions.shape
    per_row = V7X_LANES // ROPE_DIM
    inv_freq = ROPE_THETA ** (-jnp.arange(ROPE_HALF, dtype=F32) / ROPE_HALF)
    dim = jnp.arange(V7X_LANES) % ROPE_DIM
    freq = inv_freq[dim % ROPE_HALF].reshape(1, V7X_LANES)
    sgn = jnp.where(dim < ROPE_HALF, -1.0, 1.0).astype(F32).reshape(1, V7X_LANES)
    pos = jnp.repeat(positions.reshape(batch, seq // per_row, per_row), ROPE_DIM, axis=-1)
    vec_spec = pl.BlockSpec((1, V7X_LANES), lambda b: (0, 0))
    tab_spec = pl.BlockSpec((1, seq, V7X_LANES), lambda b: (b, 0, 0))
    tab = jax.ShapeDtypeStruct((batch, seq, V7X_LANES), F32)
    return pl.pallas_call(
        _rope_kernel,
        grid=(batch,),
        in_specs=[pl.BlockSpec((1, seq // per_row, V7X_LANES), lambda b: (b, 0, 0)),
                  vec_spec, vec_spec],
        out_specs=[tab_spec, tab_spec],
        out_shape=[tab, tab],
        name="rope",
    )(pos, freq, sgn)


def _inproj_kernel(x_ref, ada_ref, gain_ref, w_ref, mu_ref, gb_ref,
                   rw_ref, qkv_ref, gt_ref, carry_ref):
    rows = x_ref.shape[1]
    x = x_ref[0]
    ada = ada_ref[0]
    shift1 = ada[:, 0:D_MODEL]
    scale1 = ada[:, D_MODEL:2 * D_MODEL]
    inv = lax.rsqrt(jnp.mean(x * x, axis=-1, keepdims=True) + RMS_EPS)
    h = ((x * inv) * gain_ref[...] * (1.0 + scale1) + shift1).astype(BF16)

    @pl.when(pl.program_id(1) == 0)
    def _():
        carry_ref[...] = jnp.zeros_like(carry_ref)

    p = jnp.dot(h, w_ref[:, 0:RWKV_SHIFT_WIDTH], preferred_element_type=F32)
    prev = pltpu.roll(p, 1, axis=0)
    row = lax.broadcasted_iota(jnp.int32, p.shape, 0)
    prev = jnp.where(row == 0, carry_ref[...], prev)
    carry_ref[...] = p[rows - 1:rows, :]
    rw_ref[0] = p + (prev - p) * mu_ref[...]

    qkv_ref[0] = jnp.dot(h, w_ref[:, RWKV_SHIFT_WIDTH:RWKV_SHIFT_WIDTH + QKV_WIDTH],
                         preferred_element_type=F32).astype(BF16)
    gl = jnp.dot(h, w_ref[:, RWKV_SHIFT_WIDTH + QKV_WIDTH:], preferred_element_type=F32)
    gt_ref[0] = jax.nn.sigmoid(gl + gb_ref[...]).astype(BF16)


def _inproj(x, ada3, gain, w_in_bf, mu, gate_b):
    batch, seq, _ = x.shape
    in_width = w_in_bf.shape[1]
    tm = INPROJ_ROWS
    const = lambda b, j: (0, 0)
    return pl.pallas_call(
        _inproj_kernel,
        grid=(batch, seq // tm),
        in_specs=[
            pl.BlockSpec((1, tm, D_MODEL), lambda b, j: (b, j, 0)),
            pl.BlockSpec((1, 1, 6 * D_MODEL), lambda b, j: (b, 0, 0)),
            pl.BlockSpec((1, D_MODEL), const),
            pl.BlockSpec((D_MODEL, in_width), const, pipeline_mode=pl.Buffered(1)),
            pl.BlockSpec((1, RWKV_SHIFT_WIDTH), const),
            pl.BlockSpec((1, GATE_WIDTH), const),
        ],
        out_specs=[
            pl.BlockSpec((1, tm, RWKV_SHIFT_WIDTH), lambda b, j: (b, j, 0)),
            pl.BlockSpec((1, tm, QKV_WIDTH), lambda b, j: (b, j, 0)),
            pl.BlockSpec((1, tm, GATE_WIDTH), lambda b, j: (b, j, 0)),
        ],
        out_shape=[
            jax.ShapeDtypeStruct((batch, seq, RWKV_SHIFT_WIDTH), F32),
            jax.ShapeDtypeStruct((batch, seq, QKV_WIDTH), BF16),
            jax.ShapeDtypeStruct((batch, seq, GATE_WIDTH), BF16),
        ],
        scratch_shapes=[pltpu.VMEM((1, RWKV_SHIFT_WIDTH), F32)],
        compiler_params=pltpu.CompilerParams(
            dimension_semantics=("parallel", "arbitrary"),
            vmem_limit_bytes=V7X_VMEM_LIMIT_BYTES),
        name="inproj",
    )(x, ada3, gain, w_in_bf, mu, gate_b)


def _cumsum_rows(x):
    n = x.shape[0]
    row = lax.broadcasted_iota(jnp.int32, x.shape, 0)
    s = 1
    while s < n:
        x = x + jnp.where(row >= s, pltpu.roll(x, s, axis=0), 0.0)
        s *= 2
    return x


def _wkv_prologue(cols, w0_ref, dup_ref, a0_ref, aup_ref, gup_ref, kk_ref, ka_ref, rk_ref, hsum):
    L, W, HD = WKV_CHUNK, RWKV_WIDTH, HEAD_DIM
    n_chunks = cols.shape[0] // L
    r = cols[:, 0:W]
    k = cols[:, W:2 * W]
    v = cols[:, 2 * W:3 * W]
    o = 3 * W
    xw = cols[:, o:o + DECAY_LORA]
    xa = cols[:, o + DECAY_LORA:o + DECAY_LORA + ICLR_LORA]
    xg = cols[:, o + DECAY_LORA + ICLR_LORA:]

    w_log =-jax.nn.softplus(-(w0_ref[...] + _mm(jnp.tanh(xw), dup_ref[...]))) - 0.5
    lw = -jnp.exp(w_log)
    a = jax.nn.sigmoid(a0_ref[...] + _mm(xa, aup_ref[...]))
    g = _mm(jax.nn.sigmoid(xg), gup_ref[...])
    kkp = k * kk_ref[...]
    kk = kkp * jnp.minimum(lax.rsqrt(_mm(kkp * kkp, hsum) * float(HD)), 1e12)
    k_mod = k * (1.0 + (a - 1.0) * ka_ref[...])
    bonus = _mm(r * k_mod * rk_ref[...], hsum) * float(HD) * v

    cum = jnp.concatenate([_cumsum_rows(lw[c * L:(c + 1) * L]) for c in range(n_chunks)], axis=0)
    e_in = jnp.exp(cum)
    e_neg = jnp.exp(-cum)
    return dict(
        at=-kk * jnp.exp(cum - lw),
        bt=kk * a * e_neg,
        rt=r * e_in,
        kt=k_mod * e_neg,
        v=v, e_in=e_in, bonus=bonus, g=g)


def _wkv_chunks(p, state_ref):
    L, HD = WKV_CHUNK, HEAD_DIM
    n_chunks = p["v"].shape[0] // L
    row2 = lax.broadcasted_iota(jnp.int32, (2 * L, 2 * L), 0)
    col2 = lax.broadcasted_iota(jnp.int32, (2 * L, 2 * L), 1) % L
    lower2 = col2 < jnp.where(row2 < L, row2, row2 - L + 1)
    lane3 = lax.broadcasted_iota(jnp.int32, (L, 3 * HD), 1)
    zeros_b = jnp.zeros((L, HD), BF16)

    units = [(c, h) for c in range(n_chunks) for h in range(RWKV_HEADS)]

    def pick(arr, c, h):
        return arr[c * L:(c + 1) * L, h * HD:(h + 1) * HD]

    at_b, rt_b, v_b = p["at"].astype(BF16), p["rt"].astype(BF16), p["v"].astype(BF16)
    bt_b, kt_b = p["bt"].astype(BF16), p["kt"].astype(BF16)
    a_t = {u: pick(p["at"], *u) for u in units}
    r_t = {u: pick(p["rt"], *u) for u in units}
    v_h = {u: pick(v_b, *u) for u in units}
    w_l = {(c, h): p["e_in"][(c + 1) * L - 1:(c + 1) * L, h * HD:(h + 1) * HD] for c, h in units}
    bk_t = {u: jnp.concatenate([pick(bt_b, *u), pick(kt_b, *u)], axis=0) for u in units}
    bk_hat = {u: jnp.concatenate([pick(p["bt"], *u) * w_l[u], pick(p["kt"], *u) * w_l[u]],
                                 axis=0).astype(BF16) for u in units}

    sc = {u: jnp.where(lower2, _dot_nt(jnp.concatenate([pick(at_b, *u), pick(rt_b, *u)], axis=0),
                                       bk_t[u]), 0.0) for u in units}
    sc_b = {u: sc[u].astype(BF16) for u in units}
    top = {u: sc_b[u][0:L] for u in units}
    bot = {u: sc_b[u][L:2 * L] for u in units}
    akv = {u: _dot(top[u], jnp.concatenate([zeros_b, v_h[u]], axis=0)) for u in units}
    yield None

    wx = {u: jnp.concatenate([a_t[u], akv[u], sc[u][0:L, 0:HD]], axis=1) for u in units}
    levels = L.bit_length() - 1
    for _ in range(levels):
        wx_b = {u: wx[u].astype(BF16) for u in units}
        wx = {u: _dot(wx_b[u][:, 2 * HD:3 * HD], wx_b[u]) + jnp.where(lane3 < 2 * HD, wx[u], 0.0)
              for u in units}
    x2 = {u: jnp.concatenate([wx[u][:, 0:2 * HD].astype(BF16),
                              jnp.concatenate([zeros_b, v_h[u]], axis=1)], axis=0) for u in units}
    ry = {u: _dot(bot[u], x2[u]) for u in units}
    gs = {u: _dot_tn(x2[u], bk_hat[u]) for u in units}

    y_rows = []
    for c in range(n_chunks):
        y_heads = []
        for h in range(RWKV_HEADS):
            u = (c, h)
            s0 = state_ref[h]
            s0_b = s0.astype(BF16)
            y_heads.append(_dot_nt((r_t[u] + ry[u][:, 0:HD]).astype(BF16), s0_b) + ry[u][:, HD:2 * HD])
            state_ref[h] = s0 * w_l[u] + _dot(s0_b, gs[u][0:HD].astype(BF16)) + gs[u][HD:2 * HD]
        y_rows.append(jnp.concatenate(y_heads, axis=1))
    yield jnp.concatenate(y_rows, axis=0)


def _head_mean(x):
    slabs = []
    for s in range(x.shape[1] // V7X_LANES):
        xs = x[:, s * V7X_LANES:(s + 1) * V7X_LANES]
        lo = lax.broadcasted_iota(jnp.int32, xs.shape, 1) < HEAD_DIM
        lo_sum = jnp.sum(jnp.where(lo, xs, 0.0), axis=-1, keepdims=True)
        hi_sum = jnp.sum(jnp.where(lo, 0.0, xs), axis=-1, keepdims=True)
        slabs.append(jnp.where(lo, lo_sum, hi_sum) * (1.0 / HEAD_DIM))
    return jnp.concatenate(slabs, axis=1)


def _wkv_epilogue(y, p, lng_ref, lnb_ref):
    yc = y - _head_mean(y)
    yn = yc * lax.rsqrt(_head_mean(yc * yc) + GN_EPS) * lng_ref[...] + lnb_ref[...]
    return ((yn + p["bonus"]) * p["g"]).astype(BF16)


def _wkv_kernel(rw_ref, w0_ref, dup_ref, a0_ref, aup_ref, gup_ref, kk_ref, ka_ref, rk_ref,
                lng_ref, lnb_ref, hsum_ref, y_ref, state_ref):
    rows = WKV_GROUP_ROWS
    n_groups = rw_ref.shape[1] // rows

    @pl.when(pl.program_id(1) == 0)
    def _():
        state_ref[...] = jnp.zeros_like(state_ref)

    hsum = hsum_ref[...]

    def prologue(gi):
        return _wkv_prologue(rw_ref[0, gi * rows:(gi + 1) * rows, :], w0_ref, dup_ref, a0_ref,
                             aup_ref, gup_ref, kk_ref, ka_ref, rk_ref, hsum)

    prep = [prologue(0)]
    scan = [_wkv_chunks(prep[0], state_ref)]
    next(scan[0])
    ys = []
    for gi in range(n_groups):
        if gi + 1 < n_groups:
            prep.append(prologue(gi + 1))
        ys.append(next(scan[gi]))
        if gi + 1 < n_groups:
            scan.append(_wkv_chunks(prep[gi + 1], state_ref))
            next(scan[gi + 1])
    for gi in range(n_groups):
        y_ref[0, gi * rows:(gi + 1) * rows, :] = _wkv_epilogue(ys[gi], prep[gi], lng_ref, lnb_ref)


def _wkv(rw, decay_w0, decay_up, iclr_a0, iclr_up, gate_up, k_k, k_a, r_k, lnx_gain, lnx_bias):
    batch, seq, _ = rw.shape
    rows = WKV_ROWS
    W = RWKV_WIDTH
    const = lambda b, j: (0, 0)
    vec = pl.BlockSpec((1, W), const)
    head = jnp.arange(W) // HEAD_DIM
    hsum = jnp.where(head[:, None] == head[None, :], 1.0 / HEAD_DIM, 0.0).astype(BF16)
    return pl.pallas_call(
        _wkv_kernel,
        grid=(batch, seq // rows),
        in_specs=[
            pl.BlockSpec((1, rows, RWKV_SHIFT_WIDTH), lambda b, j: (b, j, 0)),
            vec,
            pl.BlockSpec((DECAY_LORA, W), const),
            vec,
            pl.BlockSpec((ICLR_LORA, W), const),
            pl.BlockSpec((GATE_LORA, W), const),
            vec, vec, vec, vec, vec,
            pl.BlockSpec((W, W), const),
        ],
        out_specs=pl.BlockSpec((1, rows, W), lambda b, j: (b, j, 0)),
        out_shape=jax.ShapeDtypeStruct((batch, seq, W), BF16),
        scratch_shapes=[pltpu.VMEM((RWKV_HEADS, HEAD_DIM, HEAD_DIM), F32)],
        compiler_params=pltpu.CompilerParams(
            dimension_semantics=("parallel", "arbitrary")),
        name="wkv",
    )(rw, decay_w0.reshape(1, W), decay_up.astype(BF16), iclr_a0.reshape(1, W),
      iclr_up.astype(BF16), gate_up.astype(BF16), k_k.reshape(1, W), k_a.reshape(1, W),
      r_k.reshape(1, W), lnx_gain.reshape(1, W), lnx_bias.reshape(1, W), hsum)


def _attn_kernel(sink_ref, q_ref, kv_ref, cos_ref, sin_ref, bias_ref, qg_ref, kg_ref, hmean_ref,
                 perm_ref, o_ref, kprev_ref, vprev_ref):
    n_blk = q_ref.shape[1] // BLOCK
    n_slab = ATTN_Q_WIDTH // V7X_LANES
    slab_per_kv = n_slab // ATTN_KV_HEADS
    kvs = range(ATTN_KV_HEADS)
    pars = range(2 * ATTN_KV_HEADS)

    @pl.when(pl.program_id(1) == 0)
    def _():
        kprev_ref[...] = jnp.zeros_like(kprev_ref)
        vprev_ref[...] = jnp.zeros_like(vprev_ref)

    cos, sin = cos_ref[0], sin_ref[0]
    q_gain = qg_ref[...] * (HEAD_DIM ** -0.5)
    q_all = q_ref[0].astype(F32)
    kv = kv_ref[0].astype(F32)
    slabs = [q_all[:, s * V7X_LANES:(s + 1) * V7X_LANES] for s in range(n_slab)]
    slabs.append(kv[:, 0:ATTN_KV_WIDTH])
    gains = [q_gain] * n_slab + [kg_ref[...]]
    n_rows = n_blk * BLOCK

    ms_all = _mm(jnp.concatenate([x * x for x in slabs], axis=0), hmean_ref[...])
    xn = [x * lax.rsqrt(ms_all[i * n_rows:(i + 1) * n_rows] + RMS_EPS) * gains[i]
          for i, x in enumerate(slabs)]
    partner = _mm(jnp.concatenate(xn, axis=0), perm_ref[0])
    normed = [xn[i] * cos + partner[i * n_rows:(i + 1) * n_rows] * sin for i in range(len(slabs))]
    qn, k_cur = normed[:n_slab], normed[n_slab]
    v_cur = kv[:, ATTN_KV_WIDTH:]
    swapped = _mm(jnp.concatenate([k_cur, v_cur], axis=0), perm_ref[1])
    k_swap, v_swap = swapped[0:n_rows], swapped[n_rows:]
    lo = lax.broadcasted_iota(jnp.int32, (n_rows, V7X_LANES), 1) < HEAD_DIM
    kdup_cur = [jnp.where(lo, k_cur, k_swap).astype(BF16), jnp.where(lo, k_swap, k_cur).astype(BF16)]
    vpar_cur = [jnp.where(lo, v_cur, 0.0).astype(BF16), jnp.where(lo, 0.0, v_swap).astype(BF16),
                jnp.where(lo, v_swap, 0.0).astype(BF16), jnp.where(lo, 0.0, v_cur).astype(BF16)]

    def band(prev_ref, cur, j, i):
        if i == 0:
            return jnp.concatenate([prev_ref[j], cur[j][0:BLOCK]], axis=0)
        return cur[j][(i - 1) * BLOCK:(i + 1) * BLOCK]

    units = [(i, hk) for i in range(n_blk) for hk in kvs]
    kband = {(i, hk): band(kprev_ref, kdup_cur, hk, i) for i, hk in units}
    ones_b = jnp.ones((2 * BLOCK, V7X_LANES), BF16)
    vaug = {(i, j): jnp.concatenate([band(vprev_ref, vpar_cur, j, i), ones_b], axis=1)
            for i in range(n_blk) for j in pars}
    for hk in kvs:
        kprev_ref[hk] = kdup_cur[hk][n_rows - BLOCK:]
    for j in pars:
        vprev_ref[j] = vpar_cur[j][n_rows - BLOCK:]

    lo1 = lax.broadcasted_iota(jnp.int32, (BLOCK, V7X_LANES), 1) < HEAD_DIM
    stack = 2 * slab_per_kv
    first_bias = bias_ref[jnp.minimum(pl.program_id(1), 1)]
    bias = [jnp.concatenate([first_bias if i == 0 else bias_ref[1]] * stack, axis=0)
            for i in range(n_blk)]
    heads = [[2 * (hk * slab_per_kv + j) + p for p in range(2) for j in range(slab_per_kv)]
             for hk in kvs]
    lhs = {}
    for i, hk in units:
        mine = [qn[hk * slab_per_kv + j][i * BLOCK:(i + 1) * BLOCK] for j in range(slab_per_kv)]
        lhs[(i, hk)] = jnp.concatenate([jnp.where(lo1, x, 0.0) for x in mine]
                                       + [jnp.where(lo1, 0.0, x) for x in mine], axis=0).astype(BF16)
    s = {u: _mm_nt(lhs[u], kband[u]) + bias[u[0]] for u in units}
    rmax = {u: jnp.max(s[u], axis=-1, keepdims=True) for u in units}
    m = {(u, t): jnp.maximum(rmax[u][t * BLOCK:(t + 1) * BLOCK], sink_ref[heads[u[1]][t]])
         for u in units for t in range(stack)}
    e = {u: jnp.concatenate([jnp.exp(s[u][t * BLOCK:(t + 1) * BLOCK] - m[(u, t)])
                             for t in range(stack)], axis=0).astype(BF16) for u in units}
    half = stack * BLOCK // 2
    pv = {(i, hk, p): _mm(e[(i, hk)][p * half:(p + 1) * half], vaug[(i, 2 * hk + p)])
          for i, hk in units for p in range(2)}
    extra = {(u, t): jnp.exp(sink_ref[heads[u[1]][t]] - m[(u, t)]) for u in units for t in range(stack)}
    for i, hk in units:
        for j in range(slab_per_kv):
            r0 = slice(j * BLOCK, (j + 1) * BLOCK)
            even, odd = pv[(i, hk, 0)][r0], pv[(i, hk, 1)][r0]
            num = even[:, 0:V7X_LANES] + odd[:, 0:V7X_LANES]
            den = jnp.where(lo1, even[:, V7X_LANES:] + extra[((i, hk), j)],
                            odd[:, V7X_LANES:] + extra[((i, hk), slab_per_kv + j)])
            slab = hk * slab_per_kv + j
            o_ref[0, i * BLOCK:(i + 1) * BLOCK, slab * V7X_LANES:(slab + 1) * V7X_LANES] = (
                num * (1.0 / den)).astype(BF16)


def _attn(qkv, cos_tab, sin_tab, q_gain, k_gain, sinks):
    batch, seq, _ = qkv.shape
    rows = ATTN_ROWS
    cur = lambda b, n: (b, n, 0)
    const = lambda b, n: (0, 0)
    const3 = lambda b, n: (0, 0, 0)
    kv_blk = ATTN_Q_WIDTH // (2 * ATTN_KV_WIDTH)
    gain2 = lambda gn: jnp.tile(gn.reshape(1, HEAD_DIM), (1, V7X_LANES // HEAD_DIM))
    lane = jnp.arange(V7X_LANES)
    head = lane // HEAD_DIM
    hmean = jnp.where(head[:, None] == head[None, :], 1.0 / HEAD_DIM, 0.0).astype(BF16)
    dim = lane % HEAD_DIM
    src = jnp.where(dim < ROPE_HALF, lane + ROPE_HALF, jnp.where(dim < ROPE_DIM, lane - ROPE_HALF, -1))
    partner_p = lane[:, None] == src[None, :]
    swap_p = lane[:, None] == ((lane + HEAD_DIM) % V7X_LANES)[None, :]
    perm = jnp.stack([partner_p, swap_p]).astype(BF16)
    dist = jnp.arange(BLOCK)[:, None] + BLOCK - jnp.arange(2 * BLOCK)[None, :]
    in_band = (dist >= 0) & (dist < WINDOW)
    own = (jnp.arange(2 * BLOCK) >= BLOCK)[None, :]
    bias = jnp.where(jnp.stack([in_band & own, in_band]), 0.0, NEG_INF).astype(F32)
    return pl.pallas_call(
        _attn_kernel,
        grid=(batch, seq // rows),
        in_specs=[
            pl.BlockSpec(memory_space=pltpu.SMEM),
            pl.BlockSpec((1, rows, ATTN_Q_WIDTH), cur),
            pl.BlockSpec((1, rows, 2 * ATTN_KV_WIDTH), lambda b, n: (b, n, kv_blk)),
            pl.BlockSpec((1, rows, V7X_LANES), cur),
            pl.BlockSpec((1, rows, V7X_LANES), cur),
            pl.BlockSpec((2, BLOCK, 2 * BLOCK), const3),
            pl.BlockSpec((1, V7X_LANES), const),
            pl.BlockSpec((1, V7X_LANES), const),
            pl.BlockSpec((V7X_LANES, V7X_LANES), const),
            pl.BlockSpec((2, V7X_LANES, V7X_LANES), const3),
        ],
        out_specs=pl.BlockSpec((1, rows, ATTN_Q_WIDTH), cur),
        out_shape=jax.ShapeDtypeStruct((batch, seq, ATTN_Q_WIDTH), BF16),
        scratch_shapes=[
            pltpu.VMEM((ATTN_KV_HEADS, BLOCK, V7X_LANES), BF16),
            pltpu.VMEM((2 * ATTN_KV_HEADS, BLOCK, V7X_LANES), BF16),
        ],
        compiler_params=pltpu.CompilerParams(
            dimension_semantics=("parallel", "arbitrary")),
        name="attn",
    )(sinks, qkv, qkv, cos_tab, sin_tab, bias, gain2(q_gain), gain2(k_gain), hmean, perm)


def _tail_kernel(x_ref, ada_ref, ya_ref, yb_ref, gt_ref, wa_ref, wb_ref, wo_ref, gain_ref,
                 w1_ref, w3_ref, w2_ref, o_ref):
    ada = ada_ref[0]
    gate1 = ada[:, 2 * D_MODEL:3 * D_MODEL]
    shift2 = ada[:, 3 * D_MODEL:4 * D_MODEL]
    scale2 = ada[:, 4 * D_MODEL:5 * D_MODEL]
    gate2 = ada[:, 5 * D_MODEL:6 * D_MODEL]

    ma = jnp.dot(ya_ref[0], wa_ref[...], preferred_element_type=F32)
    mb = jnp.dot(yb_ref[0], wb_ref[...], preferred_element_type=F32)
    gates = gt_ref[0].astype(F32)
    merged = gates[:, 0:D_MODEL] * ma + gates[:, D_MODEL:] * mb
    x1 = x_ref[0] + gate1 * jnp.dot(merged.astype(BF16), wo_ref[...], preferred_element_type=F32)

    inv = lax.rsqrt(jnp.mean(x1 * x1, axis=-1, keepdims=True) + RMS_EPS)
    h2 = ((x1 * inv) * gain_ref[...] * (1.0 + scale2) + shift2).astype(BF16)
    a1 = jnp.dot(h2, w1_ref[...], preferred_element_type=F32)
    a3 = jnp.dot(h2, w3_ref[...], preferred_element_type=F32)
    z = (jax.nn.silu(a1) * a3).astype(BF16)
    o_ref[0] = x1 + gate2 * jnp.dot(z, w2_ref[...], preferred_element_type=F32)


def _tail(x, ada3, ya, yb, gt, wa, wb, wo, gain2, w1, w3, w2):
    batch, seq, _ = x.shape
    tm = TAIL_ROWS
    d_ff = w1.shape[1]
    const = lambda b, j: (0, 0)
    rows = lambda width: pl.BlockSpec((1, tm, width), lambda b, j: (b, j, 0))
    weight = lambda shape: pl.BlockSpec(shape, const, pipeline_mode=pl.Buffered(1))
    return pl.pallas_call(
        _tail_kernel,
        grid=(batch, seq // tm),
        in_specs=[
            rows(D_MODEL),
            pl.BlockSpec((1, 1, 6 * D_MODEL), lambda b, j: (b, 0, 0)),
            rows(RWKV_WIDTH),
            rows(ATTN_Q_WIDTH),
            rows(GATE_WIDTH),
            weight((RWKV_WIDTH, D_MODEL)),
            weight((ATTN_Q_WIDTH, D_MODEL)),
            weight((D_MODEL, D_MODEL)),
            pl.BlockSpec((1, D_MODEL), const),
            weight((D_MODEL, d_ff)),
            weight((D_MODEL, d_ff)),
            weight((d_ff, D_MODEL)),
        ],
        out_specs=rows(D_MODEL),
        out_shape=jax.ShapeDtypeStruct((batch, seq, D_MODEL), F32),
        compiler_params=pltpu.CompilerParams(
            dimension_semantics=("parallel", "parallel"),
            vmem_limit_bytes=V7X_VMEM_LIMIT_BYTES),
        name="tail",
    )(x, ada3, ya, yb, gt, wa, wb, wo, gain2, w1, w3, w2)


def kernel(x, c, positions, ada_w, ada_b, norm1_gain, norm2_gain, w_in, tshift_mu, decay_w0,
           decay_up, iclr_a0, iclr_up, gate_up, k_k, k_a, r_k, lnx_gain, lnx_bias, q_norm_gain,
           k_norm_gain, attn_sinks, branch_gate_b, w_branch_a, w_branch_b, w_out, ffn_w1, ffn_w3,
           ffn_w2):
    depth = ada_w.shape[0]
    batch = x.shape[0]
    cos_tab, sin_tab = _rope_tables(positions)
    for l in range(depth):
        ada3 = _ada(c, ada_w[l], ada_b[l]).reshape(batch, 1, 6 * D_MODEL)
        rw, qkv, gt = _inproj(x, ada3, norm1_gain[l].reshape(1, D_MODEL), w_in[l].astype(BF16),
                              tshift_mu[l].reshape(1, RWKV_SHIFT_WIDTH),
                              branch_gate_b[l].reshape(1, GATE_WIDTH))
        ya = _wkv(rw, decay_w0[l], decay_up[l], iclr_a0[l], iclr_up[l], gate_up[l], k_k[l],
                  k_a[l], r_k[l], lnx_gain[l], lnx_bias[l])
        yb = _attn(qkv, cos_tab, sin_tab, q_norm_gain[l], k_norm_gain[l], attn_sinks[l])
        x = _tail(x, ada3, ya, yb, gt, w_branch_a[l].astype(BF16), w_branch_b[l].astype(BF16),
                  w_out[l].astype(BF16), norm2_gain[l].reshape(1, D_MODEL),
                  ffn_w1[l].astype(BF16), ffn_w3[l].astype(BF16), ffn_w2[l].astype(BF16))
    return x
```

```python
import math

import jax
import jax.numpy as jnp
from jax import lax
from jax.experimental import pallas as pl
from jax.experimental.pallas import tpu as pltpu

F32 = jnp.float32
BF16 = jnp.bfloat16

D_MODEL = 1024
HEAD_DIM = 64
RWKV_HEADS = 8
RWKV_WIDTH = RWKV_HEADS * HEAD_DIM
DECAY_LORA = 64
ICLR_LORA = 64
GATE_LORA = 128
ATTN_Q_HEADS = 8
ATTN_KV_HEADS = 2
ATTN_GROUPS = ATTN_Q_HEADS // ATTN_KV_HEADS
ATTN_Q_WIDTH = ATTN_Q_HEADS * HEAD_DIM
ATTN_KV_WIDTH = ATTN_KV_HEADS * HEAD_DIM
WINDOW = 128
BLOCK = 128
ROPE_THETA = 500000.0
ROPE_DIM = HEAD_DIM // 4
ROPE_HALF = ROPE_DIM // 2
RMS_EPS = 1e-6
GN_EPS = 64e-5
NEG_INF = -1e30
RWKV_SHIFT_WIDTH = 3 * RWKV_WIDTH + DECAY_LORA + ICLR_LORA + GATE_LORA
QKV_WIDTH = ATTN_Q_WIDTH + 2 * ATTN_KV_WIDTH
GATE_WIDTH = 2 * D_MODEL

V7X_LANES = 128
V7X_SUBLANES = 8
V7X_VMEM_LIMIT_BYTES = 56 * 1024 * 1024

INPROJ_ROWS = 1024
INPROJ_SUB_ROWS = 256
WKV_CHUNK = 64
WKV_GROUP_ROWS = 128
WKV_ROWS = 256
ATTN_ROWS = 256
TAIL_ROWS = 512
TAIL_SUB_ROWS = 256


def _dot(a, b):
    return jnp.dot(a, b, preferred_element_type=F32)


def _dot_nt(a, b):
    return lax.dot_general(a, b, (((1,), (1,)), ((), ())), preferred_element_type=F32)


def _dot_tn(a, b):
    return lax.dot_general(a, b, (((0,), (0,)), ((), ())), preferred_element_type=F32)


def _mm(a, b):
    return _dot(a.astype(BF16), b.astype(BF16))


def _mm_nt(a, b):
    return _dot_nt(a.astype(BF16), b.astype(BF16))


def _ada_kernel(c_ref, w_ref, b_ref, o_ref):
    o_ref[...] = jnp.dot(c_ref[...], w_ref[...], precision=lax.Precision.HIGHEST,
                         preferred_element_type=F32) + b_ref[...]


def _ada(c, ada_w, ada_b):
    batch = c.shape[0]
    n_out = ada_w.shape[1]
    return pl.pallas_call(
        _ada_kernel,
        grid=(n_out // D_MODEL,),
        in_specs=[
            pl.BlockSpec((batch, D_MODEL), lambda j: (0, 0)),
            pl.BlockSpec((D_MODEL, D_MODEL), lambda j: (0, j)),
            pl.BlockSpec((1, D_MODEL), lambda j: (0, j)),
        ],
        out_specs=pl.BlockSpec((batch, D_MODEL), lambda j: (0, j)),
        out_shape=jax.ShapeDtypeStruct((batch, n_out), F32),
        name="ada",
    )(c, ada_w, ada_b.reshape(1, n_out))


def _rope_kernel(pos_ref, freq_ref, sgn_ref, cos_ref, sin_ref):
    per_row = V7X_LANES // ROPE_DIM
    dense_rows = pos_ref.shape[1]
    ang = pos_ref[0].astype(F32) * freq_ref[...]
    cos_d = jnp.cos(ang)
    sin_d = jnp.sin(ang) * sgn_ref[...]
    lane = lax.broadcasted_iota(jnp.int32, ang.shape, 1)
    rotary0 = lane < ROPE_DIM
    rotary1 = (lane >= HEAD_DIM) & (lane < HEAD_DIM + ROPE_DIM)
    for i in range(per_row):
        shift = (V7X_LANES - ROPE_DIM * i) % V7X_LANES
        for dense, fill, out_ref in ((cos_d, 1.0, cos_ref), (sin_d, 0.0, sin_ref)):
            head0 = pltpu.roll(dense, shift, axis=1) if shift else dense
            head1 = pltpu.roll(head0, HEAD_DIM, axis=1)
            row = jnp.where(rotary0, head0, jnp.where(rotary1, head1, fill))
            out_ref[0, pl.ds(i, dense_rows, stride=per_row), :] = row


def _rope_tables(positions):
    batch, seq = positions.shape
    per_row = V7X_LANES // ROPE_DIM
    inv_freq = ROPE_THETA ** (-jnp.arange(ROPE_HALF, dtype=F32) / ROPE_HALF)
    dim = jnp.arange(V7X_LANES) % ROPE_DIM
    freq = inv_freq[dim % ROPE_HALF].reshape(1, V7X_LANES)
    sgn = jnp.where(dim < ROPE_HALF, -1.0, 1.0).astype(F32).reshape(1, V7X_LANES)
    pos = jnp.repeat(positions.reshape(batch, seq // per_row, per_row), ROPE_DIM, axis=-1)
    vec_spec = pl.BlockSpec((1, V7X_LANES), lambda b: (0, 0))
    tab_spec = pl.BlockSpec((1, seq, V7X_LANES), lambda b: (b, 0, 0))
    tab = jax.ShapeDtypeStruct((batch, seq, V7X_LANES), F32)
    return pl.pallas_call(
        _rope_kernel,
        grid=(batch,),
        in_specs=[pl.BlockSpec((1, seq // per_row, V7X_LANES), lambda b: (b, 0, 0)),
                  vec_spec, vec_spec],
        out_specs=[tab_spec, tab_spec],
        out_shape=[tab, tab],
        name="rope",
    )(pos, freq, sgn)


def _inproj_kernel(x_ref, ada_ref, gain_ref, w_ref, mu_ref, gb_ref,
                   rw_ref, qkv_ref, gt_ref, carry_ref):
    rows = INPROJ_SUB_ROWS
    ada = ada_ref[0]
    shift1 = ada[:, 0:D_MODEL]
    mod1 = gain_ref[...] * (1.0 + ada[:, D_MODEL:2 * D_MODEL])

    @pl.when(pl.program_id(1) == 0)
    def _():
        carry_ref[...] = jnp.zeros_like(carry_ref)

    last = carry_ref[...]
    for i in range(x_ref.shape[1] // rows):
        rs = slice(i * rows, (i + 1) * rows)
        x = x_ref[0, rs, :]
        inv = lax.rsqrt(jnp.mean(x * x, axis=-1, keepdims=True) + RMS_EPS)
        h = ((x * inv) * mod1 + shift1).astype(BF16)

        p = _dot(h, w_ref[:, 0:RWKV_SHIFT_WIDTH])
        prev = pltpu.roll(p, 1, axis=0)
        row = lax.broadcasted_iota(jnp.int32, p.shape, 0)
        prev = jnp.where(row == 0, last, prev)
        last = p[rows - 1:rows, :]
        rw_ref[0, rs, :] = p + (prev - p) * mu_ref[...]

        qkv_ref[0, rs, :] = _dot(h, w_ref[:, RWKV_SHIFT_WIDTH:RWKV_SHIFT_WIDTH + QKV_WIDTH]).astype(BF16)
        gl = _dot(h, w_ref[:, RWKV_SHIFT_WIDTH + QKV_WIDTH:])
        gt_ref[0, rs, :] = jax.nn.sigmoid(gl + gb_ref[...]).astype(BF16)
    carry_ref[...] = last


def _inproj(x, ada3, gain, w_in_bf, mu, gate_b):
    batch, seq, _ = x.shape
    in_width = w_in_bf.shape[1]
    tm = INPROJ_ROWS
    const = lambda b, j: (0, 0)
    return pl.pallas_call(
        _inproj_kernel,
        grid=(batch, seq // tm),
        in_specs=[
            pl.BlockSpec((1, tm, D_MODEL), lambda b, j: (b, j, 0)),
            pl.BlockSpec((1, 1, 6 * D_MODEL), lambda b, j: (b, 0, 0)),
            pl.BlockSpec((1, D_MODEL), const),
            pl.BlockSpec((D_MODEL, in_width), const, pipeline_mode=pl.Buffered(1)),
            pl.BlockSpec((1, RWKV_SHIFT_WIDTH), const),
            pl.BlockSpec((1, GATE_WIDTH), const),
        ],
        out_specs=[
            pl.BlockSpec((1, tm, RWKV_SHIFT_WIDTH), lambda b, j: (b, j, 0)),
            pl.BlockSpec((1, tm, QKV_WIDTH), lambda b, j: (b, j, 0)),
            pl.BlockSpec((1, tm, GATE_WIDTH), lambda b, j: (b, j, 0)),
        ],
        out_shape=[
            jax.ShapeDtypeStruct((batch, seq, RWKV_SHIFT_WIDTH), F32),
            jax.ShapeDtypeStruct((batch, seq, QKV_WIDTH), BF16),
            jax.ShapeDtypeStruct((batch, seq, GATE_WIDTH), BF16),
        ],
        scratch_shapes=[pltpu.VMEM((1, RWKV_SHIFT_WIDTH), F32)],
        compiler_params=pltpu.CompilerParams(
            dimension_semantics=("parallel", "arbitrary"),
            vmem_limit_bytes=V7X_VMEM_LIMIT_BYTES),
        name="inproj",
    )(x, ada3, gain, w_in_bf, mu, gate_b)


def _cumsum_rows(x):
    n = x.shape[0]
    row = lax.broadcasted_iota(jnp.int32, x.shape, 0)
    s = 1
    while s < min(n, V7X_SUBLANES):
        x = x + jnp.where(row >= s, pltpu.roll(x, s, axis=0), 0.0)
        s *= 2
    while s < n:
        x = jnp.concatenate([x[:s], x[s:] + x[:n - s]], axis=0)
        s *= 2
    return x


def _wkv_prologue(cols, w0_ref, dup_ref, a0_ref, aup_ref, gup_ref, kk_ref, ka_ref, rk_ref, hsum):
    L, W, HD = WKV_CHUNK, RWKV_WIDTH, HEAD_DIM
    n_chunks = cols.shape[0] // L
    r = cols[:, 0:W]
    k = cols[:, W:2 * W]
    v = cols[:, 2 * W:3 * W]
    o = 3 * W
    xw = cols[:, o:o + DECAY_LORA]
    xa = cols[:, o + DECAY_LORA:o + DECAY_LORA + ICLR_LORA]
    xg = cols[:, o + DECAY_LORA + ICLR_LORA:]

    lw = jax.nn.sigmoid(w0_ref[...] + _mm(jnp.tanh(xw), dup_ref[...])) * (-math.exp(-0.5))
    a = jax.nn.sigmoid(a0_ref[...] + _mm(xa, aup_ref[...]))
    g = _mm(jax.nn.sigmoid(xg), gup_ref[...])
    kkp = k * kk_ref[...]
    kk = kkp * jnp.minimum(lax.rsqrt(_mm(kkp * kkp, hsum) * float(HD)), 1e12)
    k_mod = k * (1.0 + (a - 1.0) * ka_ref[...])
    bonus = _mm(r * k_mod * rk_ref[...], hsum) * float(HD) * v

    cum = jnp.concatenate([_cumsum_rows(lw[c * L:(c + 1) * L]) for c in range(n_chunks)], axis=0)
    e_in = jnp.exp(cum)
    e_neg = jnp.exp(-cum)
    return dict(
        at=-kk * jnp.exp(cum - lw),
        bt=kk * a * e_neg,
        rt=r * e_in,
        kt=k_mod * e_neg,
        v=v, e_in=e_in, bonus=bonus, g=g)


def _wkv_chunks(p, state_ref):
    L, HD = WKV_CHUNK, HEAD_DIM
    n_chunks = p["v"].shape[0] // L
    row2 = lax.broadcasted_iota(jnp.int32, (2 * L, 2 * L), 0)
    col2 = lax.broadcasted_iota(jnp.int32, (2 * L, 2 * L), 1) % L
    lower2 = col2 < jnp.where(row2 < L, row2, row2 - L + 1)
    lane3 = lax.broadcasted_iota(jnp.int32, (L, 3 * HD), 1)
    zeros_b = jnp.zeros((L, HD), BF16)

    units = [(c, h) for c in range(n_chunks) for h in range(RWKV_HEADS)]

    def pick(arr, c, h):
        return arr[c * L:(c + 1) * L, h * HD:(h + 1) * HD]

    at_b, rt_b, v_b = p["at"].astype(BF16), p["rt"].astype(BF16), p["v"].astype(BF16)
    bt_b, kt_b = p["bt"].astype(BF16), p["kt"].astype(BF16)
    a_t = {u: pick(p["at"], *u) for u in units}
    r_t = {u: pick(p["rt"], *u) for u in units}
    v_h = {u: pick(v_b, *u) for u in units}
    w_l = {(c, h): p["e_in"][(c + 1) * L - 1:(c + 1) * L, h * HD:(h + 1) * HD] for c, h in units}
    bk_t = {u: jnp.concatenate([pick(bt_b, *u), pick(kt_b, *u)], axis=0) for u in units}
    bk_hat = {u: jnp.concatenate([pick(p["bt"], *u) * w_l[u], pick(p["kt"], *u) * w_l[u]],
                                 axis=0).astype(BF16) for u in units}

    sc = {u: jnp.where(lower2, _dot_nt(jnp.concatenate([pick(at_b, *u), pick(rt_b, *u)], axis=0),
                                       bk_t[u]), 0.0) for u in units}
    sc_b = {u: sc[u].astype(BF16) for u in units}
    top = {u: sc_b[u][0:L] for u in units}
    bot = {u: sc_b[u][L:2 * L] for u in units}
    akv = {u: _dot(top[u], jnp.concatenate([zeros_b, v_h[u]], axis=0)) for u in units}
    yield None

    wx = {u: jnp.concatenate([a_t[u], akv[u], sc[u][0:L, 0:HD]], axis=1) for u in units}
    levels = L.bit_length() - 1
    for _ in range(levels):
        wx_b = {u: wx[u].astype(BF16) for u in units}
        wx = {u: _dot(wx_b[u][:, 2 * HD:3 * HD], wx_b[u]) + jnp.where(lane3 < 2 * HD, wx[u], 0.0)
              for u in units}
    x2 = {u: jnp.concatenate([wx[u][:, 0:2 * HD].astype(BF16),
                              jnp.concatenate([zeros_b, v_h[u]], axis=1)], axis=0) for u in units}
    ry = {u: _dot(bot[u], x2[u]) for u in units}
    gs = {u: _dot_tn(x2[u], bk_hat[u]) for u in units}

    y_rows = []
    for c in range(n_chunks):
        y_heads = []
        for h in range(RWKV_HEADS):
            u = (c, h)
            s0 = state_ref[h]
            s0_b = s0.astype(BF16)
            y_heads.append(_dot_nt((r_t[u] + ry[u][:, 0:HD]).astype(BF16), s0_b) + ry[u][:, HD:2 * HD])
            state_ref[h] = s0 * w_l[u] + _dot(s0_b, gs[u][0:HD].astype(BF16)) + gs[u][HD:2 * HD]
        y_rows.append(jnp.concatenate(y_heads, axis=1))
    yield jnp.concatenate(y_rows, axis=0)


def _head_mean(x):
    slabs = []
    for s in range(x.shape[1] // V7X_LANES):
        xs = x[:, s * V7X_LANES:(s + 1) * V7X_LANES]
        lo = lax.broadcasted_iota(jnp.int32, xs.shape, 1) < HEAD_DIM
        lo_sum = jnp.sum(jnp.where(lo, xs, 0.0), axis=-1, keepdims=True)
        hi_sum = jnp.sum(jnp.where(lo, 0.0, xs), axis=-1, keepdims=True)
        slabs.append(jnp.where(lo, lo_sum, hi_sum) * (1.0 / HEAD_DIM))
    return jnp.concatenate(slabs, axis=1)


def _wkv_epilogue(y, p, lng_ref, lnb_ref, head_mean):
    yc = y - head_mean(y)
    yn = yc * lax.rsqrt(head_mean(yc * yc) + GN_EPS) * lng_ref[...] + lnb_ref[...]
    return ((yn + p["bonus"]) * p["g"]).astype(BF16)


def _wkv_kernel(rw_ref, w0_ref, dup_ref, a0_ref, aup_ref, gup_ref, kk_ref, ka_ref, rk_ref,
                lng_ref, lnb_ref, hsum_ref, y_ref, state_ref):
    rows = WKV_GROUP_ROWS
    n_groups = rw_ref.shape[1] // rows

    @pl.when(pl.program_id(1) == 0)
    def _():
        state_ref[...] = jnp.zeros_like(state_ref)

    hsum = hsum_ref[...]

    def prologue(gi):
        return _wkv_prologue(rw_ref[0, gi * rows:(gi + 1) * rows, :], w0_ref, dup_ref, a0_ref,
                             aup_ref, gup_ref, kk_ref, ka_ref, rk_ref, hsum)

    prep = [prologue(0)]
    scan = [_wkv_chunks(prep[0], state_ref)]
    next(scan[0])
    ys = []
    for gi in range(n_groups):
        if gi + 1 < n_groups:
            prep.append(prologue(gi + 1))
        ys.append(next(scan[gi]))
        if gi + 1 < n_groups:
            scan.append(_wkv_chunks(prep[gi + 1], state_ref))
            next(scan[gi + 1])
    for gi in range(n_groups):
        mean = (lambda t: _mm(t, hsum)) if gi == n_groups - 1 else _head_mean
        y_ref[0, gi * rows:(gi + 1) * rows, :] = _wkv_epilogue(ys[gi], prep[gi], lng_ref, lnb_ref, mean)


def _wkv(rw, decay_w0, decay_up, iclr_a0, iclr_up, gate_up, k_k, k_a, r_k, lnx_gain, lnx_bias):
    batch, seq, _ = rw.shape
    rows = WKV_ROWS
    W = RWKV_WIDTH
    const = lambda b, j: (0, 0)
    vec = pl.BlockSpec((1, W), const)
    head = jnp.arange(W) // HEAD_DIM
    hsum = jnp.where(head[:, None] == head[None, :], 1.0 / HEAD_DIM, 0.0).astype(BF16)
    return pl.pallas_call(
        _wkv_kernel,
        grid=(batch, seq // rows),
        in_specs=[
            pl.BlockSpec((1, rows, RWKV_SHIFT_WIDTH), lambda b, j: (b, j, 0)),
            vec,
            pl.BlockSpec((DECAY_LORA, W), const),
            vec,
            pl.BlockSpec((ICLR_LORA, W), const),
            pl.BlockSpec((GATE_LORA, W), const),
            vec, vec, vec, vec, vec,
            pl.BlockSpec((W, W), const),
        ],
        out_specs=pl.BlockSpec((1, rows, W), lambda b, j: (b, j, 0)),
        out_shape=jax.ShapeDtypeStruct((batch, seq, W), BF16),
        scratch_shapes=[pltpu.VMEM((RWKV_HEADS, HEAD_DIM, HEAD_DIM), F32)],
        compiler_params=pltpu.CompilerParams(
            dimension_semantics=("parallel", "arbitrary")),
        name="wkv",
    )(rw, decay_w0.reshape(1, W), decay_up.astype(BF16), iclr_a0.reshape(1, W),
      iclr_up.astype(BF16), gate_up.astype(BF16), k_k.reshape(1, W), k_a.reshape(1, W),
      r_k.reshape(1, W), lnx_gain.reshape(1, W), lnx_bias.reshape(1, W), hsum)


def _attn_kernel(sink_ref, q_ref, kv_ref, cos_ref, sin_ref, bias_ref, qg_ref, kg_ref, hmean_ref,
                 perm_ref, o_ref, kprev_ref, vprev_ref):
    n_blk = q_ref.shape[1] // BLOCK
    n_slab = ATTN_Q_WIDTH // V7X_LANES
    slab_per_kv = n_slab // ATTN_KV_HEADS
    kvs = range(ATTN_KV_HEADS)
    pars = range(2 * ATTN_KV_HEADS)

    @pl.when(pl.program_id(1) == 0)
    def _():
        kprev_ref[...] = jnp.zeros_like(kprev_ref)
        vprev_ref[...] = jnp.zeros_like(vprev_ref)

    cos, sin = cos_ref[0], sin_ref[0]
    q_gain = qg_ref[...] * (HEAD_DIM ** -0.5)
    q_all = q_ref[0].astype(F32)
    kv = kv_ref[0].astype(F32)
    slabs = [q_all[:, s * V7X_LANES:(s + 1) * V7X_LANES] for s in range(n_slab)]
    slabs.append(kv[:, 0:ATTN_KV_WIDTH])
    gains = [q_gain] * n_slab + [kg_ref[...]]
    n_rows = n_blk * BLOCK

    ms_all = _mm(jnp.concatenate([x * x for x in slabs], axis=0), hmean_ref[...])
    xn = [x * lax.rsqrt(ms_all[i * n_rows:(i + 1) * n_rows] + RMS_EPS) * gains[i]
          for i, x in enumerate(slabs)]
    partner = _mm(jnp.concatenate(xn, axis=0), perm_ref[0])
    normed = [xn[i] * cos + partner[i * n_rows:(i + 1) * n_rows] * sin for i in range(len(slabs))]
    qn, k_cur = normed[:n_slab], normed[n_slab]
    v_cur = kv[:, ATTN_KV_WIDTH:]
    swapped = _mm(jnp.concatenate([k_cur, v_cur], axis=0), perm_ref[1])
    k_swap, v_swap = swapped[0:n_rows], swapped[n_rows:]
    lo = lax.broadcasted_iota(jnp.int32, (n_rows, V7X_LANES), 1) < HEAD_DIM
    kdup_cur = [jnp.where(lo, k_cur, k_swap).astype(BF16), jnp.where(lo, k_swap, k_cur).astype(BF16)]
    vpar_cur = [jnp.where(lo, v_cur, 0.0).astype(BF16), jnp.where(lo, 0.0, v_swap).astype(BF16),
                jnp.where(lo, v_swap, 0.0).astype(BF16), jnp.where(lo, 0.0, v_cur).astype(BF16)]

    def band(prev_ref, cur, j, i):
        if i == 0:
            return jnp.concatenate([prev_ref[j], cur[j][0:BLOCK]], axis=0)
        return cur[j][(i - 1) * BLOCK:(i + 1) * BLOCK]

    units = [(i, hk) for i in range(n_blk) for hk in kvs]
    kband = {(i, hk): band(kprev_ref, kdup_cur, hk, i) for i, hk in units}
    ones_b = jnp.ones((2 * BLOCK, V7X_LANES), BF16)
    vaug = {(i, j): jnp.concatenate([band(vprev_ref, vpar_cur, j, i), ones_b], axis=1)
            for i in range(n_blk) for j in pars}
    for hk in kvs:
        kprev_ref[hk] = kdup_cur[hk][n_rows - BLOCK:]
    for j in pars:
        vprev_ref[j] = vpar_cur[j][n_rows - BLOCK:]

    lo1 = lax.broadcasted_iota(jnp.int32, (BLOCK, V7X_LANES), 1) < HEAD_DIM
    stack = 2 * slab_per_kv
    first_bias = bias_ref[jnp.minimum(pl.program_id(1), 1)]
    bias = [jnp.concatenate([first_bias if i == 0 else bias_ref[1]] * stack, axis=0)
            for i in range(n_blk)]
    heads = [[2 * (hk * slab_per_kv + j) + p for p in range(2) for j in range(slab_per_kv)]
             for hk in kvs]
    lhs = {}
    for i, hk in units:
        mine = [qn[hk * slab_per_kv + j][i * BLOCK:(i + 1) * BLOCK] for j in range(slab_per_kv)]
        lhs[(i, hk)] = jnp.concatenate([jnp.where(lo1, x, 0.0) for x in mine]
                                       + [jnp.where(lo1, 0.0, x) for x in mine], axis=0).astype(BF16)
    s = {u: _mm_nt(lhs[u], kband[u]) + bias[u[0]] for u in units}
    rmax = {u: jnp.max(s[u], axis=-1, keepdims=True) for u in units}
    m = {(u, t): jnp.maximum(rmax[u][t * BLOCK:(t + 1) * BLOCK], sink_ref[heads[u[1]][t]])
         for u in units for t in range(stack)}
    e = {u: jnp.concatenate([jnp.exp(s[u][t * BLOCK:(t + 1) * BLOCK] - m[(u, t)])
                             for t in range(stack)], axis=0).astype(BF16) for u in units}
    half = stack * BLOCK // 2
    pv = {(i, hk, p): _mm(e[(i, hk)][p * half:(p + 1) * half], vaug[(i, 2 * hk + p)])
          for i, hk in units for p in range(2)}
    extra = {(u, t): jnp.exp(sink_ref[heads[u[1]][t]] - m[(u, t)]) for u in units for t in range(stack)}
    for i, hk in units:
        for j in range(slab_per_kv):
            r0 = slice(j * BLOCK, (j + 1) * BLOCK)
            even, odd = pv[(i, hk, 0)][r0], pv[(i, hk, 1)][r0]
            num = even[:, 0:V7X_LANES] + odd[:, 0:V7X_LANES]
            den = jnp.where(lo1, even[:, V7X_LANES:] + extra[((i, hk), j)],
                            odd[:, V7X_LANES:] + extra[((i, hk), slab_per_kv + j)])
            slab = hk * slab_per_kv + j
            o_ref[0, i * BLOCK:(i + 1) * BLOCK, slab * V7X_LANES:(slab + 1) * V7X_LANES] = (
                num * (1.0 / den)).astype(BF16)


def _attn(qkv, cos_tab, sin_tab, q_gain, k_gain, sinks):
    batch, seq, _ = qkv.shape
    rows = ATTN_ROWS
    cur = lambda b, n: (b, n, 0)
    const = lambda b, n: (0, 0)
    const3 = lambda b, n: (0, 0, 0)
    kv_blk = ATTN_Q_WIDTH // (2 * ATTN_KV_WIDTH)
    gain2 = lambda gn: jnp.tile(gn.reshape(1, HEAD_DIM), (1, V7X_LANES // HEAD_DIM))
    lane = jnp.arange(V7X_LANES)
    head = lane // HEAD_DIM
    hmean = jnp.where(head[:, None] == head[None, :], 1.0 / HEAD_DIM, 0.0).astype(BF16)
    dim = lane % HEAD_DIM
    src = jnp.where(dim < ROPE_HALF, lane + ROPE_HALF, jnp.where(dim < ROPE_DIM, lane - ROPE_HALF, -1))
    partner_p = lane[:, None] == src[None, :]
    swap_p = lane[:, None] == ((lane + HEAD_DIM) % V7X_LANES)[None, :]
    perm = jnp.stack([partner_p, swap_p]).astype(BF16)
    dist = jnp.arange(BLOCK)[:, None] + BLOCK - jnp.arange(2 * BLOCK)[None, :]
    in_band = (dist >= 0) & (dist < WINDOW)
    own = (jnp.arange(2 * BLOCK) >= BLOCK)[None, :]
    bias = jnp.where(jnp.stack([in_band & own, in_band]), 0.0, NEG_INF).astype(F32)
    return pl.pallas_call(
        _attn_kernel,
        grid=(batch, seq // rows),
        in_specs=[
            pl.BlockSpec(memory_space=pltpu.SMEM),
            pl.BlockSpec((1, rows, ATTN_Q_WIDTH), cur),
            pl.BlockSpec((1, rows, 2 * ATTN_KV_WIDTH), lambda b, n: (b, n, kv_blk)),
            pl.BlockSpec((1, rows, V7X_LANES), cur),
            pl.BlockSpec((1, rows, V7X_LANES), cur),
            pl.BlockSpec((2, BLOCK, 2 * BLOCK), const3),
            pl.BlockSpec((1, V7X_LANES), const),
            pl.BlockSpec((1, V7X_LANES), const),
            pl.BlockSpec((V7X_LANES, V7X_LANES), const),
            pl.BlockSpec((2, V7X_LANES, V7X_LANES), const3),
        ],
        out_specs=pl.BlockSpec((1, rows, ATTN_Q_WIDTH), cur),
        out_shape=jax.ShapeDtypeStruct((batch, seq, ATTN_Q_WIDTH), BF16),
        scratch_shapes=[
            pltpu.VMEM((ATTN_KV_HEADS, BLOCK, V7X_LANES), BF16),
            pltpu.VMEM((2 * ATTN_KV_HEADS, BLOCK, V7X_LANES), BF16),
        ],
        compiler_params=pltpu.CompilerParams(
            dimension_semantics=("parallel", "arbitrary")),
        name="attn",
    )(sinks, qkv, qkv, cos_tab, sin_tab, bias, gain2(q_gain), gain2(k_gain), hmean, perm)


def _tail_kernel(x_ref, ada_ref, ya_ref, yb_ref, gt_ref, wa_ref, wb_ref, wo_ref, gain_ref,
                 w1_ref, w3_ref, w2_ref, o_ref):
    ada = ada_ref[0]
    gate1 = ada[:, 2 * D_MODEL:3 * D_MODEL]
    shift2 = ada[:, 3 * D_MODEL:4 * D_MODEL]
    scale2 = ada[:, 4 * D_MODEL:5 * D_MODEL]
    gate2 = ada[:, 5 * D_MODEL:6 * D_MODEL]
    mod2 = gain_ref[...] * (1.0 + scale2)

    rows = TAIL_SUB_ROWS
    subs = [slice(i * rows, (i + 1) * rows) for i in range(x_ref.shape[1] // rows)]
    ma = [_dot(ya_ref[0, rs, :], wa_ref[...]) for rs in subs]
    mb = [_dot(yb_ref[0, rs, :], wb_ref[...]) for rs in subs]
    merged = []
    for i, rs in enumerate(subs):
        gates = gt_ref[0, rs, :].astype(F32)
        merged.append((gates[:, 0:D_MODEL] * ma[i] + gates[:, D_MODEL:] * mb[i]).astype(BF16))
    x1 = [x_ref[0, rs, :] + gate1 * _dot(merged[i], wo_ref[...]) for i, rs in enumerate(subs)]
    h2 = []
    for x1_i in x1:
        inv = lax.rsqrt(jnp.mean(x1_i * x1_i, axis=-1, keepdims=True) + RMS_EPS)
        h2.append(((x1_i * inv) * mod2 + shift2).astype(BF16))
    a1 = [_dot(h, w1_ref[...]) for h in h2]
    a3 = [_dot(h, w3_ref[...]) for h in h2]
    z = [(jax.nn.silu(a1[i]) * a3[i]).astype(BF16) for i in range(len(subs))]
    for i, rs in enumerate(subs):
        o_ref[0, rs, :] = x1[i] + gate2 * _dot(z[i], w2_ref[...])


def _tail(x, ada3, ya, yb, gt, wa, wb, wo, gain2, w1, w3, w2):
    batch, seq, _ = x.shape
    tm = TAIL_ROWS
    d_ff = w1.shape[1]
    const = lambda b, j: (0, 0)
    rows = lambda width: pl.BlockSpec((1, tm, width), lambda b, j: (b, j, 0))
    weight = lambda shape: pl.BlockSpec(shape, const, pipeline_mode=pl.Buffered(1))
    return pl.pallas_call(
        _tail_kernel,
        grid=(batch, seq // tm),
        in_specs=[
            rows(D_MODEL),
            pl.BlockSpec((1, 1, 6 * D_MODEL), lambda b, j: (b, 0, 0)),
            rows(RWKV_WIDTH),
            rows(ATTN_Q_WIDTH),
            rows(GATE_WIDTH),
            weight((RWKV_WIDTH, D_MODEL)),
            weight((ATTN_Q_WIDTH, D_MODEL)),
            weight((D_MODEL, D_MODEL)),
            pl.BlockSpec((1, D_MODEL), const),
            weight((D_MODEL, d_ff)),
            weight((D_MODEL, d_ff)),
            weight((d_ff, D_MODEL)),
        ],
        out_specs=rows(D_MODEL),
        out_shape=jax.ShapeDtypeStruct((batch, seq, D_MODEL), F32),
        compiler_params=pltpu.CompilerParams(
            dimension_semantics=("parallel", "parallel"),
            vmem_limit_bytes=V7X_VMEM_LIMIT_BYTES),
        name="tail",
    )(x, ada3, ya, yb, gt, wa, wb, wo, gain2, w1, w3, w2)


def kernel(x, c, positions, ada_w, ada_b, norm1_gain, norm2_gain, w_in, tshift_mu, decay_w0,
           decay_up, iclr_a0, iclr_up, gate_up, k_k, k_a, r_k, lnx_gain, lnx_bias, q_norm_gain,
           k_norm_gain, attn_sinks, branch_gate_b, w_branch_a, w_branch_b, w_out, ffn_w1, ffn_w3,
           ffn_w2):
    depth = ada_w.shape[0]
    batch = x.shape[0]
    cos_tab, sin_tab = _rope_tables(positions)
    for l in range(depth):
        ada3 = _ada(c, ada_w[l], ada_b[l]).reshape(batch, 1, 6 * D_MODEL)
        rw, qkv, gt = _inproj(x, ada3, norm1_gain[l].reshape(1, D_MODEL), w_in[l].astype(BF16),
                              tshift_mu[l].reshape(1, RWKV_SHIFT_WIDTH),
                              branch_gate_b[l].reshape(1, GATE_WIDTH))
        ya = _wkv(rw, decay_w0[l], decay_up[l], iclr_a0[l], iclr_up[l], gate_up[l], k_k[l],
                  k_a[l], r_k[l], lnx_gain[l], lnx_bias[l])
        yb = _attn(qkv, cos_tab, sin_tab, q_norm_gain[l], k_norm_gain[l], attn_sinks[l])
        x = _tail(x, ada3, ya, yb, gt, w_branch_a[l].astype(BF16), w_branch_b[l].astype(BF16),
                  w_out[l].astype(BF16), norm2_gain[l].reshape(1, D_MODEL),
                  ffn_w1[l].astype(BF16), ffn_w3[l].astype(BF16), ffn_w2[l].astype(BF16))
    return x
```

```python
import math

import jax
import jax.numpy as jnp
from jax import lax
from jax.experimental import pallas as pl
from jax.experimental.pallas import tpu as pltpu

F32 = jnp.float32
BF16 = jnp.bfloat16

D_MODEL = 1024
HEAD_DIM = 64
RWKV_HEADS = 8
RWKV_WIDTH = RWKV_HEADS * HEAD_DIM
DECAY_LORA = 64
ICLR_LORA = 64
GATE_LORA = 128
ATTN_Q_HEADS = 8
ATTN_KV_HEADS = 2
ATTN_GROUPS = ATTN_Q_HEADS // ATTN_KV_HEADS
ATTN_Q_WIDTH = ATTN_Q_HEADS * HEAD_DIM
ATTN_KV_WIDTH = ATTN_KV_HEADS * HEAD_DIM
WINDOW = 128
BLOCK = 128
ROPE_THETA = 500000.0
ROPE_DIM = HEAD_DIM // 4
ROPE_HALF = ROPE_DIM // 2
RMS_EPS = 1e-6
GN_EPS = 64e-5
NEG_INF = -1e30
RWKV_SHIFT_WIDTH = 3 * RWKV_WIDTH + DECAY_LORA + ICLR_LORA + GATE_LORA
QKV_WIDTH = ATTN_Q_WIDTH + 2 * ATTN_KV_WIDTH
GATE_WIDTH = 2 * D_MODEL
WKV_F32_PACK = 4 * RWKV_WIDTH
WKV_BF16_PACK = 5 * RWKV_WIDTH

V7X_LANES = 128
V7X_SUBLANES = 8
V7X_VMEM_LIMIT_BYTES = 56 * 1024 * 1024

INPROJ_ROWS = 512
INPROJ_SUB_ROWS = 256
WKV_CHUNK = 64
WKV_GROUP_ROWS = 256
WKV_ROWS = 256
ATTN_ROWS = 256
TAIL_ROWS = 512
TAIL_SUB_ROWS = 256


def _dot(a, b):
    return jnp.dot(a, b, preferred_element_type=F32)


def _dot_nt(a, b):
    return lax.dot_general(a, b, (((1,), (1,)), ((), ())), preferred_element_type=F32)


def _dot_tn(a, b):
    return lax.dot_general(a, b, (((0,), (0,)), ((), ())), preferred_element_type=F32)


def _mm(a, b):
    return _dot(a.astype(BF16), b.astype(BF16))


def _mm_nt(a, b):
    return _dot_nt(a.astype(BF16), b.astype(BF16))


def _ada_kernel(c_ref, w_ref, b_ref, o_ref):
    o_ref[...] = jnp.dot(c_ref[...], w_ref[...], precision=lax.Precision.HIGHEST,
                         preferred_element_type=F32) + b_ref[...]


def _ada(c, ada_w, ada_b):
    batch = c.shape[0]
    n_out = ada_w.shape[1]
    return pl.pallas_call(
        _ada_kernel,
        grid=(n_out // D_MODEL,),
        in_specs=[
            pl.BlockSpec((batch, D_MODEL), lambda j: (0, 0)),
            pl.BlockSpec((D_MODEL, D_MODEL), lambda j: (0, j)),
            pl.BlockSpec((1, D_MODEL), lambda j: (0, j)),
        ],
        out_specs=pl.BlockSpec((batch, D_MODEL), lambda j: (0, j)),
        out_shape=jax.ShapeDtypeStruct((batch, n_out), F32),
        name="ada",
    )(c, ada_w, ada_b.reshape(1, n_out))


def _rope_kernel(pos_ref, freq_ref, sgn_ref, cos_ref, sin_ref):
    per_row = V7X_LANES // ROPE_DIM
    dense_rows = pos_ref.shape[1]
    ang = pos_ref[0].astype(F32) * freq_ref[...]
    cos_d = jnp.cos(ang)
    sin_d = jnp.sin(ang) * sgn_ref[...]
    lane = lax.broadcasted_iota(jnp.int32, ang.shape, 1)
    rotary0 = lane < ROPE_DIM
    rotary1 = (lane >= HEAD_DIM) & (lane < HEAD_DIM + ROPE_DIM)
    for i in range(per_row):
        shift = (V7X_LANES - ROPE_DIM * i) % V7X_LANES
        for dense, fill, out_ref in ((cos_d, 1.0, cos_ref), (sin_d, 0.0, sin_ref)):
            head0 = pltpu.roll(dense, shift, axis=1) if shift else dense
            head1 = pltpu.roll(head0, HEAD_DIM, axis=1)
            row = jnp.where(rotary0, head0, jnp.where(rotary1, head1, fill))
            out_ref[0, pl.ds(i, dense_rows, stride=per_row), :] = row


def _rope_tables(positions):
    batch, seq = positions.shape
    per_row = V7X_LANES // ROPE_DIM
    inv_freq = ROPE_THETA ** (-jnp.arange(ROPE_HALF, dtype=F32) / ROPE_HALF)
    dim = jnp.arange(V7X_LANES) % ROPE_DIM
    freq = inv_freq[dim % ROPE_HALF].reshape(1, V7X_LANES)
    sgn = jnp.where(dim < ROPE_HALF, -1.0, 1.0).astype(F32).reshape(1, V7X_LANES)
    pos = jnp.repeat(positions.reshape(batch, seq // per_row, per_row), ROPE_DIM, axis=-1)
    vec_spec = pl.BlockSpec((1, V7X_LANES), lambda b: (0, 0))
    tab_spec = pl.BlockSpec((1, seq, V7X_LANES), lambda b: (b, 0, 0))
    tab = jax.ShapeDtypeStruct((batch, seq, V7X_LANES), F32)
    return pl.pallas_call(
        _rope_kernel,
        grid=(batch,),
        in_specs=[pl.BlockSpec((1, seq // per_row, V7X_LANES), lambda b: (b, 0, 0)),
                  vec_spec, vec_spec],
        out_specs=[tab_spec, tab_spec],
        out_shape=[tab, tab],
        name="rope",
    )(pos, freq, sgn)


def _inproj_kernel(x_ref, ada_ref, gain_ref, w_ref, mu_ref, gb_ref, w0_ref, dup_ref, a0_ref,
                   aup_ref, gup_ref, kk_ref, ka_ref, rk_ref,
                   f32_ref, bf16_ref, wl_ref, qkv_ref, gt_ref, carry_ref):
    rows = INPROJ_SUB_ROWS
    w_rows = rows // WKV_CHUNK * V7X_SUBLANES
    ada = ada_ref[0]
    shift1 = ada[:, 0:D_MODEL]
    mod1 = gain_ref[...] * (1.0 + ada[:, D_MODEL:2 * D_MODEL])

    @pl.when(pl.program_id(1) == 0)
    def _():
        carry_ref[...] = jnp.zeros_like(carry_ref)

    last = carry_ref[...]
    subs = [slice(i * rows, (i + 1) * rows) for i in range(x_ref.shape[1] // rows)]
    hs, cols = [], []
    for rs in subs:
        x = x_ref[0, rs, :]
        inv = lax.rsqrt(jnp.mean(x * x, axis=-1, keepdims=True) + RMS_EPS)
        h = ((x * inv) * mod1 + shift1).astype(BF16)
        p = _dot(h, w_ref[:, 0:RWKV_SHIFT_WIDTH])
        prev = pltpu.roll(p, 1, axis=0)
        row = lax.broadcasted_iota(jnp.int32, p.shape, 0)
        prev = jnp.where(row == 0, last, prev)
        last = p[rows - 1:rows, :]
        hs.append(h)
        cols.append(p + (prev - p) * mu_ref[...])
    carry_ref[...] = last
    for i, rs in enumerate(subs):
        f32_pack, bf16_pack, w_pack = _wkv_prologue(cols[i], w0_ref, dup_ref, a0_ref, aup_ref,
                                                    gup_ref, kk_ref, ka_ref, rk_ref)
        f32_ref[0, rs, :] = f32_pack
        bf16_ref[0, rs, :] = bf16_pack
        wl_ref[0, i * w_rows:(i + 1) * w_rows, :] = w_pack
        qkv_ref[0, rs, :] = _dot(hs[i], w_ref[:, RWKV_SHIFT_WIDTH:RWKV_SHIFT_WIDTH + QKV_WIDTH]).astype(BF16)
        gl = _dot(hs[i], w_ref[:, RWKV_SHIFT_WIDTH + QKV_WIDTH:])
        gt_ref[0, rs, :] = jax.nn.sigmoid(gl + gb_ref[...]).astype(BF16)


def _inproj(x, ada3, gain, w_in_bf, mu, gate_b, decay_w0, decay_up, iclr_a0, iclr_up, gate_up,
            k_k, k_a, r_k):
    batch, seq, _ = x.shape
    in_width = w_in_bf.shape[1]
    tm = INPROJ_ROWS
    W = RWKV_WIDTH
    const = lambda b, j: (0, 0)
    blk = lambda b, j: (b, j, 0)
    vec = pl.BlockSpec((1, W), const)
    w_rows = tm // WKV_CHUNK * V7X_SUBLANES
    return pl.pallas_call(
        _inproj_kernel,
        grid=(batch, seq // tm),
        in_specs=[
            pl.BlockSpec((1, tm, D_MODEL), blk),
            pl.BlockSpec((1, 1, 6 * D_MODEL), lambda b, j: (b, 0, 0)),
            pl.BlockSpec((1, D_MODEL), const),
            pl.BlockSpec((D_MODEL, in_width), const, pipeline_mode=pl.Buffered(1)),
            pl.BlockSpec((1, RWKV_SHIFT_WIDTH), const),
            pl.BlockSpec((1, GATE_WIDTH), const),
            vec,
            pl.BlockSpec((DECAY_LORA, W), const),
            vec,
            pl.BlockSpec((ICLR_LORA, W), const),
            pl.BlockSpec((GATE_LORA, W), const),
            vec, vec, vec,
        ],
        out_specs=[
            pl.BlockSpec((1, tm, WKV_F32_PACK), blk),
            pl.BlockSpec((1, tm, WKV_BF16_PACK), blk),
            pl.BlockSpec((1, w_rows, W), blk),
            pl.BlockSpec((1, tm, QKV_WIDTH), blk),
            pl.BlockSpec((1, tm, GATE_WIDTH), blk),
        ],
        out_shape=[
            jax.ShapeDtypeStruct((batch, seq, WKV_F32_PACK), F32),
            jax.ShapeDtypeStruct((batch, seq, WKV_BF16_PACK), BF16),
            jax.ShapeDtypeStruct((batch, seq // WKV_CHUNK * V7X_SUBLANES, W), F32),
            jax.ShapeDtypeStruct((batch, seq, QKV_WIDTH), BF16),
            jax.ShapeDtypeStruct((batch, seq, GATE_WIDTH), BF16),
        ],
        scratch_shapes=[pltpu.VMEM((1, RWKV_SHIFT_WIDTH), F32)],
        compiler_params=pltpu.CompilerParams(
            dimension_semantics=("parallel", "arbitrary"),
            vmem_limit_bytes=V7X_VMEM_LIMIT_BYTES),
        name="inproj",
    )(x, ada3, gain, w_in_bf, mu, gate_b, decay_w0.reshape(1, W), decay_up.astype(BF16),
      iclr_a0.reshape(1, W), iclr_up.astype(BF16), gate_up.astype(BF16), k_k.reshape(1, W),
      k_a.reshape(1, W), r_k.reshape(1, W))


def _cumsum_rows(x):
    n = x.shape[0]
    row = lax.broadcasted_iota(jnp.int32, x.shape, 0)
    s = 1
    while s < min(n, V7X_SUBLANES):
        x = x + jnp.where(row >= s, pltpu.roll(x, s, axis=0), 0.0)
        s *= 2
    while s < n:
        x = jnp.concatenate([x[:s], x[s:] + x[:n - s]], axis=0)
        s *= 2
    return x


def _wkv_prologue(cols, w0_ref, dup_ref, a0_ref, aup_ref, gup_ref, kk_ref, ka_ref, rk_ref):
    L, W, HD = WKV_CHUNK, RWKV_WIDTH, HEAD_DIM
    n_chunks = cols.shape[0] // L
    r = cols[:, 0:W]
    k = cols[:, W:2 * W]
    v = cols[:, 2 * W:3 * W]
    o = 3 * W
    xw = cols[:, o:o + DECAY_LORA]
    xa = cols[:, o + DECAY_LORA:o + DECAY_LORA + ICLR_LORA]
    xg = cols[:, o + DECAY_LORA + ICLR_LORA:]

    lw = jax.nn.sigmoid(w0_ref[...] + _mm(jnp.tanh(xw), dup_ref[...])) * (-math.exp(-0.5))
    a = jax.nn.sigmoid(a0_ref[...] + _mm(xa, aup_ref[...]))
    g = _mm(jax.nn.sigmoid(xg), gup_ref[...])
    kkp = k * kk_ref[...]
    kk = kkp * jnp.minimum(lax.rsqrt(_head_mean(kkp * kkp) * float(HD)), 1e12)
    k_mod = k * (1.0 + (a - 1.0) * ka_ref[...])
    bonus = _head_mean(r * k_mod * rk_ref[...]) * float(HD) * v

    cum = jnp.concatenate([_cumsum_rows(lw[c * L:(c + 1) * L]) for c in range(n_chunks)], axis=0)
    e_in = jnp.exp(cum)
    e_neg = jnp.exp(-cum)
    w_last = [e_in[(c + 1) * L - 1:(c + 1) * L, :] for c in range(n_chunks)]
    w_rows = jnp.concatenate([jnp.broadcast_to(w, (L, W)) for w in w_last], axis=0)
    bt = kk * a * e_neg
    kt = k_mod * e_neg
    f32_pack = jnp.concatenate([-kk * jnp.exp(cum - lw), r * e_in, g, bonus], axis=1)
    bf16_pack = jnp.concatenate([bt, kt, bt * w_rows, kt * w_rows, v], axis=1).astype(BF16)
    w_pack = jnp.concatenate([jnp.broadcast_to(w, (V7X_SUBLANES, W)) for w in w_last], axis=0)
    return f32_pack, bf16_pack, w_pack


def _wkv_chunks(f32_pack, bf16_pack, w_pack, state_ref):
    L, W, HD = WKV_CHUNK, RWKV_WIDTH, HEAD_DIM
    n_chunks = f32_pack.shape[0] // L
    row2 = lax.broadcasted_iota(jnp.int32, (2 * L, 2 * L), 0)
    col2 = lax.broadcasted_iota(jnp.int32, (2 * L, 2 * L), 1) % L
    lower2 = col2 < jnp.where(row2 < L, row2, row2 - L + 1)
    lane3 = lax.broadcasted_iota(jnp.int32, (L, 3 * HD), 1)
    zeros_b = jnp.zeros((L, HD), BF16)

    units = [(c, h) for c in range(n_chunks) for h in range(RWKV_HEADS)]

    def pick(arr, c, h):
        return arr[c * L:(c + 1) * L, h * HD:(h + 1) * HD]

    at, rt = f32_pack[:, 0:W], f32_pack[:, W:2 * W]
    at_b, rt_b = at.astype(BF16), rt.astype(BF16)
    bt_b, kt_b = bf16_pack[:, 0:W], bf16_pack[:, W:2 * W]
    bth_b, kth_b = bf16_pack[:, 2 * W:3 * W], bf16_pack[:, 3 * W:4 * W]
    v_b = bf16_pack[:, 4 * W:5 * W]
    a_t = {u: pick(at, *u) for u in units}
    r_t = {u: pick(rt, *u) for u in units}
    v_h = {u: pick(v_b, *u) for u in units}
    w_l = {(c, h): w_pack[c * V7X_SUBLANES:c * V7X_SUBLANES + 1, h * HD:(h + 1) * HD]
           for c, h in units}
    bk_t = {u: jnp.concatenate([pick(bt_b, *u), pick(kt_b, *u)], axis=0) for u in units}
    bk_hat = {u: jnp.concatenate([pick(bth_b, *u), pick(kth_b, *u)], axis=0) for u in units}

    sc = {u: jnp.where(lower2, _dot_nt(jnp.concatenate([pick(at_b, *u), pick(rt_b, *u)], axis=0),
                                       bk_t[u]), 0.0) for u in units}
    sc_b = {u: sc[u].astype(BF16) for u in units}
    top = {u: sc_b[u][0:L] for u in units}
    bot = {u: sc_b[u][L:2 * L] for u in units}
    akv = {u: _dot(top[u], jnp.concatenate([zeros_b, v_h[u]], axis=0)) for u in units}

    wx = {u: jnp.concatenate([a_t[u], akv[u], sc[u][0:L, 0:HD]], axis=1) for u in units}
    levels = L.bit_length() - 1
    for _ in range(levels):
        wx_b = {u: wx[u].astype(BF16) for u in units}
        wx = {u: _dot(wx_b[u][:, 2 * HD:3 * HD], wx_b[u]) + jnp.where(lane3 < 2 * HD, wx[u], 0.0)
              for u in units}
    x2 = {u: jnp.concatenate([wx[u][:, 0:2 * HD].astype(BF16),
                              jnp.concatenate([zeros_b, v_h[u]], axis=1)], axis=0) for u in units}
    ry = {u: _dot(bot[u], x2[u]) for u in units}
    gs = {u: _dot_tn(x2[u], bk_hat[u]) for u in units}

    y_rows = []
    for c in range(n_chunks):
        y_heads = []
        for h in range(RWKV_HEADS):
            u = (c, h)
            s0 = state_ref[h]
            s0_b = s0.astype(BF16)
            y_heads.append(_dot_nt((r_t[u] + ry[u][:, 0:HD]).astype(BF16), s0_b) + ry[u][:, HD:2 * HD])
            state_ref[h] = s0 * w_l[u] + _dot(s0_b, gs[u][0:HD].astype(BF16)) + gs[u][HD:2 * HD]
        y_rows.append(jnp.concatenate(y_heads, axis=1))
    return jnp.concatenate(y_rows, axis=0)


def _head_mean(x):
    slabs = []
    for s in range(x.shape[1] // V7X_LANES):
        xs = x[:, s * V7X_LANES:(s + 1) * V7X_LANES]
        lo = lax.broadcasted_iota(jnp.int32, xs.shape, 1) < HEAD_DIM
        lo_sum = jnp.sum(jnp.where(lo, xs, 0.0), axis=-1, keepdims=True)
        hi_sum = jnp.sum(jnp.where(lo, 0.0, xs), axis=-1, keepdims=True)
        slabs.append(jnp.where(lo, lo_sum, hi_sum) * (1.0 / HEAD_DIM))
    return jnp.concatenate(slabs, axis=1)


def _wkv_epilogue(y, f32_pack, lng_ref, lnb_ref, head_mean):
    W = RWKV_WIDTH
    g, bonus = f32_pack[:, 2 * W:3 * W], f32_pack[:, 3 * W:4 * W]
    yc = y - head_mean(y)
    yn = yc * lax.rsqrt(head_mean(yc * yc) + GN_EPS) * lng_ref[...] + lnb_ref[...]
    return ((yn + bonus) * g).astype(BF16)


def _wkv_kernel(f32_ref, bf16_ref, w_ref, lng_ref, lnb_ref, hsum_ref, y_ref, state_ref):
    rows = WKV_GROUP_ROWS
    w_rows = rows // WKV_CHUNK * V7X_SUBLANES
    n_groups = f32_ref.shape[1] // rows

    @pl.when(pl.program_id(1) == 0)
    def _():
        state_ref[...] = jnp.zeros_like(state_ref)

    hsum = hsum_ref[...]
    ys = [_wkv_chunks(f32_ref[0, gi * rows:(gi + 1) * rows, :],
                      bf16_ref[0, gi * rows:(gi + 1) * rows, :],
                      w_ref[0, gi * w_rows:(gi + 1) * w_rows, :], state_ref) for gi in range(n_groups)]
    for gi in range(n_groups):
        mean = (lambda t: _mm(t, hsum)) if gi == n_groups - 1 else _head_mean
        y_ref[0, gi * rows:(gi + 1) * rows, :] = _wkv_epilogue(
            ys[gi], f32_ref[0, gi * rows:(gi + 1) * rows, :], lng_ref, lnb_ref, mean)


def _wkv(f32_pack, bf16_pack, w_pack, lnx_gain, lnx_bias):
    batch, seq, _ = f32_pack.shape
    rows = WKV_ROWS
    W = RWKV_WIDTH
    const = lambda b, j: (0, 0)
    blk = lambda b, j: (b, j, 0)
    vec = pl.BlockSpec((1, W), const)
    head = jnp.arange(W) // HEAD_DIM
    hsum = jnp.where(head[:, None] == head[None, :], 1.0 / HEAD_DIM, 0.0).astype(BF16)
    return pl.pallas_call(
        _wkv_kernel,
        grid=(batch, seq // rows),
        in_specs=[
            pl.BlockSpec((1, rows, WKV_F32_PACK), blk),
            pl.BlockSpec((1, rows, WKV_BF16_PACK), blk),
            pl.BlockSpec((1, rows // WKV_CHUNK * V7X_SUBLANES, W), blk),
            vec, vec,
            pl.BlockSpec((W, W), const),
        ],
        out_specs=pl.BlockSpec((1, rows, W), blk),
        out_shape=jax.ShapeDtypeStruct((batch, seq, W), BF16),
        scratch_shapes=[pltpu.VMEM((RWKV_HEADS, HEAD_DIM, HEAD_DIM), F32)],
        compiler_params=pltpu.CompilerParams(
            dimension_semantics=("parallel", "arbitrary")),
        name="wkv",
    )(f32_pack, bf16_pack, w_pack, lnx_gain.reshape(1, W), lnx_bias.reshape(1, W), hsum)


def _attn_kernel(sink_ref, q_ref, kv_ref, cos_ref, sin_ref, bias_ref, qg_ref, kg_ref, hmean_ref,
                 perm_ref, o_ref, kprev_ref, vprev_ref):
    n_blk = q_ref.shape[1] // BLOCK
    n_slab = ATTN_Q_WIDTH // V7X_LANES
    slab_per_kv = n_slab // ATTN_KV_HEADS
    kvs = range(ATTN_KV_HEADS)
    pars = range(2 * ATTN_KV_HEADS)

    @pl.when(pl.program_id(1) == 0)
    def _():
        kprev_ref[...] = jnp.zeros_like(kprev_ref)
        vprev_ref[...] = jnp.zeros_like(vprev_ref)

    cos, sin = cos_ref[0], sin_ref[0]
    q_gain = qg_ref[...] * (HEAD_DIM ** -0.5)
    q_all = q_ref[0].astype(F32)
    kv = kv_ref[0].astype(F32)
    slabs = [q_all[:, s * V7X_LANES:(s + 1) * V7X_LANES] for s in range(n_slab)]
    slabs.append(kv[:, 0:ATTN_KV_WIDTH])
    gains = [q_gain] * n_slab + [kg_ref[...]]
    n_rows = n_blk * BLOCK

    ms_all = _mm(jnp.concatenate([x * x for x in slabs], axis=0), hmean_ref[...])
    xn = [x * lax.rsqrt(ms_all[i * n_rows:(i + 1) * n_rows] + RMS_EPS) * gains[i]
          for i, x in enumerate(slabs)]
    partner = _mm(jnp.concatenate(xn, axis=0), perm_ref[0])
    normed = [xn[i] * cos + partner[i * n_rows:(i + 1) * n_rows] * sin for i in range(len(slabs))]
    qn, k_cur = normed[:n_slab], normed[n_slab]
    v_cur = kv[:, ATTN_KV_WIDTH:]
    swapped = _mm(jnp.concatenate([k_cur, v_cur], axis=0), perm_ref[1])
    k_swap, v_swap = swapped[0:n_rows], swapped[n_rows:]
    lo = lax.broadcasted_iota(jnp.int32, (n_rows, V7X_LANES), 1) < HEAD_DIM
    kdup_cur = [jnp.where(lo, k_cur, k_swap).astype(BF16), jnp.where(lo, k_swap, k_cur).astype(BF16)]
    vpar_cur = [jnp.where(lo, v_cur, 0.0).astype(BF16), jnp.where(lo, 0.0, v_swap).astype(BF16),
                jnp.where(lo, v_swap, 0.0).astype(BF16), jnp.where(lo, 0.0, v_cur).astype(BF16)]

    def band(prev_ref, cur, j, i):
        if i == 0:
            return jnp.concatenate([prev_ref[j], cur[j][0:BLOCK]], axis=0)
        return cur[j][(i - 1) * BLOCK:(i + 1) * BLOCK]

    units = [(i, hk) for i in range(n_blk) for hk in kvs]
    kband = {(i, hk): band(kprev_ref, kdup_cur, hk, i) for i, hk in units}
    ones_b = jnp.ones((2 * BLOCK, V7X_LANES), BF16)
    vaug = {(i, j): jnp.concatenate([band(vprev_ref, vpar_cur, j, i), ones_b], axis=1)
            for i in range(n_blk) for j in pars}
    for hk in kvs:
        kprev_ref[hk] = kdup_cur[hk][n_rows - BLOCK:]
    for j in pars:
        vprev_ref[j] = vpar_cur[j][n_rows - BLOCK:]

    lo1 = lax.broadcasted_iota(jnp.int32, (BLOCK, V7X_LANES), 1) < HEAD_DIM
    stack = 2 * slab_per_kv
    first_bias = bias_ref[jnp.minimum(pl.program_id(1), 1)]
    bias = [jnp.concatenate([first_bias if i == 0 else bias_ref[1]] * stack, axis=0)
            for i in range(n_blk)]
    heads = [[2 * (hk * slab_per_kv + j) + p for p in range(2) for j in range(slab_per_kv)]
             for hk in kvs]
    lhs = {}
    for i, hk in units:
        mine = [qn[hk * slab_per_kv + j][i * BLOCK:(i + 1) * BLOCK] for j in range(slab_per_kv)]
        lhs[(i, hk)] = jnp.concatenate([jnp.where(lo1, x, 0.0) for x in mine]
                                       + [jnp.where(lo1, 0.0, x) for x in mine], axis=0).astype(BF16)
    s = {u: _mm_nt(lhs[u], kband[u]) + bias[u[0]] for u in units}
    rmax = {u: jnp.max(s[u], axis=-1, keepdims=True) for u in units}
    m = {(u, t): jnp.maximum(rmax[u][t * BLOCK:(t + 1) * BLOCK], sink_ref[heads[u[1]][t]])
         for u in units for t in range(stack)}
    e = {u: jnp.concatenate([jnp.exp(s[u][t * BLOCK:(t + 1) * BLOCK] - m[(u, t)])
                             for t in range(stack)], axis=0).astype(BF16) for u in units}
    half = stack * BLOCK // 2
    pv = {(i, hk, p): _mm(e[(i, hk)][p * half:(p + 1) * half], vaug[(i, 2 * hk + p)])
          for i, hk in units for p in range(2)}
    extra = {(u, t): jnp.exp(sink_ref[heads[u[1]][t]] - m[(u, t)]) for u in units for t in range(stack)}
    for i, hk in units:
        for j in range(slab_per_kv):
            r0 = slice(j * BLOCK, (j + 1) * BLOCK)
            even, odd = pv[(i, hk, 0)][r0], pv[(i, hk, 1)][r0]
            num = even[:, 0:V7X_LANES] + odd[:, 0:V7X_LANES]
            den = jnp.where(lo1, even[:, V7X_LANES:] + extra[((i, hk), j)],
                            odd[:, V7X_LANES:] + extra[((i, hk), slab_per_kv + j)])
            slab = hk * slab_per_kv + j
            o_ref[0, i * BLOCK:(i + 1) * BLOCK, slab * V7X_LANES:(slab + 1) * V7X_LANES] = (
                num * (1.0 / den)).astype(BF16)


def _attn(qkv, cos_tab, sin_tab, q_gain, k_gain, sinks):
    batch, seq, _ = qkv.shape
    rows = ATTN_ROWS
    cur = lambda b, n: (b, n, 0)
    const = lambda b, n: (0, 0)
    const3 = lambda b, n: (0, 0, 0)
    kv_blk = ATTN_Q_WIDTH // (2 * ATTN_KV_WIDTH)
    gain2 = lambda gn: jnp.tile(gn.reshape(1, HEAD_DIM), (1, V7X_LANES // HEAD_DIM))
    lane = jnp.arange(V7X_LANES)
    head = lane // HEAD_DIM
    hmean = jnp.where(head[:, None] == head[None, :], 1.0 / HEAD_DIM, 0.0).astype(BF16)
    dim = lane % HEAD_DIM
    src = jnp.where(dim < ROPE_HALF, lane + ROPE_HALF, jnp.where(dim < ROPE_DIM, lane - ROPE_HALF, -1))
    partner_p = lane[:, None] == src[None, :]
    swap_p = lane[:, None] == ((lane + HEAD_DIM) % V7X_LANES)[None, :]
    perm = jnp.stack([partner_p, swap_p]).astype(BF16)
    dist = jnp.arange(BLOCK)[:, None] + BLOCK - jnp.arange(2 * BLOCK)[None, :]
    in_band = (dist >= 0) & (dist < WINDOW)
    own = (jnp.arange(2 * BLOCK) >= BLOCK)[None, :]
    bias = jnp.where(jnp.stack([in_band & own, in_band]), 0.0, NEG_INF).astype(F32)
    return pl.pallas_call(
        _attn_kernel,
        grid=(batch, seq // rows),
        in_specs=[
            pl.BlockSpec(memory_space=pltpu.SMEM),
            pl.BlockSpec((1, rows, ATTN_Q_WIDTH), cur),
            pl.BlockSpec((1, rows, 2 * ATTN_KV_WIDTH), lambda b, n: (b, n, kv_blk)),
            pl.BlockSpec((1, rows, V7X_LANES), cur),
            pl.BlockSpec((1, rows, V7X_LANES), cur),
            pl.BlockSpec((2, BLOCK, 2 * BLOCK), const3),
            pl.BlockSpec((1, V7X_LANES), const),
            pl.BlockSpec((1, V7X_LANES), const),
            pl.BlockSpec((V7X_LANES, V7X_LANES), const),
            pl.BlockSpec((2, V7X_LANES, V7X_LANES), const3),
        ],
        out_specs=pl.BlockSpec((1, rows, ATTN_Q_WIDTH), cur),
        out_shape=jax.ShapeDtypeStruct((batch, seq, ATTN_Q_WIDTH), BF16),
        scratch_shapes=[
            pltpu.VMEM((ATTN_KV_HEADS, BLOCK, V7X_LANES), BF16),
            pltpu.VMEM((2 * ATTN_KV_HEADS, BLOCK, V7X_LANES), BF16),
        ],
        compiler_params=pltpu.CompilerParams(
            dimension_semantics=("parallel", "arbitrary")),
        name="attn",
    )(sinks, qkv, qkv, cos_tab, sin_tab, bias, gain2(q_gain), gain2(k_gain), hmean, perm)


def _tail_kernel(x_ref, ada_ref, ya_ref, yb_ref, gt_ref, wa_ref, wb_ref, wo_ref, gain_ref,
                 w1_ref, w3_ref, w2_ref, o_ref):
    ada = ada_ref[0]
    gate1 = ada[:, 2 * D_MODEL:3 * D_MODEL]
    shift2 = ada[:, 3 * D_MODEL:4 * D_MODEL]
    scale2 = ada[:, 4 * D_MODEL:5 * D_MODEL]
    gate2 = ada[:, 5 * D_MODEL:6 * D_MODEL]
    mod2 = gain_ref[...] * (1.0 + scale2)

    rows = TAIL_SUB_ROWS
    subs = [slice(i * rows, (i + 1) * rows) for i in range(x_ref.shape[1] // rows)]
    ma = [_dot(ya_ref[0, rs, :], wa_ref[...]) for rs in subs]
    mb = [_dot(yb_ref[0, rs, :], wb_ref[...]) for rs in subs]
    merged = []
    for i, rs in enumerate(subs):
        gates = gt_ref[0, rs, :].astype(F32)
        merged.append((gates[:, 0:D_MODEL] * ma[i] + gates[:, D_MODEL:] * mb[i]).astype(BF16))
    x1 = [x_ref[0, rs, :] + gate1 * _dot(merged[i], wo_ref[...]) for i, rs in enumerate(subs)]
    h2 = []
    for x1_i in x1:
        inv = lax.rsqrt(jnp.mean(x1_i * x1_i, axis=-1, keepdims=True) + RMS_EPS)
        h2.append(((x1_i * inv) * mod2 + shift2).astype(BF16))
    a1 = [_dot(h, w1_ref[...]) for h in h2]
    a3 = [_dot(h, w3_ref[...]) for h in h2]
    z = [(jax.nn.silu(a1[i]) * a3[i]).astype(BF16) for i in range(len(subs))]
    for i, rs in enumerate(subs):
        o_ref[0, rs, :] = x1[i] + gate2 * _dot(z[i], w2_ref[...])


def _tail(x, ada3, ya, yb, gt, wa, wb, wo, gain2, w1, w3, w2):
    batch, seq, _ = x.shape
    tm = TAIL_ROWS
    d_ff = w1.shape[1]
    const = lambda b, j: (0, 0)
    rows = lambda width: pl.BlockSpec((1, tm, width), lambda b, j: (b, j, 0))
    weight = lambda shape: pl.BlockSpec(shape, const, pipeline_mode=pl.Buffered(1))
    return pl.pallas_call(
        _tail_kernel,
        grid=(batch, seq // tm),
        in_specs=[
            rows(D_MODEL),
            pl.BlockSpec((1, 1, 6 * D_MODEL), lambda b, j: (b, 0, 0)),
            rows(RWKV_WIDTH),
            rows(ATTN_Q_WIDTH),
            rows(GATE_WIDTH),
            weight((RWKV_WIDTH, D_MODEL)),
            weight((ATTN_Q_WIDTH, D_MODEL)),
            weight((D_MODEL, D_MODEL)),
            pl.BlockSpec((1, D_MODEL), const),
            weight((D_MODEL, d_ff)),
            weight((D_MODEL, d_ff)),
            weight((d_ff, D_MODEL)),
        ],
        out_specs=rows(D_MODEL),
        out_shape=jax.ShapeDtypeStruct((batch, seq, D_MODEL), F32),
        compiler_params=pltpu.CompilerParams(
            dimension_semantics=("parallel", "parallel"),
            vmem_limit_bytes=V7X_VMEM_LIMIT_BYTES),
        name="tail",
    )(x, ada3, ya, yb, gt, wa, wb, wo, gain2, w1, w3, w2)


def kernel(x, c, positions, ada_w, ada_b, norm1_gain, norm2_gain, w_in, tshift_mu, decay_w0,
           decay_up, iclr_a0, iclr_up, gate_up, k_k, k_a, r_k, lnx_gain, lnx_bias, q_norm_gain,
           k_norm_gain, attn_sinks, branch_gate_b, w_branch_a, w_branch_b, w_out, ffn_w1, ffn_w3,
           ffn_w2):
    depth = ada_w.shape[0]
    batch = x.shape[0]
    cos_tab, sin_tab = _rope_tables(positions)
    for l in range(depth):
        ada3 = _ada(c, ada_w[l], ada_b[l]).reshape(batch, 1, 6 * D_MODEL)
        f32_pack, bf16_pack, w_pack, qkv, gt = _inproj(
            x, ada3, norm1_gain[l].reshape(1, D_MODEL), w_in[l].astype(BF16),
            tshift_mu[l].reshape(1, RWKV_SHIFT_WIDTH), branch_gate_b[l].reshape(1, GATE_WIDTH),
            decay_w0[l], decay_up[l], iclr_a0[l], iclr_up[l], gate_up[l], k_k[l], k_a[l], r_k[l])
        ya = _wkv(f32_pack, bf16_pack, w_pack, lnx_gain[l], lnx_bias[l])
        yb = _attn(qkv, cos_tab, sin_tab, q_norm_gain[l], k_norm_gain[l], attn_sinks[l])
        x = _tail(x, ada3, ya, yb, gt, w_branch_a[l].astype(BF16), w_branch_b[l].astype(BF16),
                  w_out[l].astype(BF16), norm2_gain[l].reshape(1, D_MODEL),
                  ffn_w1[l].astype(BF16), ffn_w3[l].astype(BF16), ffn_w2[l].astype(BF16))
    return x
```

```python
import math

import jax
import jax.numpy as jnp
from jax import lax
from jax.experimental import pallas as pl
from jax.experimental.pallas import tpu as pltpu

F32 = jnp.float32
BF16 = jnp.bfloat16

D_MODEL = 1024
HEAD_DIM = 64
RWKV_HEADS = 8
RWKV_WIDTH = RWKV_HEADS * HEAD_DIM
DECAY_LORA = 64
ICLR_LORA = 64
GATE_LORA = 128
ATTN_Q_HEADS = 8
ATTN_KV_HEADS = 2
ATTN_GROUPS = ATTN_Q_HEADS // ATTN_KV_HEADS
ATTN_Q_WIDTH = ATTN_Q_HEADS * HEAD_DIM
ATTN_KV_WIDTH = ATTN_KV_HEADS * HEAD_DIM
WINDOW = 128
BLOCK = 128
ROPE_THETA = 500000.0
ROPE_DIM = HEAD_DIM // 4
ROPE_HALF = ROPE_DIM // 2
RMS_EPS = 1e-6
GN_EPS = 64e-5
NEG_INF = -1e30
RWKV_SHIFT_WIDTH = 3 * RWKV_WIDTH + DECAY_LORA + ICLR_LORA + GATE_LORA
QKV_WIDTH = ATTN_Q_WIDTH + 2 * ATTN_KV_WIDTH
GATE_WIDTH = 2 * D_MODEL
WKV_PACK_A = 4 * RWKV_WIDTH
WKV_PACK_B = 5 * RWKV_WIDTH

V7X_LANES = 128
V7X_SUBLANES = 8
V7X_VMEM_LIMIT_BYTES = 56 * 1024 * 1024

INPROJ_ROWS = 512
INPROJ_SUB_ROWS = 256
WKV_CHUNK = 64
WKV_GROUP_ROWS = 256
WKV_ROWS = 256
ATTN_ROWS = 1024
TAIL_ROWS = 512
TAIL_SUB_ROWS = 256


def _dot(a, b):
    return jnp.dot(a, b, preferred_element_type=F32)


def _dot_nt(a, b):
    return lax.dot_general(a, b, (((1,), (1,)), ((), ())), preferred_element_type=F32)


def _dot_tn(a, b):
    return lax.dot_general(a, b, (((0,), (0,)), ((), ())), preferred_element_type=F32)


def _mm(a, b):
    return _dot(a.astype(BF16), b.astype(BF16))


def _mm_nt(a, b):
    return _dot_nt(a.astype(BF16), b.astype(BF16))


def _ada_kernel(c_ref, w_ref, b_ref, o_ref):
    o_ref[...] = jnp.dot(c_ref[...], w_ref[...], precision=lax.Precision.HIGHEST,
                         preferred_element_type=F32) + b_ref[...]


def _ada(c, ada_w, ada_b):
    batch = c.shape[0]
    n_out = ada_w.shape[1]
    return pl.pallas_call(
        _ada_kernel,
        grid=(n_out // D_MODEL,),
        in_specs=[
            pl.BlockSpec((batch, D_MODEL), lambda j: (0, 0)),
            pl.BlockSpec((D_MODEL, D_MODEL), lambda j: (0, j)),
            pl.BlockSpec((1, D_MODEL), lambda j: (0, j)),
        ],
        out_specs=pl.BlockSpec((batch, D_MODEL), lambda j: (0, j)),
        out_shape=jax.ShapeDtypeStruct((batch, n_out), F32),
        name="ada",
    )(c, ada_w, ada_b.reshape(1, n_out))


def _rope_kernel(pos_ref, freq_ref, sgn_ref, cos_ref, sin_ref):
    per_row = V7X_LANES // ROPE_DIM
    dense_rows = pos_ref.shape[1]
    ang = pos_ref[0].astype(F32) * freq_ref[...]
    cos_d = jnp.cos(ang)
    sin_d = jnp.sin(ang) * sgn_ref[...]
    lane = lax.broadcasted_iota(jnp.int32, ang.shape, 1)
    rotary0 = lane < ROPE_DIM
    rotary1 = (lane >= HEAD_DIM) & (lane < HEAD_DIM + ROPE_DIM)
    for i in range(per_row):
        shift = (V7X_LANES - ROPE_DIM * i) % V7X_LANES
        for dense, fill, out_ref in ((cos_d, 1.0, cos_ref), (sin_d, 0.0, sin_ref)):
            head0 = pltpu.roll(dense, shift, axis=1) if shift else dense
            head1 = pltpu.roll(head0, HEAD_DIM, axis=1)
            row = jnp.where(rotary0, head0, jnp.where(rotary1, head1, fill))
            out_ref[0, pl.ds(i, dense_rows, stride=per_row), :] = row


def _rope_tables(positions):
    batch, seq = positions.shape
    per_row = V7X_LANES // ROPE_DIM
    inv_freq = ROPE_THETA ** (-jnp.arange(ROPE_HALF, dtype=F32) / ROPE_HALF)
    dim = jnp.arange(V7X_LANES) % ROPE_DIM
    freq = inv_freq[dim % ROPE_HALF].reshape(1, V7X_LANES)
    sgn = jnp.where(dim < ROPE_HALF, -1.0, 1.0).astype(F32).reshape(1, V7X_LANES)
    pos = jnp.repeat(positions.reshape(batch, seq // per_row, per_row), ROPE_DIM, axis=-1)
    vec_spec = pl.BlockSpec((1, V7X_LANES), lambda b: (0, 0))
    tab_spec = pl.BlockSpec((1, seq, V7X_LANES), lambda b: (b, 0, 0))
    tab = jax.ShapeDtypeStruct((batch, seq, V7X_LANES), F32)
    return pl.pallas_call(
        _rope_kernel,
        grid=(batch,),
        in_specs=[pl.BlockSpec((1, seq // per_row, V7X_LANES), lambda b: (b, 0, 0)),
                  vec_spec, vec_spec],
        out_specs=[tab_spec, tab_spec],
        out_shape=[tab, tab],
        name="rope",
    )(pos, freq, sgn)


def _inproj_kernel(x_ref, ada_ref, gain_ref, w_ref, mu_ref, gb_ref, w0_ref, dup_ref, a0_ref,
                   aup_ref, gup_ref, kk_ref, ka_ref, rk_ref,
                   pa_ref, pb_ref, wl_ref, qkv_ref, gt_ref, carry_ref):
    rows = INPROJ_SUB_ROWS
    w_rows = rows // WKV_CHUNK * V7X_SUBLANES
    ada = ada_ref[0]
    shift1 = ada[:, 0:D_MODEL]
    mod1 = gain_ref[...] * (1.0 + ada[:, D_MODEL:2 * D_MODEL])

    @pl.when(pl.program_id(1) == 0)
    def _():
        carry_ref[...] = jnp.zeros_like(carry_ref)

    last = carry_ref[...]
    subs = [slice(i * rows, (i + 1) * rows) for i in range(x_ref.shape[1] // rows)]
    hs, cols = [], []
    for rs in subs:
        x = x_ref[0, rs, :]
        inv = lax.rsqrt(jnp.mean(x * x, axis=-1, keepdims=True) + RMS_EPS)
        h = ((x * inv) * mod1 + shift1).astype(BF16)
        p = _dot(h, w_ref[:, 0:RWKV_SHIFT_WIDTH])
        prev = pltpu.roll(p, 1, axis=0)
        row = lax.broadcasted_iota(jnp.int32, p.shape, 0)
        prev = jnp.where(row == 0, last, prev)
        last = p[rows - 1:rows, :]
        hs.append(h)
        cols.append(p + (prev - p) * mu_ref[...])
    carry_ref[...] = last
    for i, rs in enumerate(subs):
        pack_a, pack_b, w_pack = _wkv_prologue(cols[i], w0_ref, dup_ref, a0_ref, aup_ref,
                                                    gup_ref, kk_ref, ka_ref, rk_ref)
        pa_ref[0, rs, :] = pack_a
        pb_ref[0, rs, :] = pack_b
        wl_ref[0, i * w_rows:(i + 1) * w_rows, :] = w_pack
        qkv_ref[0, rs, :] = _dot(hs[i], w_ref[:, RWKV_SHIFT_WIDTH:RWKV_SHIFT_WIDTH + QKV_WIDTH]).astype(BF16)
        gl = _dot(hs[i], w_ref[:, RWKV_SHIFT_WIDTH + QKV_WIDTH:])
        gt_ref[0, rs, :] = jax.nn.sigmoid(gl + gb_ref[...]).astype(BF16)


def _inproj(x, ada3, gain, w_in_bf, mu, gate_b, decay_w0, decay_up, iclr_a0, iclr_up, gate_up,
            k_k, k_a, r_k):
    batch, seq, _ = x.shape
    in_width = w_in_bf.shape[1]
    tm = INPROJ_ROWS
    W = RWKV_WIDTH
    const = lambda b, j: (0, 0)
    blk = lambda b, j: (b, j, 0)
    vec = pl.BlockSpec((1, W), const)
    w_rows = tm // WKV_CHUNK * V7X_SUBLANES
    return pl.pallas_call(
        _inproj_kernel,
        grid=(batch, seq // tm),
        in_specs=[
            pl.BlockSpec((1, tm, D_MODEL), blk),
            pl.BlockSpec((1, 1, 6 * D_MODEL), lambda b, j: (b, 0, 0)),
            pl.BlockSpec((1, D_MODEL), const),
            pl.BlockSpec((D_MODEL, in_width), const, pipeline_mode=pl.Buffered(1)),
            pl.BlockSpec((1, RWKV_SHIFT_WIDTH), const),
            pl.BlockSpec((1, GATE_WIDTH), const),
            vec,
            pl.BlockSpec((DECAY_LORA, W), const),
            vec,
            pl.BlockSpec((ICLR_LORA, W), const),
            pl.BlockSpec((GATE_LORA, W), const),
            vec, vec, vec,
        ],
        out_specs=[
            pl.BlockSpec((1, tm, WKV_PACK_A), blk),
            pl.BlockSpec((1, tm, WKV_PACK_B), blk),
            pl.BlockSpec((1, w_rows, W), blk),
            pl.BlockSpec((1, tm, QKV_WIDTH), blk),
            pl.BlockSpec((1, tm, GATE_WIDTH), blk),
        ],
        out_shape=[
            jax.ShapeDtypeStruct((batch, seq, WKV_PACK_A), BF16),
            jax.ShapeDtypeStruct((batch, seq, WKV_PACK_B), BF16),
            jax.ShapeDtypeStruct((batch, seq // WKV_CHUNK * V7X_SUBLANES, W), F32),
            jax.ShapeDtypeStruct((batch, seq, QKV_WIDTH), BF16),
            jax.ShapeDtypeStruct((batch, seq, GATE_WIDTH), BF16),
        ],
        scratch_shapes=[pltpu.VMEM((1, RWKV_SHIFT_WIDTH), F32)],
        compiler_params=pltpu.CompilerParams(
            dimension_semantics=("parallel", "arbitrary"),
            vmem_limit_bytes=V7X_VMEM_LIMIT_BYTES),
        name="inproj",
    )(x, ada3, gain, w_in_bf, mu, gate_b, decay_w0.reshape(1, W), decay_up.astype(BF16),
      iclr_a0.reshape(1, W), iclr_up.astype(BF16), gate_up.astype(BF16), k_k.reshape(1, W),
      k_a.reshape(1, W), r_k.reshape(1, W))


def _cumsum_rows(x):
    n = x.shape[0]
    row = lax.broadcasted_iota(jnp.int32, x.shape, 0)
    s = 1
    while s < min(n, V7X_SUBLANES):
        x = x + jnp.where(row >= s, pltpu.roll(x, s, axis=0), 0.0)
        s *= 2
    while s < n:
        x = jnp.concatenate([x[:s], x[s:] + x[:n - s]], axis=0)
        s *= 2
    return x


def _wkv_prologue(cols, w0_ref, dup_ref, a0_ref, aup_ref, gup_ref, kk_ref, ka_ref, rk_ref):
    L, W, HD = WKV_CHUNK, RWKV_WIDTH, HEAD_DIM
    n_chunks = cols.shape[0] // L
    r = cols[:, 0:W]
    k = cols[:, W:2 * W]
    v = cols[:, 2 * W:3 * W]
    o = 3 * W
    xw = cols[:, o:o + DECAY_LORA]
    xa = cols[:, o + DECAY_LORA:o + DECAY_LORA + ICLR_LORA]
    xg = cols[:, o + DECAY_LORA + ICLR_LORA:]

    lw = jax.nn.sigmoid(w0_ref[...] + _mm(jnp.tanh(xw), dup_ref[...])) * (-math.exp(-0.5))
    a = jax.nn.sigmoid(a0_ref[...] + _mm(xa, aup_ref[...]))
    g = _mm(jax.nn.sigmoid(xg), gup_ref[...])
    kkp = k * kk_ref[...]
    kk = kkp * jnp.minimum(lax.rsqrt(_head_mean(kkp * kkp) * float(HD)), 1e12)
    k_mod = k * (1.0 + (a - 1.0) * ka_ref[...])
    bonus = _head_mean(r * k_mod * rk_ref[...]) * float(HD) * v

    cum = jnp.concatenate([_cumsum_rows(lw[c * L:(c + 1) * L]) for c in range(n_chunks)], axis=0)
    e_in = jnp.exp(cum)
    e_neg = jnp.exp(-cum)
    w_last = [e_in[(c + 1) * L - 1:(c + 1) * L, :] for c in range(n_chunks)]
    w_rows = jnp.concatenate([jnp.broadcast_to(w, (L, W)) for w in w_last], axis=0)
    bt = kk * a * e_neg
    kt = k_mod * e_neg
    pack_a = jnp.concatenate([-kk * jnp.exp(cum - lw), r * e_in, g, bonus], axis=1).astype(BF16)
    pack_b = jnp.concatenate([bt, kt, bt * w_rows, kt * w_rows, v], axis=1).astype(BF16)
    w_pack = jnp.concatenate([jnp.broadcast_to(w, (V7X_SUBLANES, W)) for w in w_last], axis=0)
    return pack_a, pack_b, w_pack


def _wkv_chunks(pack_a, pack_b, w_pack, state_ref):
    L, W, HD = WKV_CHUNK, RWKV_WIDTH, HEAD_DIM
    n_chunks = pack_a.shape[0] // L
    row2 = lax.broadcasted_iota(jnp.int32, (2 * L, 2 * L), 0)
    col2 = lax.broadcasted_iota(jnp.int32, (2 * L, 2 * L), 1) % L
    lower2 = col2 < jnp.where(row2 < L, row2, row2 - L + 1)
    lane3 = lax.broadcasted_iota(jnp.int32, (L, 3 * HD), 1)
    zeros_b = jnp.zeros((L, HD), BF16)

    units = [(c, h) for c in range(n_chunks) for h in range(RWKV_HEADS)]

    def pick(arr, c, h):
        return arr[c * L:(c + 1) * L, h * HD:(h + 1) * HD]

    at_b, rt_b = pack_a[:, 0:W], pack_a[:, W:2 * W]
    at, rt = at_b.astype(F32), rt_b.astype(F32)
    bt_b, kt_b = pack_b[:, 0:W], pack_b[:, W:2 * W]
    bth_b, kth_b = pack_b[:, 2 * W:3 * W], pack_b[:, 3 * W:4 * W]
    v_b = pack_b[:, 4 * W:5 * W]
    a_t = {u: pick(at, *u) for u in units}
    r_t = {u: pick(rt, *u) for u in units}
    v_h = {u: pick(v_b, *u) for u in units}
    w_l = {(c, h): w_pack[c * V7X_SUBLANES:c * V7X_SUBLANES + 1, h * HD:(h + 1) * HD]
           for c, h in units}
    bk_t = {u: jnp.concatenate([pick(bt_b, *u), pick(kt_b, *u)], axis=0) for u in units}
    bk_hat = {u: jnp.concatenate([pick(bth_b, *u), pick(kth_b, *u)], axis=0) for u in units}

    sc = {u: jnp.where(lower2, _dot_nt(jnp.concatenate([pick(at_b, *u), pick(rt_b, *u)], axis=0),
                                       bk_t[u]), 0.0) for u in units}
    sc_b = {u: sc[u].astype(BF16) for u in units}
    top = {u: sc_b[u][0:L] for u in units}
    bot = {u: sc_b[u][L:2 * L] for u in units}
    akv = {u: _dot(top[u], jnp.concatenate([zeros_b, v_h[u]], axis=0)) for u in units}

    wx = {u: jnp.concatenate([a_t[u], akv[u], sc[u][0:L, 0:HD]], axis=1) for u in units}
    levels = L.bit_length() - 1
    for _ in range(levels):
        wx_b = {u: wx[u].astype(BF16) for u in units}
        wx = {u: _dot(wx_b[u][:, 2 * HD:3 * HD], wx_b[u]) + jnp.where(lane3 < 2 * HD, wx[u], 0.0)
              for u in units}
    x2 = {u: jnp.concatenate([wx[u][:, 0:2 * HD].astype(BF16),
                              jnp.concatenate([zeros_b, v_h[u]], axis=1)], axis=0) for u in units}
    ry = {u: _dot(bot[u], x2[u]) for u in units}
    gs = {u: _dot_tn(x2[u], bk_hat[u]) for u in units}

    y_rows = []
    for c in range(n_chunks):
        y_heads = []
        for h in range(RWKV_HEADS):
            u = (c, h)
            s0 = state_ref[h]
            s0_b = s0.astype(BF16)
            y_heads.append(_dot_nt((r_t[u] + ry[u][:, 0:HD]).astype(BF16), s0_b) + ry[u][:, HD:2 * HD])
            state_ref[h] = s0 * w_l[u] + _dot(s0_b, gs[u][0:HD].astype(BF16)) + gs[u][HD:2 * HD]
        y_rows.append(jnp.concatenate(y_heads, axis=1))
    return jnp.concatenate(y_rows, axis=0)


def _head_mean(x):
    slabs = []
    for s in range(x.shape[1] // V7X_LANES):
        xs = x[:, s * V7X_LANES:(s + 1) * V7X_LANES]
        lo = lax.broadcasted_iota(jnp.int32, xs.shape, 1) < HEAD_DIM
        lo_sum = jnp.sum(jnp.where(lo, xs, 0.0), axis=-1, keepdims=True)
        hi_sum = jnp.sum(jnp.where(lo, 0.0, xs), axis=-1, keepdims=True)
        slabs.append(jnp.where(lo, lo_sum, hi_sum) * (1.0 / HEAD_DIM))
    return jnp.concatenate(slabs, axis=1)


def _wkv_epilogue(y, pack_a, lng_ref, lnb_ref, head_mean):
    W = RWKV_WIDTH
    g, bonus = pack_a[:, 2 * W:3 * W].astype(F32), pack_a[:, 3 * W:4 * W].astype(F32)
    yc = y - head_mean(y)
    yn = yc * lax.rsqrt(head_mean(yc * yc) + GN_EPS) * lng_ref[...] + lnb_ref[...]
    return ((yn + bonus) * g).astype(BF16)


def _wkv_kernel(pa_ref, pb_ref, w_ref, lng_ref, lnb_ref, hsum_ref, y_ref, state_ref):
    rows = WKV_GROUP_ROWS
    w_rows = rows // WKV_CHUNK * V7X_SUBLANES
    n_groups = pa_ref.shape[1] // rows

    @pl.when(pl.program_id(1) == 0)
    def _():
        state_ref[...] = jnp.zeros_like(state_ref)

    hsum = hsum_ref[...]
    ys = [_wkv_chunks(pa_ref[0, gi * rows:(gi + 1) * rows, :],
                      pb_ref[0, gi * rows:(gi + 1) * rows, :],
                      w_ref[0, gi * w_rows:(gi + 1) * w_rows, :], state_ref) for gi in range(n_groups)]
    for gi in range(n_groups):
        mean = (lambda t: _mm(t, hsum)) if gi == n_groups - 1 else _head_mean
        y_ref[0, gi * rows:(gi + 1) * rows, :] = _wkv_epilogue(
            ys[gi], pa_ref[0, gi * rows:(gi + 1) * rows, :], lng_ref, lnb_ref, mean)


def _wkv(pack_a, pack_b, w_pack, lnx_gain, lnx_bias):
    batch, seq, _ = pack_a.shape
    rows = WKV_ROWS
    W = RWKV_WIDTH
    const = lambda b, j: (0, 0)
    blk = lambda b, j: (b, j, 0)
    vec = pl.BlockSpec((1, W), const)
    head = jnp.arange(W) // HEAD_DIM
    hsum = jnp.where(head[:, None] == head[None, :], 1.0 / HEAD_DIM, 0.0).astype(BF16)
    return pl.pallas_call(
        _wkv_kernel,
        grid=(batch, seq // rows),
        in_specs=[
            pl.BlockSpec((1, rows, WKV_PACK_A), blk),
            pl.BlockSpec((1, rows, WKV_PACK_B), blk),
            pl.BlockSpec((1, rows // WKV_CHUNK * V7X_SUBLANES, W), blk),
            vec, vec,
            pl.BlockSpec((W, W), const),
        ],
        out_specs=pl.BlockSpec((1, rows, W), blk),
        out_shape=jax.ShapeDtypeStruct((batch, seq, W), BF16),
        scratch_shapes=[pltpu.VMEM((RWKV_HEADS, HEAD_DIM, HEAD_DIM), F32)],
        compiler_params=pltpu.CompilerParams(
            dimension_semantics=("parallel", "arbitrary")),
        name="wkv",
    )(pack_a, pack_b, w_pack, lnx_gain.reshape(1, W), lnx_bias.reshape(1, W), hsum)


def _attn_kernel(sink_ref, q_ref, kv_ref, cos_ref, sin_ref, bias_ref, qg_ref, kg_ref, hmean_ref,
                 perm_ref, o_ref, kprev_ref, vprev_ref):
    n_blk = q_ref.shape[1] // BLOCK
    n_slab = ATTN_Q_WIDTH // V7X_LANES
    slab_per_kv = n_slab // ATTN_KV_HEADS
    kvs = range(ATTN_KV_HEADS)
    pars = range(2 * ATTN_KV_HEADS)

    @pl.when(pl.program_id(1) == 0)
    def _():
        kprev_ref[...] = jnp.zeros_like(kprev_ref)
        vprev_ref[...] = jnp.zeros_like(vprev_ref)

    cos, sin = cos_ref[0], sin_ref[0]
    q_gain = qg_ref[...] * (HEAD_DIM ** -0.5)
    q_all = q_ref[0].astype(F32)
    kv = kv_ref[0].astype(F32)
    slabs = [q_all[:, s * V7X_LANES:(s + 1) * V7X_LANES] for s in range(n_slab)]
    slabs.append(kv[:, 0:ATTN_KV_WIDTH])
    gains = [q_gain] * n_slab + [kg_ref[...]]
    n_rows = n_blk * BLOCK

    ms_all = _mm(jnp.concatenate([x * x for x in slabs], axis=0), hmean_ref[...])
    xn = [x * lax.rsqrt(ms_all[i * n_rows:(i + 1) * n_rows] + RMS_EPS) * gains[i]
          for i, x in enumerate(slabs)]
    partner = _mm(jnp.concatenate(xn, axis=0), perm_ref[0])
    normed = [xn[i] * cos + partner[i * n_rows:(i + 1) * n_rows] * sin for i in range(len(slabs))]
    qn, k_cur = normed[:n_slab], normed[n_slab]
    v_cur = kv[:, ATTN_KV_WIDTH:]
    swapped = _mm(jnp.concatenate([k_cur, v_cur], axis=0), perm_ref[1])
    k_swap, v_swap = swapped[0:n_rows], swapped[n_rows:]
    lo = lax.broadcasted_iota(jnp.int32, (n_rows, V7X_LANES), 1) < HEAD_DIM
    kdup_cur = [jnp.where(lo, k_cur, k_swap).astype(BF16), jnp.where(lo, k_swap, k_cur).astype(BF16)]
    vpar_cur = [jnp.where(lo, v_cur, 0.0).astype(BF16), jnp.where(lo, 0.0, v_swap).astype(BF16),
                jnp.where(lo, v_swap, 0.0).astype(BF16), jnp.where(lo, 0.0, v_cur).astype(BF16)]

    def band(prev_ref, cur, j, i):
        if i == 0:
            return jnp.concatenate([prev_ref[j], cur[j][0:BLOCK]], axis=0)
        return cur[j][(i - 1) * BLOCK:(i + 1) * BLOCK]

    units = [(i, hk) for i in range(n_blk) for hk in kvs]
    kband = {(i, hk): band(kprev_ref, kdup_cur, hk, i) for i, hk in units}
    ones_b = jnp.ones((2 * BLOCK, V7X_LANES), BF16)
    vaug = {(i, j): jnp.concatenate([band(vprev_ref, vpar_cur, j, i), ones_b], axis=1)
            for i in range(n_blk) for j in pars}
    for hk in kvs:
        kprev_ref[hk] = kdup_cur[hk][n_rows - BLOCK:]
    for j in pars:
        vprev_ref[j] = vpar_cur[j][n_rows - BLOCK:]

    lo1 = lax.broadcasted_iota(jnp.int32, (BLOCK, V7X_LANES), 1) < HEAD_DIM
    stack = 2 * slab_per_kv
    first_bias = bias_ref[jnp.minimum(pl.program_id(1), 1)]
    bias = [jnp.concatenate([first_bias if i == 0 else bias_ref[1]] * stack, axis=0)
            for i in range(n_blk)]
    heads = [[2 * (hk * slab_per_kv + j) + p for p in range(2) for j in range(slab_per_kv)]
             for hk in kvs]
    lhs = {}
    for i, hk in units:
        mine = [qn[hk * slab_per_kv + j][i * BLOCK:(i + 1) * BLOCK] for j in range(slab_per_kv)]
        lhs[(i, hk)] = jnp.concatenate([jnp.where(lo1, x, 0.0) for x in mine]
                                       + [jnp.where(lo1, 0.0, x) for x in mine], axis=0).astype(BF16)
    s = {u: _mm_nt(lhs[u], kband[u]) + bias[u[0]] for u in units}
    rmax = {u: jnp.max(s[u], axis=-1, keepdims=True) for u in units}
    m = {(u, t): jnp.maximum(rmax[u][t * BLOCK:(t + 1) * BLOCK], sink_ref[heads[u[1]][t]])
         for u in units for t in range(stack)}
    e = {u: jnp.concatenate([jnp.exp(s[u][t * BLOCK:(t + 1) * BLOCK] - m[(u, t)])
                             for t in range(stack)], axis=0).astype(BF16) for u in units}
    half = stack * BLOCK // 2
    pv = {(i, hk, p): _mm(e[(i, hk)][p * half:(p + 1) * half], vaug[(i, 2 * hk + p)])
          for i, hk in units for p in range(2)}
    extra = {(u, t): jnp.exp(sink_ref[heads[u[1]][t]] - m[(u, t)]) for u in units for t in range(stack)}
    for i, hk in units:
        for j in range(slab_per_kv):
            r0 = slice(j * BLOCK, (j + 1) * BLOCK)
            even, odd = pv[(i, hk, 0)][r0], pv[(i, hk, 1)][r0]
            num = even[:, 0:V7X_LANES] + odd[:, 0:V7X_LANES]
            den = jnp.where(lo1, even[:, V7X_LANES:] + extra[((i, hk), j)],
                            odd[:, V7X_LANES:] + extra[((i, hk), slab_per_kv + j)])
            slab = hk * slab_per_kv + j
            o_ref[0, i * BLOCK:(i + 1) * BLOCK, slab * V7X_LANES:(slab + 1) * V7X_LANES] = (
                num * (1.0 / den)).astype(BF16)


def _attn(qkv, cos_tab, sin_tab, q_gain, k_gain, sinks):
    batch, seq, _ = qkv.shape
    rows = ATTN_ROWS
    cur = lambda b, n: (b, n, 0)
    const = lambda b, n: (0, 0)
    const3 = lambda b, n: (0, 0, 0)
    kv_blk = ATTN_Q_WIDTH // (2 * ATTN_KV_WIDTH)
    gain2 = lambda gn: jnp.tile(gn.reshape(1, HEAD_DIM), (1, V7X_LANES // HEAD_DIM))
    lane = jnp.arange(V7X_LANES)
    head = lane // HEAD_DIM
    hmean = jnp.where(head[:, None] == head[None, :], 1.0 / HEAD_DIM, 0.0).astype(BF16)
    dim = lane % HEAD_DIM
    src = jnp.where(dim < ROPE_HALF, lane + ROPE_HALF, jnp.where(dim < ROPE_DIM, lane - ROPE_HALF, -1))
    partner_p = lane[:, None] == src[None, :]
    swap_p = lane[:, None] == ((lane + HEAD_DIM) % V7X_LANES)[None, :]
    perm = jnp.stack([partner_p, swap_p]).astype(BF16)
    dist = jnp.arange(BLOCK)[:, None] + BLOCK - jnp.arange(2 * BLOCK)[None, :]
    in_band = (dist >= 0) & (dist < WINDOW)
    own = (jnp.arange(2 * BLOCK) >= BLOCK)[None, :]
    bias = jnp.where(jnp.stack([in_band & own, in_band]), 0.0, NEG_INF).astype(F32)
    return pl.pallas_call(
        _attn_kernel,
        grid=(batch, seq // rows),
        in_specs=[
            pl.BlockSpec(memory_space=pltpu.SMEM),
            pl.BlockSpec((1, rows, ATTN_Q_WIDTH), cur),
            pl.BlockSpec((1, rows, 2 * ATTN_KV_WIDTH), lambda b, n: (b, n, kv_blk)),
            pl.BlockSpec((1, rows, V7X_LANES), cur),
            pl.BlockSpec((1, rows, V7X_LANES), cur),
            pl.BlockSpec((2, BLOCK, 2 * BLOCK), const3),
            pl.BlockSpec((1, V7X_LANES), const),
            pl.BlockSpec((1, V7X_LANES), const),
            pl.BlockSpec((V7X_LANES, V7X_LANES), const),
            pl.BlockSpec((2, V7X_LANES, V7X_LANES), const3),
        ],
        out_specs=pl.BlockSpec((1, rows, ATTN_Q_WIDTH), cur),
        out_shape=jax.ShapeDtypeStruct((batch, seq, ATTN_Q_WIDTH), BF16),
        scratch_shapes=[
            pltpu.VMEM((ATTN_KV_HEADS, BLOCK, V7X_LANES), BF16),
            pltpu.VMEM((2 * ATTN_KV_HEADS, BLOCK, V7X_LANES), BF16),
        ],
        compiler_params=pltpu.CompilerParams(
            dimension_semantics=("parallel", "arbitrary")),
        name="attn",
    )(sinks, qkv, qkv, cos_tab, sin_tab, bias, gain2(q_gain), gain2(k_gain), hmean, perm)


def _tail_kernel(x_ref, ada_ref, ya_ref, yb_ref, gt_ref, wa_ref, wb_ref, wo_ref, gain_ref,
                 w1_ref, w3_ref, w2_ref, o_ref):
    ada = ada_ref[0]
    gate1 = ada[:, 2 * D_MODEL:3 * D_MODEL]
    shift2 = ada[:, 3 * D_MODEL:4 * D_MODEL]
    scale2 = ada[:, 4 * D_MODEL:5 * D_MODEL]
    gate2 = ada[:, 5 * D_MODEL:6 * D_MODEL]
    mod2 = gain_ref[...] * (1.0 + scale2)

    rows = TAIL_SUB_ROWS
    subs = [slice(i * rows, (i + 1) * rows) for i in range(x_ref.shape[1] // rows)]
    ma = [_dot(ya_ref[0, rs, :], wa_ref[...]) for rs in subs]
    mb = [_dot(yb_ref[0, rs, :], wb_ref[...]) for rs in subs]
    merged = []
    for i, rs in enumerate(subs):
        gates = gt_ref[0, rs, :].astype(F32)
        merged.append((gates[:, 0:D_MODEL] * ma[i] + gates[:, D_MODEL:] * mb[i]).astype(BF16))
    x1 = [x_ref[0, rs, :] + gate1 * _dot(merged[i], wo_ref[...]) for i, rs in enumerate(subs)]
    h2 = []
    for x1_i in x1:
        inv = lax.rsqrt(jnp.mean(x1_i * x1_i, axis=-1, keepdims=True) + RMS_EPS)
        h2.append(((x1_i * inv) * mod2 + shift2).astype(BF16))
    a1 = [_dot(h, w1_ref[...]) for h in h2]
    a3 = [_dot(h, w3_ref[...]) for h in h2]
    z = [(jax.nn.silu(a1[i]) * a3[i]).astype(BF16) for i in range(len(subs))]
    for i, rs in enumerate(subs):
        o_ref[0, rs, :] = x1[i] + gate2 * _dot(z[i], w2_ref[...])


def _tail(x, ada3, ya, yb, gt, wa, wb, wo, gain2, w1, w3, w2):
    batch, seq, _ = x.shape
    tm = TAIL_ROWS
    d_ff = w1.shape[1]
    const = lambda b, j: (0, 0)
    rows = lambda width: pl.BlockSpec((1, tm, width), lambda b, j: (b, j, 0))
    weight = lambda shape: pl.BlockSpec(shape, const, pipeline_mode=pl.Buffered(1))
    return pl.pallas_call(
        _tail_kernel,
        grid=(batch, seq // tm),
        in_specs=[
            rows(D_MODEL),
            pl.BlockSpec((1, 1, 6 * D_MODEL), lambda b, j: (b, 0, 0)),
            rows(RWKV_WIDTH),
            rows(ATTN_Q_WIDTH),
            rows(GATE_WIDTH),
            weight((RWKV_WIDTH, D_MODEL)),
            weight((ATTN_Q_WIDTH, D_MODEL)),
            weight((D_MODEL, D_MODEL)),
            pl.BlockSpec((1, D_MODEL), const),
            weight((D_MODEL, d_ff)),
            weight((D_MODEL, d_ff)),
            weight((d_ff, D_MODEL)),
        ],
        out_specs=rows(D_MODEL),
        out_shape=jax.ShapeDtypeStruct((batch, seq, D_MODEL), F32),
        compiler_params=pltpu.CompilerParams(
            dimension_semantics=("parallel", "parallel"),
            vmem_limit_bytes=V7X_VMEM_LIMIT_BYTES),
        name="tail",
    )(x, ada3, ya, yb, gt, wa, wb, wo, gain2, w1, w3, w2)


def kernel(x, c, positions, ada_w, ada_b, norm1_gain, norm2_gain, w_in, tshift_mu, decay_w0,
           decay_up, iclr_a0, iclr_up, gate_up, k_k, k_a, r_k, lnx_gain, lnx_bias, q_norm_gain,
           k_norm_gain, attn_sinks, branch_gate_b, w_branch_a, w_branch_b, w_out, ffn_w1, ffn_w3,
           ffn_w2):
    depth = ada_w.shape[0]
    batch = x.shape[0]
    cos_tab, sin_tab = _rope_tables(positions)
    for l in range(depth):
        ada3 = _ada(c, ada_w[l], ada_b[l]).reshape(batch, 1, 6 * D_MODEL)
        pack_a, pack_b, w_pack, qkv, gt = _inproj(
            x, ada3, norm1_gain[l].reshape(1, D_MODEL), w_in[l].astype(BF16),
            tshift_mu[l].reshape(1, RWKV_SHIFT_WIDTH), branch_gate_b[l].reshape(1, GATE_WIDTH),
            decay_w0[l], decay_up[l], iclr_a0[l], iclr_up[l], gate_up[l], k_k[l], k_a[l], r_k[l])
        ya = _wkv(pack_a, pack_b, w_pack, lnx_gain[l], lnx_bias[l])
        yb = _attn(qkv, cos_tab, sin_tab, q_norm_gain[l], k_norm_gain[l], attn_sinks[l])
        x = _tail(x, ada3, ya, yb, gt, w_branch_a[l].astype(BF16), w_branch_b[l].astype(BF16),
                  w_out[l].astype(BF16), norm2_gain[l].reshape(1, D_MODEL),
                  ffn_w1[l].astype(BF16), ffn_w3[l].astype(BF16), ffn_w2[l].astype(BF16))
    return x
```

```python
import math

import jax
import jax.numpy as jnp
from jax import lax
from jax.experimental import pallas as pl
from jax.experimental.pallas import tpu as pltpu

F32 = jnp.float32
BF16 = jnp.bfloat16

D_MODEL = 1024
HEAD_DIM = 64
RWKV_HEADS = 8
RWKV_WIDTH = RWKV_HEADS * HEAD_DIM
DECAY_LORA = 64
ICLR_LORA = 64
GATE_LORA = 128
ATTN_Q_HEADS = 8
ATTN_KV_HEADS = 2
ATTN_GROUPS = ATTN_Q_HEADS // ATTN_KV_HEADS
ATTN_Q_WIDTH = ATTN_Q_HEADS * HEAD_DIM
ATTN_KV_WIDTH = ATTN_KV_HEADS * HEAD_DIM
WINDOW = 128
BLOCK = 128
ROPE_THETA = 500000.0
ROPE_DIM = HEAD_DIM // 4
ROPE_HALF = ROPE_DIM // 2
RMS_EPS = 1e-6
GN_EPS = 64e-5
NEG_INF = -1e30
RWKV_SHIFT_WIDTH = 3 * RWKV_WIDTH + DECAY_LORA + ICLR_LORA + GATE_LORA
QKV_WIDTH = ATTN_Q_WIDTH + 2 * ATTN_KV_WIDTH
GATE_WIDTH = 2 * D_MODEL
WKV_PACK_A = 4 * RWKV_WIDTH
WKV_PACK_B = 5 * RWKV_WIDTH

V7X_LANES = 128
V7X_SUBLANES = 8
V7X_VMEM_LIMIT_BYTES = 56 * 1024 * 1024

INPROJ_ROWS = 512
INPROJ_SUB_ROWS = 256
WKV_CHUNK = 64
WKV_GROUP_ROWS = 256
WKV_ROWS = 512
ATTN_ROWS = 1024
TAIL_ROWS = 512
TAIL_SUB_ROWS = 256


def _dot(a, b):
    return jnp.dot(a, b, preferred_element_type=F32)


def _dot_nt(a, b):
    return lax.dot_general(a, b, (((1,), (1,)), ((), ())), preferred_element_type=F32)


def _dot_tn(a, b):
    return lax.dot_general(a, b, (((0,), (0,)), ((), ())), preferred_element_type=F32)


def _mm(a, b):
    return _dot(a.astype(BF16), b.astype(BF16))


def _mm_nt(a, b):
    return _dot_nt(a.astype(BF16), b.astype(BF16))


def _ada_kernel(c_ref, w_ref, b_ref, o_ref):
    o_ref[...] = jnp.dot(c_ref[...], w_ref[...], precision=lax.Precision.HIGHEST,
                         preferred_element_type=F32) + b_ref[...]


def _ada(c, ada_w, ada_b):
    batch = c.shape[0]
    n_out = ada_w.shape[1]
    return pl.pallas_call(
        _ada_kernel,
        grid=(n_out // D_MODEL,),
        in_specs=[
            pl.BlockSpec((batch, D_MODEL), lambda j: (0, 0)),
            pl.BlockSpec((D_MODEL, D_MODEL), lambda j: (0, j)),
            pl.BlockSpec((1, D_MODEL), lambda j: (0, j)),
        ],
        out_specs=pl.BlockSpec((batch, D_MODEL), lambda j: (0, j)),
        out_shape=jax.ShapeDtypeStruct((batch, n_out), F32),
        name="ada",
    )(c, ada_w, ada_b.reshape(1, n_out))


def _rope_kernel(pos_ref, freq_ref, sgn_ref, cos_ref, sin_ref):
    per_row = V7X_LANES // ROPE_DIM
    dense_rows = pos_ref.shape[1]
    ang = pos_ref[0].astype(F32) * freq_ref[...]
    cos_d = jnp.cos(ang)
    sin_d = jnp.sin(ang) * sgn_ref[...]
    lane = lax.broadcasted_iota(jnp.int32, ang.shape, 1)
    rotary0 = lane < ROPE_DIM
    rotary1 = (lane >= HEAD_DIM) & (lane < HEAD_DIM + ROPE_DIM)
    for i in range(per_row):
        shift = (V7X_LANES - ROPE_DIM * i) % V7X_LANES
        for dense, fill, out_ref in ((cos_d, 1.0, cos_ref), (sin_d, 0.0, sin_ref)):
            head0 = pltpu.roll(dense, shift, axis=1) if shift else dense
            head1 = pltpu.roll(head0, HEAD_DIM, axis=1)
            row = jnp.where(rotary0, head0, jnp.where(rotary1, head1, fill))
            out_ref[0, pl.ds(i, dense_rows, stride=per_row), :] = row


def _rope_tables(positions):
    batch, seq = positions.shape
    per_row = V7X_LANES // ROPE_DIM
    inv_freq = ROPE_THETA ** (-jnp.arange(ROPE_HALF, dtype=F32) / ROPE_HALF)
    dim = jnp.arange(V7X_LANES) % ROPE_DIM
    freq = inv_freq[dim % ROPE_HALF].reshape(1, V7X_LANES)
    sgn = jnp.where(dim < ROPE_HALF, -1.0, 1.0).astype(F32).reshape(1, V7X_LANES)
    pos = jnp.repeat(positions.reshape(batch, seq // per_row, per_row), ROPE_DIM, axis=-1)
    vec_spec = pl.BlockSpec((1, V7X_LANES), lambda b: (0, 0))
    tab_spec = pl.BlockSpec((1, seq, V7X_LANES), lambda b: (b, 0, 0))
    tab = jax.ShapeDtypeStruct((batch, seq, V7X_LANES), F32)
    return pl.pallas_call(
        _rope_kernel,
        grid=(batch,),
        in_specs=[pl.BlockSpec((1, seq // per_row, V7X_LANES), lambda b: (b, 0, 0)),
                  vec_spec, vec_spec],
        out_specs=[tab_spec, tab_spec],
        out_shape=[tab, tab],
        name="rope",
    )(pos, freq, sgn)


def _inproj_kernel(x_ref, ada_ref, gain_ref, w_ref, mu_ref, gb_ref, w0_ref, dup_ref, a0_ref,
                   aup_ref, gup_ref, kk_ref, ka_ref, rk_ref,
                   pa_ref, pb_ref, wl_ref, qkv_ref, gt_ref, carry_ref):
    rows = INPROJ_SUB_ROWS
    w_rows = rows // WKV_CHUNK * V7X_SUBLANES
    ada = ada_ref[0]
    shift1 = ada[:, 0:D_MODEL]
    mod1 = gain_ref[...] * (1.0 + ada[:, D_MODEL:2 * D_MODEL])

    @pl.when(pl.program_id(1) == 0)
    def _():
        carry_ref[...] = jnp.zeros_like(carry_ref)

    last = carry_ref[...]
    subs = [slice(i * rows, (i + 1) * rows) for i in range(x_ref.shape[1] // rows)]
    hs, cols = [], []
    for rs in subs:
        x = x_ref[0, rs, :]
        inv = lax.rsqrt(jnp.mean(x * x, axis=-1, keepdims=True) + RMS_EPS)
        h = ((x * inv) * mod1 + shift1).astype(BF16)
        p = _dot(h, w_ref[:, 0:RWKV_SHIFT_WIDTH])
        prev = pltpu.roll(p, 1, axis=0)
        row = lax.broadcasted_iota(jnp.int32, p.shape, 0)
        prev = jnp.where(row == 0, last, prev)
        last = p[rows - 1:rows, :]
        hs.append(h)
        cols.append(p + (prev - p) * mu_ref[...])
    carry_ref[...] = last
    for i, rs in enumerate(subs):
        pack_a, pack_b, w_pack = _wkv_prologue(cols[i], w0_ref, dup_ref, a0_ref, aup_ref,
                                                    gup_ref, kk_ref, ka_ref, rk_ref)
        pa_ref[0, rs, :] = pack_a
        pb_ref[0, rs, :] = pack_b
        wl_ref[0, i * w_rows:(i + 1) * w_rows, :] = w_pack
        qkv_ref[0, rs, :] = _dot(hs[i], w_ref[:, RWKV_SHIFT_WIDTH:RWKV_SHIFT_WIDTH + QKV_WIDTH]).astype(BF16)
        gl = _dot(hs[i], w_ref[:, RWKV_SHIFT_WIDTH + QKV_WIDTH:])
        gt_ref[0, rs, :] = (gl + gb_ref[...]).astype(BF16)


def _inproj(x, ada3, gain, w_in_bf, mu, gate_b, decay_w0, decay_up, iclr_a0, iclr_up, gate_up,
            k_k, k_a, r_k):
    batch, seq, _ = x.shape
    in_width = w_in_bf.shape[1]
    tm = INPROJ_ROWS
    W = RWKV_WIDTH
    const = lambda b, j: (0, 0)
    blk = lambda b, j: (b, j, 0)
    vec = pl.BlockSpec((1, W), const)
    w_rows = tm // WKV_CHUNK * V7X_SUBLANES
    return pl.pallas_call(
        _inproj_kernel,
        grid=(batch, seq // tm),
        in_specs=[
            pl.BlockSpec((1, tm, D_MODEL), blk),
            pl.BlockSpec((1, 1, 6 * D_MODEL), lambda b, j: (b, 0, 0)),
            pl.BlockSpec((1, D_MODEL), const),
            pl.BlockSpec((D_MODEL, in_width), const, pipeline_mode=pl.Buffered(1)),
            pl.BlockSpec((1, RWKV_SHIFT_WIDTH), const),
            pl.BlockSpec((1, GATE_WIDTH), const),
            vec,
            pl.BlockSpec((DECAY_LORA, W), const),
            vec,
            pl.BlockSpec((ICLR_LORA, W), const),
            pl.BlockSpec((GATE_LORA, W), const),
            vec, vec, vec,
        ],
        out_specs=[
            pl.BlockSpec((1, tm, WKV_PACK_A), blk),
            pl.BlockSpec((1, tm, WKV_PACK_B), blk),
            pl.BlockSpec((1, w_rows, W), blk),
            pl.BlockSpec((1, tm, QKV_WIDTH), blk),
            pl.BlockSpec((1, tm, GATE_WIDTH), blk),
        ],
        out_shape=[
            jax.ShapeDtypeStruct((batch, seq, WKV_PACK_A), BF16),
            jax.ShapeDtypeStruct((batch, seq, WKV_PACK_B), BF16),
            jax.ShapeDtypeStruct((batch, seq // WKV_CHUNK * V7X_SUBLANES, W), F32),
            jax.ShapeDtypeStruct((batch, seq, QKV_WIDTH), BF16),
            jax.ShapeDtypeStruct((batch, seq, GATE_WIDTH), BF16),
        ],
        scratch_shapes=[pltpu.VMEM((1, RWKV_SHIFT_WIDTH), F32)],
        compiler_params=pltpu.CompilerParams(
            dimension_semantics=("parallel", "arbitrary"),
            vmem_limit_bytes=V7X_VMEM_LIMIT_BYTES),
        name="inproj",
    )(x, ada3, gain, w_in_bf, mu, gate_b, decay_w0.reshape(1, W), decay_up.astype(BF16),
      iclr_a0.reshape(1, W), iclr_up.astype(BF16), gate_up.astype(BF16), k_k.reshape(1, W),
      k_a.reshape(1, W), r_k.reshape(1, W))


def _cumsum_rows(x):
    n = x.shape[0]
    row = lax.broadcasted_iota(jnp.int32, x.shape, 0)
    s = 1
    while s < min(n, V7X_SUBLANES):
        x = x + jnp.where(row >= s, pltpu.roll(x, s, axis=0), 0.0)
        s *= 2
    while s < n:
        x = jnp.concatenate([x[:s], x[s:] + x[:n - s]], axis=0)
        s *= 2
    return x


def _wkv_prologue(cols, w0_ref, dup_ref, a0_ref, aup_ref, gup_ref, kk_ref, ka_ref, rk_ref):
    L, W, HD = WKV_CHUNK, RWKV_WIDTH, HEAD_DIM
    n_chunks = cols.shape[0] // L
    r = cols[:, 0:W]
    k = cols[:, W:2 * W]
    v = cols[:, 2 * W:3 * W]
    o = 3 * W
    xw = cols[:, o:o + DECAY_LORA]
    xa = cols[:, o + DECAY_LORA:o + DECAY_LORA + ICLR_LORA]
    xg = cols[:, o + DECAY_LORA + ICLR_LORA:]

    lw = jax.nn.sigmoid(w0_ref[...] + _mm(jnp.tanh(xw), dup_ref[...])) * (-math.exp(-0.5))
    a = jax.nn.sigmoid(a0_ref[...] + _mm(xa, aup_ref[...]))
    g = _mm(jax.nn.sigmoid(xg), gup_ref[...])
    kkp = k * kk_ref[...]
    kk = kkp * jnp.minimum(lax.rsqrt(_head_mean(kkp * kkp) * float(HD)), 1e12)
    k_mod = k * (1.0 + (a - 1.0) * ka_ref[...])
    bonus = _head_mean(r * k_mod * rk_ref[...]) * float(HD) * v

    cum = jnp.concatenate([_cumsum_rows(lw[c * L:(c + 1) * L]) for c in range(n_chunks)], axis=0)
    e_in = jnp.exp(cum)
    e_neg = jnp.exp(-cum)
    w_last = [e_in[(c + 1) * L - 1:(c + 1) * L, :] for c in range(n_chunks)]
    w_rows = jnp.concatenate([jnp.broadcast_to(w, (L, W)) for w in w_last], axis=0)
    bt = kk * a * e_neg
    kt = k_mod * e_neg
    pack_a = jnp.concatenate([-kk * jnp.exp(cum - lw), r * e_in, g, bonus], axis=1).astype(BF16)
    pack_b = jnp.concatenate([bt, kt, bt * w_rows, kt * w_rows, v], axis=1).astype(BF16)
    w_pack = jnp.concatenate([jnp.broadcast_to(w, (V7X_SUBLANES, W)) for w in w_last], axis=0)
    return pack_a, pack_b, w_pack


def _wkv_chunks(pack_a, pack_b, w_pack, state_ref):
    L, W, HD = WKV_CHUNK, RWKV_WIDTH, HEAD_DIM
    n_chunks = pack_a.shape[0] // L
    row2 = lax.broadcasted_iota(jnp.int32, (2 * L, 2 * L), 0)
    col2 = lax.broadcasted_iota(jnp.int32, (2 * L, 2 * L), 1) % L
    lower2 = col2 < jnp.where(row2 < L, row2, row2 - L + 1)
    lane3 = lax.broadcasted_iota(jnp.int32, (L, 3 * HD), 1)
    zeros_b = jnp.zeros((L, HD), BF16)

    units = [(c, h) for c in range(n_chunks) for h in range(RWKV_HEADS)]

    def pick(arr, c, h):
        return arr[c * L:(c + 1) * L, h * HD:(h + 1) * HD]

    at_b, rt_b = pack_a[:, 0:W], pack_a[:, W:2 * W]
    at, rt = at_b.astype(F32), rt_b.astype(F32)
    bt_b, kt_b = pack_b[:, 0:W], pack_b[:, W:2 * W]
    bth_b, kth_b = pack_b[:, 2 * W:3 * W], pack_b[:, 3 * W:4 * W]
    v_b = pack_b[:, 4 * W:5 * W]
    a_t = {u: pick(at, *u) for u in units}
    r_t = {u: pick(rt, *u) for u in units}
    v_h = {u: pick(v_b, *u) for u in units}
    w_l = {(c, h): w_pack[c * V7X_SUBLANES:c * V7X_SUBLANES + 1, h * HD:(h + 1) * HD]
           for c, h in units}
    bk_t = {u: jnp.concatenate([pick(bt_b, *u), pick(kt_b, *u)], axis=0) for u in units}
    bk_hat = {u: jnp.concatenate([pick(bth_b, *u), pick(kth_b, *u)], axis=0) for u in units}

    sc = {u: jnp.where(lower2, _dot_nt(jnp.concatenate([pick(at_b, *u), pick(rt_b, *u)], axis=0),
                                       bk_t[u]), 0.0) for u in units}
    sc_b = {u: sc[u].astype(BF16) for u in units}
    top = {u: sc_b[u][0:L] for u in units}
    bot = {u: sc_b[u][L:2 * L] for u in units}
    akv = {u: _dot(top[u], jnp.concatenate([zeros_b, v_h[u]], axis=0)) for u in units}

    wx = {u: jnp.concatenate([a_t[u], akv[u], sc[u][0:L, 0:HD]], axis=1) for u in units}
    levels = L.bit_length() - 1
    for _ in range(levels):
        wx_b = {u: wx[u].astype(BF16) for u in units}
        wx = {u: _dot(wx_b[u][:, 2 * HD:3 * HD], wx_b[u]) + jnp.where(lane3 < 2 * HD, wx[u], 0.0)
              for u in units}
    x2 = {u: jnp.concatenate([wx[u][:, 0:2 * HD].astype(BF16),
                              jnp.concatenate([zeros_b, v_h[u]], axis=1)], axis=0) for u in units}
    ry = {u: _dot(bot[u], x2[u]) for u in units}
    gs = {u: _dot_tn(x2[u], bk_hat[u]) for u in units}

    y_rows = []
    for c in range(n_chunks):
        y_heads = []
        for h in range(RWKV_HEADS):
            u = (c, h)
            s0 = state_ref[h]
            s0_b = s0.astype(BF16)
            y_heads.append(_dot_nt((r_t[u] + ry[u][:, 0:HD]).astype(BF16), s0_b) + ry[u][:, HD:2 * HD])
            state_ref[h] = s0 * w_l[u] + _dot(s0_b, gs[u][0:HD].astype(BF16)) + gs[u][HD:2 * HD]
        y_rows.append(jnp.concatenate(y_heads, axis=1))
    return jnp.concatenate(y_rows, axis=0)


def _head_mean(x):
    slabs = []
    for s in range(x.shape[1] // V7X_LANES):
        xs = x[:, s * V7X_LANES:(s + 1) * V7X_LANES]
        lo = lax.broadcasted_iota(jnp.int32, xs.shape, 1) < HEAD_DIM
        lo_sum = jnp.sum(jnp.where(lo, xs, 0.0), axis=-1, keepdims=True)
        hi_sum = jnp.sum(jnp.where(lo, 0.0, xs), axis=-1, keepdims=True)
        slabs.append(jnp.where(lo, lo_sum, hi_sum) * (1.0 / HEAD_DIM))
    return jnp.concatenate(slabs, axis=1)


def _wkv_epilogue(y, pack_a, lng_ref, lnb_ref, head_mean):
    W = RWKV_WIDTH
    g, bonus = pack_a[:, 2 * W:3 * W].astype(F32), pack_a[:, 3 * W:4 * W].astype(F32)
    yc = y - head_mean(y)
    yn = yc * lax.rsqrt(head_mean(yc * yc) + GN_EPS) * lng_ref[...] + lnb_ref[...]
    return ((yn + bonus) * g).astype(BF16)


def _wkv_kernel(pa_ref, pb_ref, w_ref, lng_ref, lnb_ref, hsum_ref, y_ref, state_ref):
    rows = WKV_GROUP_ROWS
    w_rows = rows // WKV_CHUNK * V7X_SUBLANES
    n_groups = pa_ref.shape[1] // rows

    @pl.when(pl.program_id(1) == 0)
    def _():
        state_ref[...] = jnp.zeros_like(state_ref)

    hsum = hsum_ref[...]
    ys = [_wkv_chunks(pa_ref[0, gi * rows:(gi + 1) * rows, :],
                      pb_ref[0, gi * rows:(gi + 1) * rows, :],
                      w_ref[0, gi * w_rows:(gi + 1) * w_rows, :], state_ref) for gi in range(n_groups)]
    for gi in range(n_groups):
        mean = (lambda t: _mm(t, hsum)) if gi == n_groups - 1 else _head_mean
        y_ref[0, gi * rows:(gi + 1) * rows, :] = _wkv_epilogue(
            ys[gi], pa_ref[0, gi * rows:(gi + 1) * rows, :], lng_ref, lnb_ref, mean)


def _wkv(pack_a, pack_b, w_pack, lnx_gain, lnx_bias):
    batch, seq, _ = pack_a.shape
    rows = WKV_ROWS
    W = RWKV_WIDTH
    const = lambda b, j: (0, 0)
    blk = lambda b, j: (b, j, 0)
    vec = pl.BlockSpec((1, W), const)
    head = jnp.arange(W) // HEAD_DIM
    hsum = jnp.where(head[:, None] == head[None, :], 1.0 / HEAD_DIM, 0.0).astype(BF16)
    return pl.pallas_call(
        _wkv_kernel,
        grid=(batch, seq // rows),
        in_specs=[
            pl.BlockSpec((1, rows, WKV_PACK_A), blk),
            pl.BlockSpec((1, rows, WKV_PACK_B), blk),
            pl.BlockSpec((1, rows // WKV_CHUNK * V7X_SUBLANES, W), blk),
            vec, vec,
            pl.BlockSpec((W, W), const),
        ],
        out_specs=pl.BlockSpec((1, rows, W), blk),
        out_shape=jax.ShapeDtypeStruct((batch, seq, W), BF16),
        scratch_shapes=[pltpu.VMEM((RWKV_HEADS, HEAD_DIM, HEAD_DIM), F32)],
        compiler_params=pltpu.CompilerParams(
            dimension_semantics=("parallel", "arbitrary")),
        name="wkv",
    )(pack_a, pack_b, w_pack, lnx_gain.reshape(1, W), lnx_bias.reshape(1, W), hsum)


def _attn_kernel(sink_ref, q_ref, kv_ref, cos_ref, sin_ref, bias_ref, qg_ref, kg_ref, hmean_ref,
                 perm_ref, o_ref, kprev_ref, vprev_ref):
    n_blk = q_ref.shape[1] // BLOCK
    n_slab = ATTN_Q_WIDTH // V7X_LANES
    slab_per_kv = n_slab // ATTN_KV_HEADS
    kvs = range(ATTN_KV_HEADS)
    pars = range(2 * ATTN_KV_HEADS)

    @pl.when(pl.program_id(1) == 0)
    def _():
        kprev_ref[...] = jnp.zeros_like(kprev_ref)
        vprev_ref[...] = jnp.zeros_like(vprev_ref)

    cos, sin = cos_ref[0], sin_ref[0]
    q_gain = qg_ref[...] * (HEAD_DIM ** -0.5)
    q_all = q_ref[0].astype(F32)
    kv = kv_ref[0].astype(F32)
    slabs = [q_all[:, s * V7X_LANES:(s + 1) * V7X_LANES] for s in range(n_slab)]
    slabs.append(kv[:, 0:ATTN_KV_WIDTH])
    gains = [q_gain] * n_slab + [kg_ref[...]]
    n_rows = n_blk * BLOCK

    ms_all = _mm(jnp.concatenate([x * x for x in slabs], axis=0), hmean_ref[...])
    xn = [x * lax.rsqrt(ms_all[i * n_rows:(i + 1) * n_rows] + RMS_EPS) * gains[i]
          for i, x in enumerate(slabs)]
    partner = _mm(jnp.concatenate(xn, axis=0), perm_ref[0])
    normed = [xn[i] * cos + partner[i * n_rows:(i + 1) * n_rows] * sin for i in range(len(slabs))]
    qn, k_cur = normed[:n_slab], normed[n_slab]
    v_cur = kv[:, ATTN_KV_WIDTH:]
    swapped = _mm(jnp.concatenate([k_cur, v_cur], axis=0), perm_ref[1])
    k_swap, v_swap = swapped[0:n_rows], swapped[n_rows:]
    lo = lax.broadcasted_iota(jnp.int32, (n_rows, V7X_LANES), 1) < HEAD_DIM
    kdup_cur = [jnp.where(lo, k_cur, k_swap).astype(BF16), jnp.where(lo, k_swap, k_cur).astype(BF16)]
    vpar_cur = [jnp.where(lo, v_cur, 0.0).astype(BF16), jnp.where(lo, 0.0, v_swap).astype(BF16),
                jnp.where(lo, v_swap, 0.0).astype(BF16), jnp.where(lo, 0.0, v_cur).astype(BF16)]

    def band(prev_ref, cur, j, i):
        if i == 0:
            return jnp.concatenate([prev_ref[j], cur[j][0:BLOCK]], axis=0)
        return cur[j][(i - 1) * BLOCK:(i + 1) * BLOCK]

    units = [(i, hk) for i in range(n_blk) for hk in kvs]
    kband = {(i, hk): band(kprev_ref, kdup_cur, hk, i) for i, hk in units}
    ones_b = jnp.ones((2 * BLOCK, V7X_LANES), BF16)
    vaug = {(i, j): jnp.concatenate([band(vprev_ref, vpar_cur, j, i), ones_b], axis=1)
            for i in range(n_blk) for j in pars}
    for hk in kvs:
        kprev_ref[hk] = kdup_cur[hk][n_rows - BLOCK:]
    for j in pars:
        vprev_ref[j] = vpar_cur[j][n_rows - BLOCK:]

    lo1 = lax.broadcasted_iota(jnp.int32, (BLOCK, V7X_LANES), 1) < HEAD_DIM
    stack = 2 * slab_per_kv
    first_bias = bias_ref[jnp.minimum(pl.program_id(1), 1)]
    bias = [jnp.concatenate([first_bias if i == 0 else bias_ref[1]] * stack, axis=0)
            for i in range(n_blk)]
    heads = [[2 * (hk * slab_per_kv + j) + p for p in range(2) for j in range(slab_per_kv)]
             for hk in kvs]
    lhs = {}
    for i, hk in units:
        mine = [qn[hk * slab_per_kv + j][i * BLOCK:(i + 1) * BLOCK] for j in range(slab_per_kv)]
        lhs[(i, hk)] = jnp.concatenate([jnp.where(lo1, x, 0.0) for x in mine]
                                       + [jnp.where(lo1, 0.0, x) for x in mine], axis=0).astype(BF16)
    s = {u: _mm_nt(lhs[u], kband[u]) + bias[u[0]] for u in units}
    rmax = {u: jnp.max(s[u], axis=-1, keepdims=True) for u in units}
    m = {(u, t): jnp.maximum(rmax[u][t * BLOCK:(t + 1) * BLOCK], sink_ref[heads[u[1]][t]])
         for u in units for t in range(stack)}
    e = {u: jnp.concatenate([jnp.exp(s[u][t * BLOCK:(t + 1) * BLOCK] - m[(u, t)])
                             for t in range(stack)], axis=0).astype(BF16) for u in units}
    half = stack * BLOCK // 2
    pv = {(i, hk, p): _mm(e[(i, hk)][p * half:(p + 1) * half], vaug[(i, 2 * hk + p)])
          for i, hk in units for p in range(2)}
    extra = {(u, t): jnp.exp(sink_ref[heads[u[1]][t]] - m[(u, t)]) for u in units for t in range(stack)}
    for i, hk in units:
        for j in range(slab_per_kv):
            r0 = slice(j * BLOCK, (j + 1) * BLOCK)
            even, odd = pv[(i, hk, 0)][r0], pv[(i, hk, 1)][r0]
            num = even[:, 0:V7X_LANES] + odd[:, 0:V7X_LANES]
            den = jnp.where(lo1, even[:, V7X_LANES:] + extra[((i, hk), j)],
                            odd[:, V7X_LANES:] + extra[((i, hk), slab_per_kv + j)])
            slab = hk * slab_per_kv + j
            o_ref[0, i * BLOCK:(i + 1) * BLOCK, slab * V7X_LANES:(slab + 1) * V7X_LANES] = (
                num * (1.0 / den)).astype(BF16)


def _attn(qkv, cos_tab, sin_tab, q_gain, k_gain, sinks):
    batch, seq, _ = qkv.shape
    rows = ATTN_ROWS
    cur = lambda b, n: (b, n, 0)
    const = lambda b, n: (0, 0)
    const3 = lambda b, n: (0, 0, 0)
    kv_blk = ATTN_Q_WIDTH // (2 * ATTN_KV_WIDTH)
    gain2 = lambda gn: jnp.tile(gn.reshape(1, HEAD_DIM), (1, V7X_LANES // HEAD_DIM))
    lane = jnp.arange(V7X_LANES)
    head = lane // HEAD_DIM
    hmean = jnp.where(head[:, None] == head[None, :], 1.0 / HEAD_DIM, 0.0).astype(BF16)
    dim = lane % HEAD_DIM
    src = jnp.where(dim < ROPE_HALF, lane + ROPE_HALF, jnp.where(dim < ROPE_DIM, lane - ROPE_HALF, -1))
    partner_p = lane[:, None] == src[None, :]
    swap_p = lane[:, None] == ((lane + HEAD_DIM) % V7X_LANES)[None, :]
    perm = jnp.stack([partner_p, swap_p]).astype(BF16)
    dist = jnp.arange(BLOCK)[:, None] + BLOCK - jnp.arange(2 * BLOCK)[None, :]
    in_band = (dist >= 0) & (dist < WINDOW)
    own = (jnp.arange(2 * BLOCK) >= BLOCK)[None, :]
    bias = jnp.where(jnp.stack([in_band & own, in_band]), 0.0, NEG_INF).astype(F32)
    return pl.pallas_call(
        _attn_kernel,
        grid=(batch, seq // rows),
        in_specs=[
            pl.BlockSpec(memory_space=pltpu.SMEM),
            pl.BlockSpec((1, rows, ATTN_Q_WIDTH), cur),
            pl.BlockSpec((1, rows, 2 * ATTN_KV_WIDTH), lambda b, n: (b, n, kv_blk)),
            pl.BlockSpec((1, rows, V7X_LANES), cur),
            pl.BlockSpec((1, rows, V7X_LANES), cur),
            pl.BlockSpec((2, BLOCK, 2 * BLOCK), const3),
            pl.BlockSpec((1, V7X_LANES), const),
            pl.BlockSpec((1, V7X_LANES), const),
            pl.BlockSpec((V7X_LANES, V7X_LANES), const),
            pl.BlockSpec((2, V7X_LANES, V7X_LANES), const3),
        ],
        out_specs=pl.BlockSpec((1, rows, ATTN_Q_WIDTH), cur),
        out_shape=jax.ShapeDtypeStruct((batch, seq, ATTN_Q_WIDTH), BF16),
        scratch_shapes=[
            pltpu.VMEM((ATTN_KV_HEADS, BLOCK, V7X_LANES), BF16),
            pltpu.VMEM((2 * ATTN_KV_HEADS, BLOCK, V7X_LANES), BF16),
        ],
        compiler_params=pltpu.CompilerParams(
            dimension_semantics=("parallel", "arbitrary")),
        name="attn",
    )(sinks, qkv, qkv, cos_tab, sin_tab, bias, gain2(q_gain), gain2(k_gain), hmean, perm)


def _tail_kernel(x_ref, ada_ref, ya_ref, yb_ref, gt_ref, wa_ref, wb_ref, wo_ref, gain_ref,
                 w1_ref, w3_ref, w2_ref, o_ref):
    ada = ada_ref[0]
    gate1 = ada[:, 2 * D_MODEL:3 * D_MODEL]
    shift2 = ada[:, 3 * D_MODEL:4 * D_MODEL]
    scale2 = ada[:, 4 * D_MODEL:5 * D_MODEL]
    gate2 = ada[:, 5 * D_MODEL:6 * D_MODEL]
    mod2 = gain_ref[...] * (1.0 + scale2)

    rows = TAIL_SUB_ROWS
    subs = [slice(i * rows, (i + 1) * rows) for i in range(x_ref.shape[1] // rows)]
    ma = [_dot(ya_ref[0, rs, :], wa_ref[...]) for rs in subs]
    mb = [_dot(yb_ref[0, rs, :], wb_ref[...]) for rs in subs]
    merged = []
    for i, rs in enumerate(subs):
        gates = jax.nn.sigmoid(gt_ref[0, rs, :].astype(F32))
        merged.append((gates[:, 0:D_MODEL] * ma[i] + gates[:, D_MODEL:] * mb[i]).astype(BF16))
    x1 = [x_ref[0, rs, :] + gate1 * _dot(merged[i], wo_ref[...]) for i, rs in enumerate(subs)]
    h2 = []
    for x1_i in x1:
        inv = lax.rsqrt(jnp.mean(x1_i * x1_i, axis=-1, keepdims=True) + RMS_EPS)
        h2.append(((x1_i * inv) * mod2 + shift2).astype(BF16))
    a1 = [_dot(h, w1_ref[...]) for h in h2]
    a3 = [_dot(h, w3_ref[...]) for h in h2]
    z = [(jax.nn.silu(a1[i]) * a3[i]).astype(BF16) for i in range(len(subs))]
    for i, rs in enumerate(subs):
        o_ref[0, rs, :] = x1[i] + gate2 * _dot(z[i], w2_ref[...])


def _tail(x, ada3, ya, yb, gt, wa, wb, wo, gain2, w1, w3, w2):
    batch, seq, _ = x.shape
    tm = TAIL_ROWS
    d_ff = w1.shape[1]
    const = lambda b, j: (0, 0)
    rows = lambda width: pl.BlockSpec((1, tm, width), lambda b, j: (b, j, 0))
    weight = lambda shape: pl.BlockSpec(shape, const, pipeline_mode=pl.Buffered(1))
    return pl.pallas_call(
        _tail_kernel,
        grid=(batch, seq // tm),
        in_specs=[
            rows(D_MODEL),
            pl.BlockSpec((1, 1, 6 * D_MODEL), lambda b, j: (b, 0, 0)),
            rows(RWKV_WIDTH),
            rows(ATTN_Q_WIDTH),
            rows(GATE_WIDTH),
            weight((RWKV_WIDTH, D_MODEL)),
            weight((ATTN_Q_WIDTH, D_MODEL)),
            weight((D_MODEL, D_MODEL)),
            pl.BlockSpec((1, D_MODEL), const),
            weight((D_MODEL, d_ff)),
            weight((D_MODEL, d_ff)),
            weight((d_ff, D_MODEL)),
        ],
        out_specs=rows(D_MODEL),
        out_shape=jax.ShapeDtypeStruct((batch, seq, D_MODEL), F32),
        compiler_params=pltpu.CompilerParams(
            dimension_semantics=("parallel", "parallel"),
            vmem_limit_bytes=V7X_VMEM_LIMIT_BYTES),
        name="tail",
    )(x, ada3, ya, yb, gt, wa, wb, wo, gain2, w1, w3, w2)


def kernel(x, c, positions, ada_w, ada_b, norm1_gain, norm2_gain, w_in, tshift_mu, decay_w0,
           decay_up, iclr_a0, iclr_up, gate_up, k_k, k_a, r_k, lnx_gain, lnx_bias, q_norm_gain,
           k_norm_gain, attn_sinks, branch_gate_b, w_branch_a, w_branch_b, w_out, ffn_w1, ffn_w3,
           ffn_w2):
    depth = ada_w.shape[0]
    batch = x.shape[0]
    cos_tab, sin_tab = _rope_tables(positions)
    for l in range(depth):
        ada3 = _ada(c, ada_w[l], ada_b[l]).reshape(batch, 1, 6 * D_MODEL)
        pack_a, pack_b, w_pack, qkv, gt = _inproj(
            x, ada3, norm1_gain[l].reshape(1, D_MODEL), w_in[l].astype(BF16),
            tshift_mu[l].reshape(1, RWKV_SHIFT_WIDTH), branch_gate_b[l].reshape(1, GATE_WIDTH),
            decay_w0[l], decay_up[l], iclr_a0[l], iclr_up[l], gate_up[l], k_k[l], k_a[l], r_k[l])
        ya = _wkv(pack_a, pack_b, w_pack, lnx_gain[l], lnx_bias[l])
        yb = _attn(qkv, cos_tab, sin_tab, q_norm_gain[l], k_norm_gain[l], attn_sinks[l])
        x = _tail(x, ada3, ya, yb, gt, w_branch_a[l].astype(BF16), w_branch_b[l].astype(BF16),
                  w_out[l].astype(BF16), norm2_gain[l].reshape(1, D_MODEL),
                  ffn_w1[l].astype(BF16), ffn_w3[l].astype(BF16), ffn_w2[l].astype(BF16))
    return x
```

```python
import math

import jax
import jax.numpy as jnp
from jax import lax
from jax.experimental import pallas as pl
from jax.experimental.pallas import tpu as pltpu

F32 = jnp.float32
BF16 = jnp.bfloat16

D_MODEL = 1024
HEAD_DIM = 64
RWKV_HEADS = 8
RWKV_WIDTH = RWKV_HEADS * HEAD_DIM
DECAY_LORA = 64
ICLR_LORA = 64
GATE_LORA = 128
ATTN_Q_HEADS = 8
ATTN_KV_HEADS = 2
ATTN_GROUPS = ATTN_Q_HEADS // ATTN_KV_HEADS
ATTN_Q_WIDTH = ATTN_Q_HEADS * HEAD_DIM
ATTN_KV_WIDTH = ATTN_KV_HEADS * HEAD_DIM
WINDOW = 128
BLOCK = 128
ROPE_THETA = 500000.0
ROPE_DIM = HEAD_DIM // 4
ROPE_HALF = ROPE_DIM // 2
RMS_EPS = 1e-6
GN_EPS = 64e-5
NEG_INF = -1e30
LOG2_E = math.log2(math.e)
RWKV_SHIFT_WIDTH = 3 * RWKV_WIDTH + DECAY_LORA + ICLR_LORA + GATE_LORA
QKV_WIDTH = ATTN_Q_WIDTH + 2 * ATTN_KV_WIDTH
GATE_WIDTH = 2 * D_MODEL
WKV_PACK_A = 4 * RWKV_WIDTH
WKV_PACK_B = 5 * RWKV_WIDTH

V7X_LANES = 128
V7X_SUBLANES = 8
V7X_VMEM_LIMIT_BYTES = 56 * 1024 * 1024

INPROJ_ROWS = 512
INPROJ_SUB_ROWS = 256
WKV_CHUNK = 64
WKV_GROUP_ROWS = 256
WKV_ROWS = 512
WKV_STAGE_SKEW = 4
ATTN_ROWS = 1024
TAIL_ROWS = 512
TAIL_SUB_ROWS = 256


def _dot(a, b):
    return jnp.dot(a, b, preferred_element_type=F32)


def _dot_nt(a, b):
    return lax.dot_general(a, b, (((1,), (1,)), ((), ())), preferred_element_type=F32)


def _dot_tn(a, b):
    return lax.dot_general(a, b, (((0,), (0,)), ((), ())), preferred_element_type=F32)


def _head_mean_matrix(width):
    head = jnp.arange(width) // HEAD_DIM
    return jnp.where(head[:, None] == head[None, :], 1.0 / HEAD_DIM, 0.0).astype(BF16)


def _mm(a, b):
    return _dot(a.astype(BF16), b.astype(BF16))


def _mm_nt(a, b):
    return _dot_nt(a.astype(BF16), b.astype(BF16))


def _ada_kernel(c_ref, w_ref, b_ref, o_ref):
    o_ref[...] = jnp.dot(c_ref[...], w_ref[...], precision=lax.Precision.HIGHEST,
                         preferred_element_type=F32) + b_ref[...]


def _ada(c, ada_w, ada_b):
    batch = c.shape[0]
    n_out = ada_w.shape[1]
    return pl.pallas_call(
        _ada_kernel,
        grid=(n_out // D_MODEL,),
        in_specs=[
            pl.BlockSpec((batch, D_MODEL), lambda j: (0, 0)),
            pl.BlockSpec((D_MODEL, D_MODEL), lambda j: (0, j)),
            pl.BlockSpec((1, D_MODEL), lambda j: (0, j)),
        ],
        out_specs=pl.BlockSpec((batch, D_MODEL), lambda j: (0, j)),
        out_shape=jax.ShapeDtypeStruct((batch, n_out), F32),
        name="ada",
    )(c, ada_w, ada_b.reshape(1, n_out))


def _rope_kernel(pos_ref, freq_ref, sgn_ref, cos_ref, sin_ref):
    per_row = V7X_LANES // ROPE_DIM
    dense_rows = pos_ref.shape[1]
    ang = pos_ref[0].astype(F32) * freq_ref[...]
    cos_d = jnp.cos(ang)
    sin_d = jnp.sin(ang) * sgn_ref[...]
    lane = lax.broadcasted_iota(jnp.int32, ang.shape, 1)
    rotary0 = lane < ROPE_DIM
    rotary1 = (lane >= HEAD_DIM) & (lane < HEAD_DIM + ROPE_DIM)
    for i in range(per_row):
        shift = (V7X_LANES - ROPE_DIM * i) % V7X_LANES
        for dense, fill, out_ref in ((cos_d, 1.0, cos_ref), (sin_d, 0.0, sin_ref)):
            head0 = pltpu.roll(dense, shift, axis=1) if shift else dense
            head1 = pltpu.roll(head0, HEAD_DIM, axis=1)
            row = jnp.where(rotary0, head0, jnp.where(rotary1, head1, fill))
            out_ref[0, pl.ds(i, dense_rows, stride=per_row), :] = row


def _rope_tables(positions):
    batch, seq = positions.shape
    per_row = V7X_LANES // ROPE_DIM
    inv_freq = ROPE_THETA ** (-jnp.arange(ROPE_HALF, dtype=F32) / ROPE_HALF)
    dim = jnp.arange(V7X_LANES) % ROPE_DIM
    freq = inv_freq[dim % ROPE_HALF].reshape(1, V7X_LANES)
    sgn = jnp.where(dim < ROPE_HALF, -1.0, 1.0).astype(F32).reshape(1, V7X_LANES)
    pos = jnp.repeat(positions.reshape(batch, seq // per_row, per_row), ROPE_DIM, axis=-1)
    vec_spec = pl.BlockSpec((1, V7X_LANES), lambda b: (0, 0))
    tab_spec = pl.BlockSpec((1, seq, V7X_LANES), lambda b: (b, 0, 0))
    tab = jax.ShapeDtypeStruct((batch, seq, V7X_LANES), F32)
    return pl.pallas_call(
        _rope_kernel,
        grid=(batch,),
        in_specs=[pl.BlockSpec((1, seq // per_row, V7X_LANES), lambda b: (b, 0, 0)),
                  vec_spec, vec_spec],
        out_specs=[tab_spec, tab_spec],
        out_shape=[tab, tab],
        name="rope",
    )(pos, freq, sgn)


def _inproj_kernel(x_ref, ada_ref, gain_ref, w_ref, mu_ref, gb_ref, w0_ref, dup_ref, a0_ref,
                   aup_ref, gup_ref, kk_ref, ka_ref, rk_ref,
                   pa_ref, pb_ref, wl_ref, qkv_ref, gt_ref, carry_ref):
    rows = INPROJ_SUB_ROWS
    w_rows = rows // WKV_CHUNK * V7X_SUBLANES
    ada = ada_ref[0]
    shift1 = ada[:, 0:D_MODEL]
    mod1 = gain_ref[...] * (1.0 + ada[:, D_MODEL:2 * D_MODEL])

    @pl.when(pl.program_id(1) == 0)
    def _():
        carry_ref[...] = jnp.zeros_like(carry_ref)

    last = carry_ref[...]
    subs = [slice(i * rows, (i + 1) * rows) for i in range(x_ref.shape[1] // rows)]
    hs, cols = [], []
    for rs in subs:
        x = x_ref[0, rs, :]
        inv = lax.rsqrt(jnp.mean(x * x, axis=-1, keepdims=True) + RMS_EPS)
        h = ((x * inv) * mod1 + shift1).astype(BF16)
        p = _dot(h, w_ref[:, 0:RWKV_SHIFT_WIDTH])
        prev = pltpu.roll(p, 1, axis=0)
        row = lax.broadcasted_iota(jnp.int32, p.shape, 0)
        prev = jnp.where(row == 0, last, prev)
        last = p[rows - 1:rows, :]
        hs.append(h)
        cols.append(p + (prev - p) * mu_ref[...])
    carry_ref[...] = last
    for i, rs in enumerate(subs):
        pack_a, pack_b, w_pack = _wkv_prologue(cols[i], w0_ref, dup_ref, a0_ref, aup_ref,
                                                    gup_ref, kk_ref, ka_ref, rk_ref)
        pa_ref[0, rs, :] = pack_a
        pb_ref[0, rs, :] = pack_b
        wl_ref[0, i * w_rows:(i + 1) * w_rows, :] = w_pack
        qkv_ref[0, rs, :] = _dot(hs[i], w_ref[:, RWKV_SHIFT_WIDTH:RWKV_SHIFT_WIDTH + QKV_WIDTH]).astype(BF16)
        gl = _dot(hs[i], w_ref[:, RWKV_SHIFT_WIDTH + QKV_WIDTH:])
        gt_ref[0, rs, :] = (gl + gb_ref[...]).astype(BF16)


def _inproj(x, ada3, gain, w_in_bf, mu, gate_b, decay_w0, decay_up, iclr_a0, iclr_up, gate_up,
            k_k, k_a, r_k):
    batch, seq, _ = x.shape
    in_width = w_in_bf.shape[1]
    tm = INPROJ_ROWS
    W = RWKV_WIDTH
    const = lambda b, j: (0, 0)
    blk = lambda b, j: (b, j, 0)
    vec = pl.BlockSpec((1, W), const)
    w_rows = tm // WKV_CHUNK * V7X_SUBLANES
    return pl.pallas_call(
        _inproj_kernel,
        grid=(batch, seq // tm),
        in_specs=[
            pl.BlockSpec((1, tm, D_MODEL), blk),
            pl.BlockSpec((1, 1, 6 * D_MODEL), lambda b, j: (b, 0, 0)),
            pl.BlockSpec((1, D_MODEL), const),
            pl.BlockSpec((D_MODEL, in_width), const, pipeline_mode=pl.Buffered(1)),
            pl.BlockSpec((1, RWKV_SHIFT_WIDTH), const),
            pl.BlockSpec((1, GATE_WIDTH), const),
            vec,
            pl.BlockSpec((DECAY_LORA, W), const),
            vec,
            pl.BlockSpec((ICLR_LORA, W), const),
            pl.BlockSpec((GATE_LORA, W), const),
            vec, vec, vec,
        ],
        out_specs=[
            pl.BlockSpec((1, tm, WKV_PACK_A), blk),
            pl.BlockSpec((1, tm, WKV_PACK_B), blk),
            pl.BlockSpec((1, w_rows, W), blk),
            pl.BlockSpec((1, tm, QKV_WIDTH), blk),
            pl.BlockSpec((1, tm, GATE_WIDTH), blk),
        ],
        out_shape=[
            jax.ShapeDtypeStruct((batch, seq, WKV_PACK_A), BF16),
            jax.ShapeDtypeStruct((batch, seq, WKV_PACK_B), BF16),
            jax.ShapeDtypeStruct((batch, seq // WKV_CHUNK * V7X_SUBLANES, W), F32),
            jax.ShapeDtypeStruct((batch, seq, QKV_WIDTH), BF16),
            jax.ShapeDtypeStruct((batch, seq, GATE_WIDTH), BF16),
        ],
        scratch_shapes=[pltpu.VMEM((1, RWKV_SHIFT_WIDTH), F32)],
        compiler_params=pltpu.CompilerParams(
            dimension_semantics=("parallel", "arbitrary"),
            vmem_limit_bytes=V7X_VMEM_LIMIT_BYTES),
        name="inproj",
    )(x, ada3, gain, w_in_bf, mu, gate_b, decay_w0.reshape(1, W), decay_up.astype(BF16),
      iclr_a0.reshape(1, W), iclr_up.astype(BF16), gate_up.astype(BF16), k_k.reshape(1, W),
      k_a.reshape(1, W), r_k.reshape(1, W))


def _cumsum_rows(x):
    n = x.shape[0]
    row = lax.broadcasted_iota(jnp.int32, x.shape, 0)
    s = 1
    while s < min(n, V7X_SUBLANES):
        x = x + jnp.where(row >= s, pltpu.roll(x, s, axis=0), 0.0)
        s *= 2
    while s < n:
        x = jnp.concatenate([x[:s], x[s:] + x[:n - s]], axis=0)
        s *= 2
    return x


def _wkv_prologue(cols, w0_ref, dup_ref, a0_ref, aup_ref, gup_ref, kk_ref, ka_ref, rk_ref):
    L, W, HD = WKV_CHUNK, RWKV_WIDTH, HEAD_DIM
    n_chunks = cols.shape[0] // L
    r = cols[:, 0:W]
    k = cols[:, W:2 * W]
    v = cols[:, 2 * W:3 * W]
    o = 3 * W
    xw = cols[:, o:o + DECAY_LORA]
    xa = cols[:, o + DECAY_LORA:o + DECAY_LORA + ICLR_LORA]
    xg = cols[:, o + DECAY_LORA + ICLR_LORA:]

    lw = jax.nn.sigmoid(w0_ref[...] + _mm(jnp.tanh(xw), dup_ref[...])) * (-math.exp(-0.5))
    a = jax.nn.sigmoid(a0_ref[...] + _mm(xa, aup_ref[...]))
    g = _mm(jax.nn.sigmoid(xg), gup_ref[...])
    kkp = k * kk_ref[...]
    kk = kkp * jnp.minimum(lax.rsqrt(_head_mean(kkp * kkp) * float(HD)), 1e12)
    k_mod = k * (1.0 + (a - 1.0) * ka_ref[...])
    bonus = _head_mean(r * k_mod * rk_ref[...]) * float(HD) * v

    cum = jnp.concatenate([_cumsum_rows(lw[c * L:(c + 1) * L]) for c in range(n_chunks)], axis=0)
    e_in = jnp.exp(cum)
    e_neg = jnp.exp(-cum)
    w_last = [e_in[(c + 1) * L - 1:(c + 1) * L, :] for c in range(n_chunks)]
    w_rows = jnp.concatenate([jnp.broadcast_to(w, (L, W)) for w in w_last], axis=0)
    bt = kk * a * e_neg
    kt = k_mod * e_neg
    pack_a = jnp.concatenate([-kk * jnp.exp(cum - lw), r * e_in, g, bonus], axis=1).astype(BF16)
    pack_b = jnp.concatenate([bt, kt, bt * w_rows, kt * w_rows, v], axis=1).astype(BF16)
    w_pack = jnp.concatenate([jnp.broadcast_to(w, (V7X_SUBLANES, W)) for w in w_last], axis=0)
    return pack_a, pack_b, w_pack


def _wkv_chunks(pack_a, pack_b, w_pack, state_ref):
    L, W, HD = WKV_CHUNK, RWKV_WIDTH, HEAD_DIM
    n_chunks = pack_a.shape[0] // L
    row2 = lax.broadcasted_iota(jnp.int32, (2 * L, 2 * L), 0)
    col2 = lax.broadcasted_iota(jnp.int32, (2 * L, 2 * L), 1) % L
    lower2 = col2 < jnp.where(row2 < L, row2, row2 - L + 1)
    lane3 = lax.broadcasted_iota(jnp.int32, (L, 3 * HD), 1)
    zeros_b = jnp.zeros((L, HD), BF16)

    units = [(c, h) for c in range(n_chunks) for h in range(RWKV_HEADS)]

    def pick(arr, c, h):
        return arr[c * L:(c + 1) * L, h * HD:(h + 1) * HD]

    at_b, rt_b = pack_a[:, 0:W], pack_a[:, W:2 * W]
    bt_b, kt_b = pack_b[:, 0:W], pack_b[:, W:2 * W]
    bth_b, kth_b = pack_b[:, 2 * W:3 * W], pack_b[:, 3 * W:4 * W]
    v_b = pack_b[:, 4 * W:5 * W]
    ar_b = {u: (pick(at_b, *u), pick(rt_b, *u)) for u in units}
    a_t = {u: ar_b[u][0].astype(F32) for u in units}
    r_t = {u: ar_b[u][1].astype(F32) for u in units}
    v_h = {u: pick(v_b, *u) for u in units}
    w_l = {(c, h): w_pack[c * V7X_SUBLANES:c * V7X_SUBLANES + 1, h * HD:(h + 1) * HD]
           for c, h in units}
    bk_t = {u: jnp.concatenate([pick(bt_b, *u), pick(kt_b, *u)], axis=0) for u in units}
    bk_hat = {u: jnp.concatenate([pick(bth_b, *u), pick(kth_b, *u)], axis=0) for u in units}

    sc = {u: jnp.where(lower2, _dot_nt(jnp.concatenate(ar_b[u], axis=0), bk_t[u]), 0.0)
          for u in units}
    yield None
    sc_b = {u: sc[u].astype(BF16) for u in units}
    top = {u: sc_b[u][0:L] for u in units}
    bot = {u: sc_b[u][L:2 * L] for u in units}
    vz = {u: jnp.concatenate([zeros_b, v_h[u]], axis=1) for u in units}
    zeros_w = jnp.zeros((L, 2 * HD), BF16)
    akv = {u: _dot(top[u], jnp.concatenate([zeros_w, vz[u]], axis=0)) for u in units}
    yield None

    zeros_f = jnp.zeros((L, HD), F32)
    wx = {u: jnp.concatenate([akv[u] + jnp.concatenate([a_t[u], zeros_f], axis=1),
                              sc[u][0:L, 0:HD]], axis=1) for u in units}
    levels = L.bit_length() - 1
    for _ in range(levels):
        wx_b = {u: wx[u].astype(BF16) for u in units}
        wx = {u: _dot(wx_b[u][:, 2 * HD:3 * HD], wx_b[u]) + jnp.where(lane3 < 2 * HD, wx[u], 0.0)
              for u in units}
        yield None
    x2 = {u: jnp.concatenate([wx[u][:, 0:2 * HD].astype(BF16), vz[u]], axis=0) for u in units}
    ry = {u: _dot(bot[u], x2[u]) for u in units}
    yield None
    gs = {u: _dot_tn(x2[u], bk_hat[u]) for u in units}
    yield None

    y_rows = []
    for c in range(n_chunks):
        y_heads = []
        for h in range(RWKV_HEADS):
            u = (c, h)
            s0 = state_ref[h]
            s0_b = s0.astype(BF16)
            y_heads.append(_dot_nt((r_t[u] + ry[u][:, 0:HD]).astype(BF16), s0_b) + ry[u][:, HD:2 * HD])
            state_ref[h] = s0 * w_l[u] + _dot(s0_b, gs[u][0:HD].astype(BF16)) + gs[u][HD:2 * HD]
        y_rows.append(jnp.concatenate(y_heads, axis=1))
    yield jnp.concatenate(y_rows, axis=0)


def _head_mean(x):
    slabs = []
    for s in range(x.shape[1] // V7X_LANES):
        xs = x[:, s * V7X_LANES:(s + 1) * V7X_LANES]
        lo = lax.broadcasted_iota(jnp.int32, xs.shape, 1) < HEAD_DIM
        lo_sum = jnp.sum(jnp.where(lo, xs, 0.0), axis=-1, keepdims=True)
        hi_sum = jnp.sum(jnp.where(lo, 0.0, xs), axis=-1, keepdims=True)
        slabs.append(jnp.where(lo, lo_sum, hi_sum) * (1.0 / HEAD_DIM))
    return jnp.concatenate(slabs, axis=1)


def _wkv_epilogue(y, pack_a, lng_ref, lnb_ref, head_mean):
    W = RWKV_WIDTH
    g, bonus = pack_a[:, 2 * W:3 * W].astype(F32), pack_a[:, 3 * W:4 * W].astype(F32)
    yc = y - head_mean(y)
    yn = yc * lax.rsqrt(head_mean(yc * yc) + GN_EPS) * lng_ref[...] + lnb_ref[...]
    return ((yn + bonus) * g).astype(BF16)


def _wkv_kernel(pa_ref, pb_ref, w_ref, lng_ref, lnb_ref, hsum_ref, y_ref, state_ref):
    rows = WKV_GROUP_ROWS
    w_rows = rows // WKV_CHUNK * V7X_SUBLANES
    n_groups = pa_ref.shape[1] // rows

    @pl.when(pl.program_id(1) == 0)
    def _():
        state_ref[...] = jnp.zeros_like(state_ref)

    hsum = hsum_ref[...]
    scans = [_wkv_chunks(pa_ref[0, gi * rows:(gi + 1) * rows, :],
                         pb_ref[0, gi * rows:(gi + 1) * rows, :],
                         w_ref[0, gi * w_rows:(gi + 1) * w_rows, :], state_ref) for gi in range(n_groups)]
    n_stages = WKV_CHUNK.bit_length() - 1 + 5
    ys = [None] * n_groups
    for t in range(n_stages + WKV_STAGE_SKEW * (n_groups - 1)):
        for gi in range(n_groups):
            stage = t - WKV_STAGE_SKEW * gi
            if 0 <= stage < n_stages:
                ys[gi] = next(scans[gi])
    for gi in range(n_groups):
        mean = (lambda t: _mm(t, hsum)) if gi == n_groups - 1 else _head_mean
        y_ref[0, gi * rows:(gi + 1) * rows, :] = _wkv_epilogue(
            ys[gi], pa_ref[0, gi * rows:(gi + 1) * rows, :], lng_ref, lnb_ref, mean)


def _wkv(pack_a, pack_b, w_pack, lnx_gain, lnx_bias):
    batch, seq, _ = pack_a.shape
    rows = WKV_ROWS
    W = RWKV_WIDTH
    const = lambda b, j: (0, 0)
    blk = lambda b, j: (b, j, 0)
    vec = pl.BlockSpec((1, W), const)
    hsum = _head_mean_matrix(W)
    return pl.pallas_call(
        _wkv_kernel,
        grid=(batch, seq // rows),
        in_specs=[
            pl.BlockSpec((1, rows, WKV_PACK_A), blk),
            pl.BlockSpec((1, rows, WKV_PACK_B), blk),
            pl.BlockSpec((1, rows // WKV_CHUNK * V7X_SUBLANES, W), blk),
            vec, vec,
            pl.BlockSpec((W, W), const),
        ],
        out_specs=pl.BlockSpec((1, rows, W), blk),
        out_shape=jax.ShapeDtypeStruct((batch, seq, W), BF16),
        scratch_shapes=[pltpu.VMEM((RWKV_HEADS, HEAD_DIM, HEAD_DIM), F32)],
        compiler_params=pltpu.CompilerParams(
            dimension_semantics=("parallel", "arbitrary")),
        name="wkv",
    )(pack_a, pack_b, w_pack, lnx_gain.reshape(1, W), lnx_bias.reshape(1, W), hsum)


def _attn_kernel(sink_ref, q_ref, kv_ref, cos_ref, sin_ref, bias_ref, qg_ref, kg_ref, hmean_ref,
                 perm_ref, o_ref, kprev_ref, vprev_ref):
    n_blk = q_ref.shape[1] // BLOCK
    n_slab = ATTN_Q_WIDTH // V7X_LANES
    slab_per_kv = n_slab // ATTN_KV_HEADS
    kvs = range(ATTN_KV_HEADS)
    pars = range(2 * ATTN_KV_HEADS)

    @pl.when(pl.program_id(1) == 0)
    def _():
        kprev_ref[...] = jnp.zeros_like(kprev_ref)
        vprev_ref[...] = jnp.zeros_like(vprev_ref)

    cos, sin = cos_ref[0], sin_ref[0]
    q_gain = qg_ref[...] * (HEAD_DIM ** -0.5 * LOG2_E)
    q_all = q_ref[0].astype(F32)
    kv = kv_ref[0].astype(F32)
    slabs = [q_all[:, s * V7X_LANES:(s + 1) * V7X_LANES] for s in range(n_slab)]
    slabs.append(kv[:, 0:ATTN_KV_WIDTH])
    gains = [q_gain] * n_slab + [kg_ref[...]]
    n_rows = n_blk * BLOCK

    ms_all = _mm(jnp.concatenate([x * x for x in slabs], axis=0), hmean_ref[...])
    xn = [x * lax.rsqrt(ms_all[i * n_rows:(i + 1) * n_rows] + RMS_EPS) * gains[i]
          for i, x in enumerate(slabs)]
    partner = _mm(jnp.concatenate(xn, axis=0), perm_ref[0])
    normed = [xn[i] * cos + partner[i * n_rows:(i + 1) * n_rows] * sin for i in range(len(slabs))]
    qn, k_cur = normed[:n_slab], normed[n_slab]
    v_cur = kv[:, ATTN_KV_WIDTH:]
    swapped = _mm(jnp.concatenate([k_cur, v_cur], axis=0), perm_ref[1])
    k_swap, v_swap = swapped[0:n_rows], swapped[n_rows:]
    lo = lax.broadcasted_iota(jnp.int32, (n_rows, V7X_LANES), 1) < HEAD_DIM
    kdup_cur = [jnp.where(lo, k_cur, k_swap).astype(BF16), jnp.where(lo, k_swap, k_cur).astype(BF16)]
    vpar_cur = [jnp.where(lo, v_cur, 0.0).astype(BF16), jnp.where(lo, 0.0, v_swap).astype(BF16),
                jnp.where(lo, v_swap, 0.0).astype(BF16), jnp.where(lo, 0.0, v_cur).astype(BF16)]

    def band(prev_ref, cur, j, i):
        if i == 0:
            return jnp.concatenate([prev_ref[j], cur[j][0:BLOCK]], axis=0)
        return cur[j][(i - 1) * BLOCK:(i + 1) * BLOCK]

    units = [(i, hk) for i in range(n_blk) for hk in kvs]
    kband = {(i, hk): band(kprev_ref, kdup_cur, hk, i) for i, hk in units}
    ones_b = jnp.ones((2 * BLOCK, V7X_LANES), BF16)
    vaug = {(i, j): jnp.concatenate([band(vprev_ref, vpar_cur, j, i), ones_b], axis=1)
            for i in range(n_blk) for j in pars}
    for hk in kvs:
        kprev_ref[hk] = kdup_cur[hk][n_rows - BLOCK:]
    for j in pars:
        vprev_ref[j] = vpar_cur[j][n_rows - BLOCK:]

    lo1 = lax.broadcasted_iota(jnp.int32, (BLOCK, V7X_LANES), 1) < HEAD_DIM
    stack = 2 * slab_per_kv
    first_bias = bias_ref[jnp.minimum(pl.program_id(1), 1)]
    bias = [jnp.concatenate([first_bias if i == 0 else bias_ref[1]] * stack, axis=0)
            for i in range(n_blk)]
    heads = [[2 * (hk * slab_per_kv + j) + p for p in range(2) for j in range(slab_per_kv)]
             for hk in kvs]
    lhs = {}
    for i, hk in units:
        mine = [qn[hk * slab_per_kv + j][i * BLOCK:(i + 1) * BLOCK] for j in range(slab_per_kv)]
        lhs[(i, hk)] = jnp.concatenate([jnp.where(lo1, x, 0.0) for x in mine]
                                       + [jnp.where(lo1, 0.0, x) for x in mine], axis=0).astype(BF16)
    s = {u: _mm_nt(lhs[u], kband[u]) + bias[u[0]] for u in units}
    rmax = {u: jnp.max(s[u], axis=-1, keepdims=True) for u in units}
    sink2 = [sink_ref[h] * LOG2_E for h in range(ATTN_Q_HEADS)]
    m = {(u, t): jnp.maximum(rmax[u][t * BLOCK:(t + 1) * BLOCK], sink2[heads[u[1]][t]])
         for u in units for t in range(stack)}
    e = {u: jnp.concatenate([jnp.exp2(s[u][t * BLOCK:(t + 1) * BLOCK] - m[(u, t)])
                             for t in range(stack)], axis=0).astype(BF16) for u in units}
    half = stack * BLOCK // 2
    pv = {(i, hk, p): _mm(e[(i, hk)][p * half:(p + 1) * half], vaug[(i, 2 * hk + p)])
          for i, hk in units for p in range(2)}
    extra = {(u, t): jnp.exp2(sink2[heads[u[1]][t]] - m[(u, t)]) for u in units for t in range(stack)}
    for i, hk in units:
        for j in range(slab_per_kv):
            r0 = slice(j * BLOCK, (j + 1) * BLOCK)
            even, odd = pv[(i, hk, 0)][r0], pv[(i, hk, 1)][r0]
            num = even[:, 0:V7X_LANES] + odd[:, 0:V7X_LANES]
            den = jnp.where(lo1, even[:, V7X_LANES:] + extra[((i, hk), j)],
                            odd[:, V7X_LANES:] + extra[((i, hk), slab_per_kv + j)])
            slab = hk * slab_per_kv + j
            o_ref[0, i * BLOCK:(i + 1) * BLOCK, slab * V7X_LANES:(slab + 1) * V7X_LANES] = (
                num * (1.0 / den)).astype(BF16)


def _attn(qkv, cos_tab, sin_tab, q_gain, k_gain, sinks):
    batch, seq, _ = qkv.shape
    rows = ATTN_ROWS
    cur = lambda b, n: (b, n, 0)
    const = lambda b, n: (0, 0)
    const3 = lambda b, n: (0, 0, 0)
    kv_blk = ATTN_Q_WIDTH // (2 * ATTN_KV_WIDTH)
    gain2 = lambda gn: jnp.tile(gn.reshape(1, HEAD_DIM), (1, V7X_LANES // HEAD_DIM))
    lane = jnp.arange(V7X_LANES)
    hmean = _head_mean_matrix(V7X_LANES)
    dim = lane % HEAD_DIM
    src = jnp.where(dim < ROPE_HALF, lane + ROPE_HALF, jnp.where(dim < ROPE_DIM, lane - ROPE_HALF, -1))
    partner_p = lane[:, None] == src[None, :]
    swap_p = lane[:, None] == ((lane + HEAD_DIM) % V7X_LANES)[None, :]
    perm = jnp.stack([partner_p, swap_p]).astype(BF16)
    dist = jnp.arange(BLOCK)[:, None] + BLOCK - jnp.arange(2 * BLOCK)[None, :]
    in_band = (dist >= 0) & (dist < WINDOW)
    own = (jnp.arange(2 * BLOCK) >= BLOCK)[None, :]
    bias = jnp.where(jnp.stack([in_band & own, in_band]), 0.0, NEG_INF).astype(F32)
    return pl.pallas_call(
        _attn_kernel,
        grid=(batch, seq // rows),
        in_specs=[
            pl.BlockSpec(memory_space=pltpu.SMEM),
            pl.BlockSpec((1, rows, ATTN_Q_WIDTH), cur),
            pl.BlockSpec((1, rows, 2 * ATTN_KV_WIDTH), lambda b, n: (b, n, kv_blk)),
            pl.BlockSpec((1, rows, V7X_LANES), cur),
            pl.BlockSpec((1, rows, V7X_LANES), cur),
            pl.BlockSpec((2, BLOCK, 2 * BLOCK), const3),
            pl.BlockSpec((1, V7X_LANES), const),
            pl.BlockSpec((1, V7X_LANES), const),
            pl.BlockSpec((V7X_LANES, V7X_LANES), const),
            pl.BlockSpec((2, V7X_LANES, V7X_LANES), const3),
        ],
        out_specs=pl.BlockSpec((1, rows, ATTN_Q_WIDTH), cur),
        out_shape=jax.ShapeDtypeStruct((batch, seq, ATTN_Q_WIDTH), BF16),
        scratch_shapes=[
            pltpu.VMEM((ATTN_KV_HEADS, BLOCK, V7X_LANES), BF16),
            pltpu.VMEM((2 * ATTN_KV_HEADS, BLOCK, V7X_LANES), BF16),
        ],
        compiler_params=pltpu.CompilerParams(
            dimension_semantics=("parallel", "arbitrary")),
        name="attn",
    )(sinks, qkv, qkv, cos_tab, sin_tab, bias, gain2(q_gain), gain2(k_gain), hmean, perm)


def _tail_kernel(x_ref, ada_ref, ya_ref, yb_ref, gt_ref, wa_ref, wb_ref, wo_ref, gain_ref,
                 w1_ref, w3_ref, w2_ref, o_ref):
    ada = ada_ref[0]
    gate1 = ada[:, 2 * D_MODEL:3 * D_MODEL]
    shift2 = ada[:, 3 * D_MODEL:4 * D_MODEL]
    scale2 = ada[:, 4 * D_MODEL:5 * D_MODEL]
    gate2 = ada[:, 5 * D_MODEL:6 * D_MODEL]
    mod2 = gain_ref[...] * (1.0 + scale2)

    rows = TAIL_SUB_ROWS
    subs = [slice(i * rows, (i + 1) * rows) for i in range(x_ref.shape[1] // rows)]
    ma = [_dot(ya_ref[0, rs, :], wa_ref[...]) for rs in subs]
    mb = [_dot(yb_ref[0, rs, :], wb_ref[...]) for rs in subs]
    merged = []
    for i, rs in enumerate(subs):
        gates = jax.nn.sigmoid(gt_ref[0, rs, :].astype(F32))
        merged.append((gates[:, 0:D_MODEL] * ma[i] + gates[:, D_MODEL:] * mb[i]).astype(BF16))
    x1 = [x_ref[0, rs, :] + gate1 * _dot(merged[i], wo_ref[...]) for i, rs in enumerate(subs)]
    h2 = []
    for x1_i in x1:
        inv = lax.rsqrt(jnp.mean(x1_i * x1_i, axis=-1, keepdims=True) + RMS_EPS)
        h2.append(((x1_i * inv) * mod2 + shift2).astype(BF16))
    a1 = [_dot(h, w1_ref[...]) for h in h2]
    a3 = [_dot(h, w3_ref[...]) for h in h2]
    z = [(jax.nn.silu(a1[i]) * a3[i]).astype(BF16) for i in range(len(subs))]
    for i, rs in enumerate(subs):
        o_ref[0, rs, :] = x1[i] + gate2 * _dot(z[i], w2_ref[...])


def _tail(x, ada3, ya, yb, gt, wa, wb, wo, gain2, w1, w3, w2):
    batch, seq, _ = x.shape
    tm = TAIL_ROWS
    d_ff = w1.shape[1]
    const = lambda b, j: (0, 0)
    rows = lambda width: pl.BlockSpec((1, tm, width), lambda b, j: (b, j, 0))
    weight = lambda shape: pl.BlockSpec(shape, const, pipeline_mode=pl.Buffered(1))
    return pl.pallas_call(
        _tail_kernel,
        grid=(batch, seq // tm),
        in_specs=[
            rows(D_MODEL),
            pl.BlockSpec((1, 1, 6 * D_MODEL), lambda b, j: (b, 0, 0)),
            rows(RWKV_WIDTH),
            rows(ATTN_Q_WIDTH),
            rows(GATE_WIDTH),
            weight((RWKV_WIDTH, D_MODEL)),
            weight((ATTN_Q_WIDTH, D_MODEL)),
            weight((D_MODEL, D_MODEL)),
            pl.BlockSpec((1, D_MODEL), const),
            weight((D_MODEL, d_ff)),
            weight((D_MODEL, d_ff)),
            weight((d_ff, D_MODEL)),
        ],
        out_specs=rows(D_MODEL),
        out_shape=jax.ShapeDtypeStruct((batch, seq, D_MODEL), F32),
        compiler_params=pltpu.CompilerParams(
            dimension_semantics=("parallel", "parallel"),
            vmem_limit_bytes=V7X_VMEM_LIMIT_BYTES),
        name="tail",
    )(x, ada3, ya, yb, gt, wa, wb, wo, gain2, w1, w3, w2)


def kernel(x, c, positions, ada_w, ada_b, norm1_gain, norm2_gain, w_in, tshift_mu, decay_w0,
           decay_up, iclr_a0, iclr_up, gate_up, k_k, k_a, r_k, lnx_gain, lnx_bias, q_norm_gain,
           k_norm_gain, attn_sinks, branch_gate_b, w_branch_a, w_branch_b, w_out, ffn_w1, ffn_w3,
           ffn_w2):
    depth = ada_w.shape[0]
    batch = x.shape[0]
    cos_tab, sin_tab = _rope_tables(positions)
    for l in range(depth):
        ada3 = _ada(c, ada_w[l], ada_b[l]).reshape(batch, 1, 6 * D_MODEL)
        pack_a, pack_b, w_pack, qkv, gt = _inproj(
            x, ada3, norm1_gain[l].reshape(1, D_MODEL), w_in[l].astype(BF16),
            tshift_mu[l].reshape(1, RWKV_SHIFT_WIDTH), branch_gate_b[l].reshape(1, GATE_WIDTH),
            decay_w0[l], decay_up[l], iclr_a0[l], iclr_up[l], gate_up[l], k_k[l], k_a[l], r_k[l])
        ya = _wkv(pack_a, pack_b, w_pack, lnx_gain[l], lnx_bias[l])
        yb = _attn(qkv, cos_tab, sin_tab, q_norm_gain[l], k_norm_gain[l], attn_sinks[l])
        x = _tail(x, ada3, ya, yb, gt, w_branch_a[l].astype(BF16), w_branch_b[l].astype(BF16),
                  w_out[l].astype(BF16), norm2_gain[l].reshape(1, D_MODEL),
                  ffn_w1[l].astype(BF16), ffn_w3[l].astype(BF16), ffn_w2[l].astype(BF16))
    return x
```

```python
import math

import jax
import jax.numpy as jnp
from jax import lax
from jax.experimental import pallas as pl
from jax.experimental.pallas import tpu as pltpu

F32 = jnp.float32
BF16 = jnp.bfloat16

D_MODEL = 1024
HEAD_DIM = 64
RWKV_HEADS = 8
RWKV_WIDTH = RWKV_HEADS * HEAD_DIM
DECAY_LORA = 64
ICLR_LORA = 64
GATE_LORA = 128
ATTN_Q_HEADS = 8
ATTN_KV_HEADS = 2
ATTN_GROUPS = ATTN_Q_HEADS // ATTN_KV_HEADS
ATTN_Q_WIDTH = ATTN_Q_HEADS * HEAD_DIM
ATTN_KV_WIDTH = ATTN_KV_HEADS * HEAD_DIM
WINDOW = 128
BLOCK = 128
ROPE_THETA = 500000.0
ROPE_DIM = HEAD_DIM // 4
ROPE_HALF = ROPE_DIM // 2
RMS_EPS = 1e-6
GN_EPS = 64e-5
NEG_INF = -1e30
LOG2_E = math.log2(math.e)
RWKV_SHIFT_WIDTH = 3 * RWKV_WIDTH + DECAY_LORA + ICLR_LORA + GATE_LORA
QKV_WIDTH = ATTN_Q_WIDTH + 2 * ATTN_KV_WIDTH
GATE_WIDTH = 2 * D_MODEL
WKV_PACK_A = 4 * RWKV_WIDTH
WKV_PACK_B = 5 * RWKV_WIDTH

V7X_LANES = 128
V7X_SUBLANES = 8
V7X_VMEM_LIMIT_BYTES = 56 * 1024 * 1024

INPROJ_ROWS = 512
INPROJ_SUB_ROWS = 256
WKV_CHUNK = 64
WKV_GROUP_ROWS = 256
WKV_ROWS = 512
WKV_STAGE_SKEW = 4
ATTN_ROWS = 1024
TAIL_ROWS = 512
TAIL_SUB_ROWS = 256


def _dot(a, b):
    return jnp.dot(a, b, preferred_element_type=F32)


def _dot_nt(a, b):
    return lax.dot_general(a, b, (((1,), (1,)), ((), ())), preferred_element_type=F32)


def _dot_tn(a, b):
    return lax.dot_general(a, b, (((0,), (0,)), ((), ())), preferred_element_type=F32)


def _head_mean_matrix(width):
    head = jnp.arange(width) // HEAD_DIM
    return jnp.where(head[:, None] == head[None, :], 1.0 / HEAD_DIM, 0.0).astype(BF16)


def _mm(a, b):
    return _dot(a.astype(BF16), b.astype(BF16))


def _mm_nt(a, b):
    return _dot_nt(a.astype(BF16), b.astype(BF16))


def _ada_kernel(c_ref, w_ref, b_ref, o_ref):
    o_ref[...] = jnp.dot(c_ref[...], w_ref[...], precision=lax.Precision.HIGHEST,
                         preferred_element_type=F32) + b_ref[...]


def _ada(c, ada_w, ada_b):
    batch = c.shape[0]
    n_out = ada_w.shape[1]
    return pl.pallas_call(
        _ada_kernel,
        grid=(n_out // D_MODEL,),
        in_specs=[
            pl.BlockSpec((batch, D_MODEL), lambda j: (0, 0)),
            pl.BlockSpec((D_MODEL, D_MODEL), lambda j: (0, j)),
            pl.BlockSpec((1, D_MODEL), lambda j: (0, j)),
        ],
        out_specs=pl.BlockSpec((batch, D_MODEL), lambda j: (0, j)),
        out_shape=jax.ShapeDtypeStruct((batch, n_out), F32),
        name="ada",
    )(c, ada_w, ada_b.reshape(1, n_out))


def _rope_kernel(pos_ref, freq_ref, sgn_ref, cos_ref, sin_ref):
    per_row = V7X_LANES // ROPE_DIM
    dense_rows = pos_ref.shape[1]
    ang = pos_ref[0].astype(F32) * freq_ref[...]
    cos_d = jnp.cos(ang)
    sin_d = jnp.sin(ang) * sgn_ref[...]
    lane = lax.broadcasted_iota(jnp.int32, ang.shape, 1)
    rotary0 = lane < ROPE_DIM
    rotary1 = (lane >= HEAD_DIM) & (lane < HEAD_DIM + ROPE_DIM)
    for i in range(per_row):
        shift = (V7X_LANES - ROPE_DIM * i) % V7X_LANES
        for dense, fill, out_ref in ((cos_d, 1.0, cos_ref), (sin_d, 0.0, sin_ref)):
            head0 = pltpu.roll(dense, shift, axis=1) if shift else dense
            head1 = pltpu.roll(head0, HEAD_DIM, axis=1)
            row = jnp.where(rotary0, head0, jnp.where(rotary1, head1, fill))
            out_ref[0, pl.ds(i, dense_rows, stride=per_row), :] = row


def _rope_tables(positions):
    batch, seq = positions.shape
    per_row = V7X_LANES // ROPE_DIM
    inv_freq = ROPE_THETA ** (-jnp.arange(ROPE_HALF, dtype=F32) / ROPE_HALF)
    dim = jnp.arange(V7X_LANES) % ROPE_DIM
    freq = inv_freq[dim % ROPE_HALF].reshape(1, V7X_LANES)
    sgn = jnp.where(dim < ROPE_HALF, -1.0, 1.0).astype(F32).reshape(1, V7X_LANES)
    pos = jnp.repeat(positions.reshape(batch, seq // per_row, per_row), ROPE_DIM, axis=-1)
    vec_spec = pl.BlockSpec((1, V7X_LANES), lambda b: (0, 0))
    tab_spec = pl.BlockSpec((1, seq, V7X_LANES), lambda b: (b, 0, 0))
    tab = jax.ShapeDtypeStruct((batch, seq, V7X_LANES), F32)
    return pl.pallas_call(
        _rope_kernel,
        grid=(batch,),
        in_specs=[pl.BlockSpec((1, seq // per_row, V7X_LANES), lambda b: (b, 0, 0)),
                  vec_spec, vec_spec],
        out_specs=[tab_spec, tab_spec],
        out_shape=[tab, tab],
        name="rope",
    )(pos, freq, sgn)


def _inproj_kernel(x_ref, ada_ref, gain_ref, w_ref, mu_ref, gb_ref, w0_ref, dup_ref, a0_ref,
                   aup_ref, gup_ref, kk_ref, ka_ref, rk_ref,
                   pa_ref, pb_ref, wl_ref, qkv_ref, gt_ref, carry_ref):
    rows = INPROJ_SUB_ROWS
    w_rows = rows // WKV_CHUNK * V7X_SUBLANES
    ada = ada_ref[0]
    shift1 = ada[:, 0:D_MODEL]
    mod1 = gain_ref[...] * (1.0 + ada[:, D_MODEL:2 * D_MODEL])

    @pl.when(pl.program_id(1) == 0)
    def _():
        carry_ref[...] = jnp.zeros_like(carry_ref)

    last = carry_ref[...]
    subs = [slice(i * rows, (i + 1) * rows) for i in range(x_ref.shape[1] // rows)]
    hs, cols = [], []
    for rs in subs:
        x = x_ref[0, rs, :]
        inv = lax.rsqrt(jnp.mean(x * x, axis=-1, keepdims=True) + RMS_EPS)
        h = ((x * inv) * mod1 + shift1).astype(BF16)
        p = _dot(h, w_ref[:, 0:RWKV_SHIFT_WIDTH])
        prev = pltpu.roll(p, 1, axis=0)
        row = lax.broadcasted_iota(jnp.int32, p.shape, 0)
        prev = jnp.where(row == 0, last, prev)
        last = p[rows - 1:rows, :]
        hs.append(h)
        cols.append(p + (prev - p) * mu_ref[...])
    carry_ref[...] = last
    for i, rs in enumerate(subs):
        pack_a, pack_b, w_pack = _wkv_prologue(cols[i], w0_ref, dup_ref, a0_ref, aup_ref,
                                                    gup_ref, kk_ref, ka_ref, rk_ref)
        pa_ref[0, rs, :] = pack_a
        pb_ref[0, rs, :] = pack_b
        wl_ref[0, i * w_rows:(i + 1) * w_rows, :] = w_pack
        qkv_ref[0, rs, :] = _dot(hs[i], w_ref[:, RWKV_SHIFT_WIDTH:RWKV_SHIFT_WIDTH + QKV_WIDTH]).astype(BF16)
        gl = _dot(hs[i], w_ref[:, RWKV_SHIFT_WIDTH + QKV_WIDTH:])
        gt_ref[0, rs, :] = (gl + gb_ref[...]).astype(BF16)


def _inproj(x, ada3, gain, w_in_bf, mu, gate_b, decay_w0, decay_up, iclr_a0, iclr_up, gate_up,
            k_k, k_a, r_k):
    batch, seq, _ = x.shape
    in_width = w_in_bf.shape[1]
    tm = INPROJ_ROWS
    W = RWKV_WIDTH
    const = lambda b, j: (0, 0)
    blk = lambda b, j: (b, j, 0)
    vec = pl.BlockSpec((1, W), const)
    w_rows = tm // WKV_CHUNK * V7X_SUBLANES
    return pl.pallas_call(
        _inproj_kernel,
        grid=(batch, seq // tm),
        in_specs=[
            pl.BlockSpec((1, tm, D_MODEL), blk),
            pl.BlockSpec((1, 1, 6 * D_MODEL), lambda b, j: (b, 0, 0)),
            pl.BlockSpec((1, D_MODEL), const),
            pl.BlockSpec((D_MODEL, in_width), const, pipeline_mode=pl.Buffered(1)),
            pl.BlockSpec((1, RWKV_SHIFT_WIDTH), const),
            pl.BlockSpec((1, GATE_WIDTH), const),
            vec,
            pl.BlockSpec((DECAY_LORA, W), const),
            vec,
            pl.BlockSpec((ICLR_LORA, W), const),
            pl.BlockSpec((GATE_LORA, W), const),
            vec, vec, vec,
        ],
        out_specs=[
            pl.BlockSpec((1, tm, WKV_PACK_A), blk),
            pl.BlockSpec((1, tm, WKV_PACK_B), blk),
            pl.BlockSpec((1, w_rows, W), blk),
            pl.BlockSpec((1, tm, QKV_WIDTH), blk),
            pl.BlockSpec((1, tm, GATE_WIDTH), blk),
        ],
        out_shape=[
            jax.ShapeDtypeStruct((batch, seq, WKV_PACK_A), BF16),
            jax.ShapeDtypeStruct((batch, seq, WKV_PACK_B), BF16),
            jax.ShapeDtypeStruct((batch, seq // WKV_CHUNK * V7X_SUBLANES, W), F32),
            jax.ShapeDtypeStruct((batch, seq, QKV_WIDTH), BF16),
            jax.ShapeDtypeStruct((batch, seq, GATE_WIDTH), BF16),
        ],
        scratch_shapes=[pltpu.VMEM((1, RWKV_SHIFT_WIDTH), F32)],
        compiler_params=pltpu.CompilerParams(
            dimension_semantics=("parallel", "arbitrary"),
            vmem_limit_bytes=V7X_VMEM_LIMIT_BYTES),
        name="inproj",
    )(x, ada3, gain, w_in_bf, mu, gate_b, decay_w0.reshape(1, W), decay_up.astype(BF16),
      iclr_a0.reshape(1, W), iclr_up.astype(BF16), gate_up.astype(BF16), k_k.reshape(1, W),
      k_a.reshape(1, W), r_k.reshape(1, W))


def _cumsum_rows(x):
    n = x.shape[0]
    row = lax.broadcasted_iota(jnp.int32, x.shape, 0)
    s = 1
    while s < min(n, V7X_SUBLANES):
        x = x + jnp.where(row >= s, pltpu.roll(x, s, axis=0), 0.0)
        s *= 2
    while s < n:
        x = jnp.concatenate([x[:s], x[s:] + x[:n - s]], axis=0)
        s *= 2
    return x


def _wkv_prologue(cols, w0_ref, dup_ref, a0_ref, aup_ref, gup_ref, kk_ref, ka_ref, rk_ref):
    L, W, HD = WKV_CHUNK, RWKV_WIDTH, HEAD_DIM
    n_chunks = cols.shape[0] // L
    r = cols[:, 0:W]
    k = cols[:, W:2 * W]
    v = cols[:, 2 * W:3 * W]
    o = 3 * W
    xw = cols[:, o:o + DECAY_LORA]
    xa = cols[:, o + DECAY_LORA:o + DECAY_LORA + ICLR_LORA]
    xg = cols[:, o + DECAY_LORA + ICLR_LORA:]

    lw = jax.nn.sigmoid(w0_ref[...] + _mm(jnp.tanh(xw), dup_ref[...])) * (-math.exp(-0.5) * LOG2_E)
    a = jax.nn.sigmoid(a0_ref[...] + _mm(xa, aup_ref[...]))
    g = _mm(jax.nn.sigmoid(xg), gup_ref[...])
    kkp = k * kk_ref[...]
    kk = kkp * jnp.minimum(lax.rsqrt(_head_mean(kkp * kkp) * float(HD)), 1e12)
    k_mod = k * (1.0 + (a - 1.0) * ka_ref[...])
    bonus = _head_mean(r * k_mod * rk_ref[...]) * float(HD) * v

    cum = jnp.concatenate([_cumsum_rows(lw[c * L:(c + 1) * L]) for c in range(n_chunks)], axis=0)
    e_in = jnp.exp2(cum)
    e_neg = jnp.exp2(-cum)
    w_last = [e_in[(c + 1) * L - 1:(c + 1) * L, :] for c in range(n_chunks)]
    w_rows = jnp.concatenate([jnp.broadcast_to(w, (L, W)) for w in w_last], axis=0)
    bt = kk * a * e_neg
    kt = k_mod * e_neg
    pack_a = jnp.concatenate([-kk * jnp.exp2(cum - lw), r * e_in, g, bonus], axis=1).astype(BF16)
    pack_b = jnp.concatenate([bt, kt, bt * w_rows, kt * w_rows, v], axis=1).astype(BF16)
    w_pack = jnp.concatenate([jnp.broadcast_to(w, (V7X_SUBLANES, W)) for w in w_last], axis=0)
    return pack_a, pack_b, w_pack


def _wkv_chunks(pack_a, pack_b, w_pack, state_ref):
    L, W, HD = WKV_CHUNK, RWKV_WIDTH, HEAD_DIM
    n_chunks = pack_a.shape[0] // L
    row2 = lax.broadcasted_iota(jnp.int32, (2 * L, 2 * L), 0)
    col2 = lax.broadcasted_iota(jnp.int32, (2 * L, 2 * L), 1) % L
    lower2 = col2 < jnp.where(row2 < L, row2, row2 - L + 1)
    lane3 = lax.broadcasted_iota(jnp.int32, (L, 3 * HD), 1)
    zeros_b = jnp.zeros((L, HD), BF16)

    units = [(c, h) for c in range(n_chunks) for h in range(RWKV_HEADS)]

    def pick(arr, c, h):
        return arr[c * L:(c + 1) * L, h * HD:(h + 1) * HD]

    at_b, rt_b = pack_a[:, 0:W], pack_a[:, W:2 * W]
    bt_b, kt_b = pack_b[:, 0:W], pack_b[:, W:2 * W]
    bth_b, kth_b = pack_b[:, 2 * W:3 * W], pack_b[:, 3 * W:4 * W]
    v_b = pack_b[:, 4 * W:5 * W]
    ar_b = {u: (pick(at_b, *u), pick(rt_b, *u)) for u in units}
    a_t = {u: ar_b[u][0].astype(F32) for u in units}
    r_t = {u: ar_b[u][1].astype(F32) for u in units}
    v_h = {u: pick(v_b, *u) for u in units}
    w_l = {(c, h): w_pack[c * V7X_SUBLANES:c * V7X_SUBLANES + 1, h * HD:(h + 1) * HD]
           for c, h in units}
    bk_t = {u: jnp.concatenate([pick(bt_b, *u), pick(kt_b, *u)], axis=0) for u in units}
    bk_hat = {u: jnp.concatenate([pick(bth_b, *u), pick(kth_b, *u)], axis=0) for u in units}

    sc = {u: jnp.where(lower2, _dot_nt(jnp.concatenate(ar_b[u], axis=0), bk_t[u]), 0.0)
          for u in units}
    yield None
    sc_b = {u: sc[u].astype(BF16) for u in units}
    top = {u: sc_b[u][0:L] for u in units}
    bot = {u: sc_b[u][L:2 * L] for u in units}
    vz = {u: jnp.concatenate([zeros_b, v_h[u]], axis=1) for u in units}
    zeros_w = jnp.zeros((L, 2 * HD), BF16)
    akv = {u: _dot(top[u], jnp.concatenate([zeros_w, vz[u]], axis=0)) for u in units}
    yield None

    zeros_f = jnp.zeros((L, HD), F32)
    wx = {u: jnp.concatenate([akv[u] + jnp.concatenate([a_t[u], zeros_f], axis=1),
                              sc[u][0:L, 0:HD]], axis=1) for u in units}
    levels = L.bit_length() - 1
    for _ in range(levels):
        wx_b = {u: wx[u].astype(BF16) for u in units}
        wx = {u: _dot(wx_b[u][:, 2 * HD:3 * HD], wx_b[u]) + jnp.where(lane3 < 2 * HD, wx[u], 0.0)
              for u in units}
        yield None
    x2 = {u: jnp.concatenate([wx[u][:, 0:2 * HD].astype(BF16), vz[u]], axis=0) for u in units}
    ry = {u: _dot(bot[u], x2[u]) for u in units}
    yield None
    gs = {u: _dot_tn(x2[u], bk_hat[u]) for u in units}
    yield None

    y_rows = []
    for c in range(n_chunks):
        y_heads = []
        for h in range(RWKV_HEADS):
            u = (c, h)
            s0 = state_ref[h]
            s0_b = s0.astype(BF16)
            y_heads.append(_dot_nt((r_t[u] + ry[u][:, 0:HD]).astype(BF16), s0_b) + ry[u][:, HD:2 * HD])
            state_ref[h] = s0 * w_l[u] + _dot(s0_b, gs[u][0:HD].astype(BF16)) + gs[u][HD:2 * HD]
        y_rows.append(jnp.concatenate(y_heads, axis=1))
    yield jnp.concatenate(y_rows, axis=0)


def _head_mean(x):
    slabs = []
    for s in range(x.shape[1] // V7X_LANES):
        xs = x[:, s * V7X_LANES:(s + 1) * V7X_LANES]
        lo = lax.broadcasted_iota(jnp.int32, xs.shape, 1) < HEAD_DIM
        lo_sum = jnp.sum(jnp.where(lo, xs, 0.0), axis=-1, keepdims=True)
        hi_sum = jnp.sum(jnp.where(lo, 0.0, xs), axis=-1, keepdims=True)
        slabs.append(jnp.where(lo, lo_sum, hi_sum) * (1.0 / HEAD_DIM))
    return jnp.concatenate(slabs, axis=1)


def _wkv_epilogue(y, pack_a, lng_ref, lnb_ref, head_mean):
    W = RWKV_WIDTH
    g, bonus = pack_a[:, 2 * W:3 * W].astype(F32), pack_a[:, 3 * W:4 * W].astype(F32)
    yc = y - head_mean(y)
    yn = yc * lax.rsqrt(head_mean(yc * yc) + GN_EPS) * lng_ref[...] + lnb_ref[...]
    return ((yn + bonus) * g).astype(BF16)


def _wkv_kernel(pa_ref, pb_ref, w_ref, lng_ref, lnb_ref, hsum_ref, y_ref, state_ref):
    rows = WKV_GROUP_ROWS
    w_rows = rows // WKV_CHUNK * V7X_SUBLANES
    n_groups = pa_ref.shape[1] // rows

    @pl.when(pl.program_id(1) == 0)
    def _():
        state_ref[...] = jnp.zeros_like(state_ref)

    hsum = hsum_ref[...]
    scans = [_wkv_chunks(pa_ref[0, gi * rows:(gi + 1) * rows, :],
                         pb_ref[0, gi * rows:(gi + 1) * rows, :],
                         w_ref[0, gi * w_rows:(gi + 1) * w_rows, :], state_ref) for gi in range(n_groups)]
    n_stages = WKV_CHUNK.bit_length() - 1 + 5
    ys = [None] * n_groups
    for t in range(n_stages + WKV_STAGE_SKEW * (n_groups - 1)):
        for gi in range(n_groups):
            stage = t - WKV_STAGE_SKEW * gi
            if 0 <= stage < n_stages:
                ys[gi] = next(scans[gi])
    for gi in range(n_groups):
        mean = (lambda t: _mm(t, hsum)) if gi == n_groups - 1 else _head_mean
        y_ref[0, gi * rows:(gi + 1) * rows, :] = _wkv_epilogue(
            ys[gi], pa_ref[0, gi * rows:(gi + 1) * rows, :], lng_ref, lnb_ref, mean)


def _wkv(pack_a, pack_b, w_pack, lnx_gain, lnx_bias):
    batch, seq, _ = pack_a.shape
    rows = WKV_ROWS
    W = RWKV_WIDTH
    const = lambda b, j: (0, 0)
    blk = lambda b, j: (b, j, 0)
    vec = pl.BlockSpec((1, W), const)
    hsum = _head_mean_matrix(W)
    return pl.pallas_call(
        _wkv_kernel,
        grid=(batch, seq // rows),
        in_specs=[
            pl.BlockSpec((1, rows, WKV_PACK_A), blk),
            pl.BlockSpec((1, rows, WKV_PACK_B), blk),
            pl.BlockSpec((1, rows // WKV_CHUNK * V7X_SUBLANES, W), blk),
            vec, vec,
            pl.BlockSpec((W, W), const),
        ],
        out_specs=pl.BlockSpec((1, rows, W), blk),
        out_shape=jax.ShapeDtypeStruct((batch, seq, W), BF16),
        scratch_shapes=[pltpu.VMEM((RWKV_HEADS, HEAD_DIM, HEAD_DIM), F32)],
        compiler_params=pltpu.CompilerParams(
            dimension_semantics=("parallel", "arbitrary")),
        name="wkv",
    )(pack_a, pack_b, w_pack, lnx_gain.reshape(1, W), lnx_bias.reshape(1, W), hsum)


def _attn_kernel(sink_ref, q_ref, kv_ref, cos_ref, sin_ref, bias_ref, qg_ref, kg_ref, hmean_ref,
                 perm_ref, o_ref, kprev_ref, vprev_ref):
    n_blk = q_ref.shape[1] // BLOCK
    n_slab = ATTN_Q_WIDTH // V7X_LANES
    slab_per_kv = n_slab // ATTN_KV_HEADS
    kvs = range(ATTN_KV_HEADS)
    pars = range(2 * ATTN_KV_HEADS)

    @pl.when(pl.program_id(1) == 0)
    def _():
        kprev_ref[...] = jnp.zeros_like(kprev_ref)
        vprev_ref[...] = jnp.zeros_like(vprev_ref)

    cos, sin = cos_ref[0], sin_ref[0]
    q_gain = qg_ref[...] * (HEAD_DIM ** -0.5 * LOG2_E)
    q_all = q_ref[0].astype(F32)
    kv = kv_ref[0].astype(F32)
    slabs = [q_all[:, s * V7X_LANES:(s + 1) * V7X_LANES] for s in range(n_slab)]
    slabs.append(kv[:, 0:ATTN_KV_WIDTH])
    gains = [q_gain] * n_slab + [kg_ref[...]]
    n_rows = n_blk * BLOCK

    ms_all = _mm(jnp.concatenate([x * x for x in slabs], axis=0), hmean_ref[...])
    xn = [x * lax.rsqrt(ms_all[i * n_rows:(i + 1) * n_rows] + RMS_EPS) * gains[i]
          for i, x in enumerate(slabs)]
    partner = _mm(jnp.concatenate(xn, axis=0), perm_ref[0])
    normed = [xn[i] * cos + partner[i * n_rows:(i + 1) * n_rows] * sin for i in range(len(slabs))]
    qn, k_cur = normed[:n_slab], normed[n_slab]
    v_cur = kv[:, ATTN_KV_WIDTH:]
    swapped = _mm(jnp.concatenate([k_cur, v_cur], axis=0), perm_ref[1])
    k_swap, v_swap = swapped[0:n_rows], swapped[n_rows:]
    lo = lax.broadcasted_iota(jnp.int32, (n_rows, V7X_LANES), 1) < HEAD_DIM
    kdup_cur = [jnp.where(lo, k_cur, k_swap).astype(BF16), jnp.where(lo, k_swap, k_cur).astype(BF16)]
    vpar_cur = [jnp.where(lo, v_cur, 0.0).astype(BF16), jnp.where(lo, 0.0, v_swap).astype(BF16),
                jnp.where(lo, v_swap, 0.0).astype(BF16), jnp.where(lo, 0.0, v_cur).astype(BF16)]

    def band(prev_ref, cur, j, i):
        if i == 0:
            return jnp.concatenate([prev_ref[j], cur[j][0:BLOCK]], axis=0)
        return cur[j][(i - 1) * BLOCK:(i + 1) * BLOCK]

    units = [(i, hk) for i in range(n_blk) for hk in kvs]
    kband = {(i, hk): band(kprev_ref, kdup_cur, hk, i) for i, hk in units}
    ones_b = jnp.ones((2 * BLOCK, V7X_LANES), BF16)
    vaug = {(i, j): jnp.concatenate([band(vprev_ref, vpar_cur, j, i), ones_b], axis=1)
            for i in range(n_blk) for j in pars}
    for hk in kvs:
        kprev_ref[hk] = kdup_cur[hk][n_rows - BLOCK:]
    for j in pars:
        vprev_ref[j] = vpar_cur[j][n_rows - BLOCK:]

    lo1 = lax.broadcasted_iota(jnp.int32, (BLOCK, V7X_LANES), 1) < HEAD_DIM
    stack = 2 * slab_per_kv
    first_bias = bias_ref[jnp.minimum(pl.program_id(1), 1)]
    bias = [jnp.concatenate([first_bias if i == 0 else bias_ref[1]] * stack, axis=0)
            for i in range(n_blk)]
    heads = [[2 * (hk * slab_per_kv + j) + p for p in range(2) for j in range(slab_per_kv)]
             for hk in kvs]
    lhs = {}
    for i, hk in units:
        mine = [qn[hk * slab_per_kv + j][i * BLOCK:(i + 1) * BLOCK] for j in range(slab_per_kv)]
        lhs[(i, hk)] = jnp.concatenate([jnp.where(lo1, x, 0.0) for x in mine]
                                       + [jnp.where(lo1, 0.0, x) for x in mine], axis=0).astype(BF16)
    s = {u: _mm_nt(lhs[u], kband[u]) + bias[u[0]] for u in units}
    rmax = {u: jnp.max(s[u], axis=-1, keepdims=True) for u in units}
    sink2 = [sink_ref[h] * LOG2_E for h in range(ATTN_Q_HEADS)]
    m = {(u, t): jnp.maximum(rmax[u][t * BLOCK:(t + 1) * BLOCK], sink2[heads[u[1]][t]])
         for u in units for t in range(stack)}
    e = {u: jnp.concatenate([jnp.exp2(s[u][t * BLOCK:(t + 1) * BLOCK] - m[(u, t)])
                             for t in range(stack)], axis=0).astype(BF16) for u in units}
    half = stack * BLOCK // 2
    pv = {(i, hk, p): _mm(e[(i, hk)][p * half:(p + 1) * half], vaug[(i, 2 * hk + p)])
          for i, hk in units for p in range(2)}
    extra = {(u, t): jnp.exp2(sink2[heads[u[1]][t]] - m[(u, t)]) for u in units for t in range(stack)}
    for i, hk in units:
        for j in range(slab_per_kv):
            r0 = slice(j * BLOCK, (j + 1) * BLOCK)
            even, odd = pv[(i, hk, 0)][r0], pv[(i, hk, 1)][r0]
            num = even[:, 0:V7X_LANES] + odd[:, 0:V7X_LANES]
            den = jnp.where(lo1, even[:, V7X_LANES:] + extra[((i, hk), j)],
                            odd[:, V7X_LANES:] + extra[((i, hk), slab_per_kv + j)])
            slab = hk * slab_per_kv + j
            o_ref[0, i * BLOCK:(i + 1) * BLOCK, slab * V7X_LANES:(slab + 1) * V7X_LANES] = (
                num * (1.0 / den)).astype(BF16)


def _attn(qkv, cos_tab, sin_tab, q_gain, k_gain, sinks):
    batch, seq, _ = qkv.shape
    rows = ATTN_ROWS
    cur = lambda b, n: (b, n, 0)
    const = lambda b, n: (0, 0)
    const3 = lambda b, n: (0, 0, 0)
    kv_blk = ATTN_Q_WIDTH // (2 * ATTN_KV_WIDTH)
    gain2 = lambda gn: jnp.tile(gn.reshape(1, HEAD_DIM), (1, V7X_LANES // HEAD_DIM))
    lane = jnp.arange(V7X_LANES)
    hmean = _head_mean_matrix(V7X_LANES)
    dim = lane % HEAD_DIM
    src = jnp.where(dim < ROPE_HALF, lane + ROPE_HALF, jnp.where(dim < ROPE_DIM, lane - ROPE_HALF, -1))
    partner_p = lane[:, None] == src[None, :]
    swap_p = lane[:, None] == ((lane + HEAD_DIM) % V7X_LANES)[None, :]
    perm = jnp.stack([partner_p, swap_p]).astype(BF16)
    dist = jnp.arange(BLOCK)[:, None] + BLOCK - jnp.arange(2 * BLOCK)[None, :]
    in_band = (dist >= 0) & (dist < WINDOW)
    own = (jnp.arange(2 * BLOCK) >= BLOCK)[None, :]
    bias = jnp.where(jnp.stack([in_band & own, in_band]), 0.0, NEG_INF).astype(F32)
    return pl.pallas_call(
        _attn_kernel,
        grid=(batch, seq // rows),
        in_specs=[
            pl.BlockSpec(memory_space=pltpu.SMEM),
            pl.BlockSpec((1, rows, ATTN_Q_WIDTH), cur),
            pl.BlockSpec((1, rows, 2 * ATTN_KV_WIDTH), lambda b, n: (b, n, kv_blk)),
            pl.BlockSpec((1, rows, V7X_LANES), cur),
            pl.BlockSpec((1, rows, V7X_LANES), cur),
            pl.BlockSpec((2, BLOCK, 2 * BLOCK), const3),
            pl.BlockSpec((1, V7X_LANES), const),
            pl.BlockSpec((1, V7X_LANES), const),
            pl.BlockSpec((V7X_LANES, V7X_LANES), const),
            pl.BlockSpec((2, V7X_LANES, V7X_LANES), const3),
        ],
        out_specs=pl.BlockSpec((1, rows, ATTN_Q_WIDTH), cur),
        out_shape=jax.ShapeDtypeStruct((batch, seq, ATTN_Q_WIDTH), BF16),
        scratch_shapes=[
            pltpu.VMEM((ATTN_KV_HEADS, BLOCK, V7X_LANES), BF16),
            pltpu.VMEM((2 * ATTN_KV_HEADS, BLOCK, V7X_LANES), BF16),
        ],
        compiler_params=pltpu.CompilerParams(
            dimension_semantics=("parallel", "arbitrary")),
        name="attn",
    )(sinks, qkv, qkv, cos_tab, sin_tab, bias, gain2(q_gain), gain2(k_gain), hmean, perm)


def _tail_kernel(x_ref, ada_ref, ya_ref, yb_ref, gt_ref, wa_ref, wb_ref, wo_ref, gain_ref,
                 w1_ref, w3_ref, w2_ref, o_ref):
    ada = ada_ref[0]
    gate1 = ada[:, 2 * D_MODEL:3 * D_MODEL]
    shift2 = ada[:, 3 * D_MODEL:4 * D_MODEL]
    scale2 = ada[:, 4 * D_MODEL:5 * D_MODEL]
    gate2 = ada[:, 5 * D_MODEL:6 * D_MODEL]
    mod2 = gain_ref[...] * (1.0 + scale2)

    rows = TAIL_SUB_ROWS
    subs = [slice(i * rows, (i + 1) * rows) for i in range(x_ref.shape[1] // rows)]
    ma = [_dot(ya_ref[0, rs, :], wa_ref[...]) for rs in subs]
    mb = [_dot(yb_ref[0, rs, :], wb_ref[...]) for rs in subs]
    merged = []
    for i, rs in enumerate(subs):
        gates = jax.nn.sigmoid(gt_ref[0, rs, :].astype(F32))
        merged.append((gates[:, 0:D_MODEL] * ma[i] + gates[:, D_MODEL:] * mb[i]).astype(BF16))
    x1 = [x_ref[0, rs, :] + gate1 * _dot(merged[i], wo_ref[...]) for i, rs in enumerate(subs)]
    h2 = []
    for x1_i in x1:
        inv = lax.rsqrt(jnp.mean(x1_i * x1_i, axis=-1, keepdims=True) + RMS_EPS)
        h2.append(((x1_i * inv) * mod2 + shift2).astype(BF16))
    a1 = [_dot(h, w1_ref[...]) for h in h2]
    a3 = [_dot(h, w3_ref[...]) for h in h2]
    z = [(jax.nn.silu(a1[i]) * a3[i]).astype(BF16) for i in range(len(subs))]
    for i, rs in enumerate(subs):
        o_ref[0, rs, :] = x1[i] + gate2 * _dot(z[i], w2_ref[...])


def _tail(x, ada3, ya, yb, gt, wa, wb, wo, gain2, w1, w3, w2):
    batch, seq, _ = x.shape
    tm = TAIL_ROWS
    d_ff = w1.shape[1]
    const = lambda b, j: (0, 0)
    rows = lambda width: pl.BlockSpec((1, tm, width), lambda b, j: (b, j, 0))
    weight = lambda shape: pl.BlockSpec(shape, const, pipeline_mode=pl.Buffered(1))
    return pl.pallas_call(
        _tail_kernel,
        grid=(batch, seq // tm),
        in_specs=[
            rows(D_MODEL),
            pl.BlockSpec((1, 1, 6 * D_MODEL), lambda b, j: (b, 0, 0)),
            rows(RWKV_WIDTH),
            rows(ATTN_Q_WIDTH),
            rows(GATE_WIDTH),
            weight((RWKV_WIDTH, D_MODEL)),
            weight((ATTN_Q_WIDTH, D_MODEL)),
            weight((D_MODEL, D_MODEL)),
            pl.BlockSpec((1, D_MODEL), const),
            weight((D_MODEL, d_ff)),
            weight((D_MODEL, d_ff)),
            weight((d_ff, D_MODEL)),
        ],
        out_specs=rows(D_MODEL),
        out_shape=jax.ShapeDtypeStruct((batch, seq, D_MODEL), F32),
        compiler_params=pltpu.CompilerParams(
            dimension_semantics=("parallel", "parallel"),
            vmem_limit_bytes=V7X_VMEM_LIMIT_BYTES),
        name="tail",
    )(x, ada3, ya, yb, gt, wa, wb, wo, gain2, w1, w3, w2)


def kernel(x, c, positions, ada_w, ada_b, norm1_gain, norm2_gain, w_in, tshift_mu, decay_w0,
           decay_up, iclr_a0, iclr_up, gate_up, k_k, k_a, r_k, lnx_gain, lnx_bias, q_norm_gain,
           k_norm_gain, attn_sinks, branch_gate_b, w_branch_a, w_branch_b, w_out, ffn_w1, ffn_w3,
           ffn_w2):
    depth = ada_w.shape[0]
    batch = x.shape[0]
    cos_tab, sin_tab = _rope_tables(positions)
    for l in range(depth):
        ada3 = _ada(c, ada_w[l], ada_b[l]).reshape(batch, 1, 6 * D_MODEL)
        pack_a, pack_b, w_pack, qkv, gt = _inproj(
            x, ada3, norm1_gain[l].reshape(1, D_MODEL), w_in[l].astype(BF16),
            tshift_mu[l].reshape(1, RWKV_SHIFT_WIDTH), branch_gate_b[l].reshape(1, GATE_WIDTH),
            decay_w0[l], decay_up[l], iclr_a0[l], iclr_up[l], gate_up[l], k_k[l], k_a[l], r_k[l])
        ya = _wkv(pack_a, pack_b, w_pack, lnx_gain[l], lnx_bias[l])
        yb = _attn(qkv, cos_tab, sin_tab, q_norm_gain[l], k_norm_gain[l], attn_sinks[l])
        x = _tail(x, ada3, ya, yb, gt, w_branch_a[l].astype(BF16), w_branch_b[l].astype(BF16),
                  w_out[l].astype(BF16), norm2_gain[l].reshape(1, D_MODEL),
                  ffn_w1[l].astype(BF16), ffn_w3[l].astype(BF16), ffn_w2[l].astype(BF16))
    return x
```

```python
import math

import jax
import jax.numpy as jnp
from jax import lax
from jax.experimental import pallas as pl
from jax.experimental.pallas import tpu as pltpu

F32 = jnp.float32
BF16 = jnp.bfloat16

D_MODEL = 1024
HEAD_DIM = 64
RWKV_HEADS = 8
RWKV_WIDTH = RWKV_HEADS * HEAD_DIM
DECAY_LORA = 64
ICLR_LORA = 64
GATE_LORA = 128
ATTN_Q_HEADS = 8
ATTN_KV_HEADS = 2
ATTN_GROUPS = ATTN_Q_HEADS // ATTN_KV_HEADS
ATTN_Q_WIDTH = ATTN_Q_HEADS * HEAD_DIM
ATTN_KV_WIDTH = ATTN_KV_HEADS * HEAD_DIM
WINDOW = 128
BLOCK = 128
ROPE_THETA = 500000.0
ROPE_DIM = HEAD_DIM // 4
ROPE_HALF = ROPE_DIM // 2
RMS_EPS = 1e-6
GN_EPS = 64e-5
NEG_INF = -1e30
LOG2_E = math.log2(math.e)
RWKV_SHIFT_WIDTH = 3 * RWKV_WIDTH + DECAY_LORA + ICLR_LORA + GATE_LORA
QKV_WIDTH = ATTN_Q_WIDTH + 2 * ATTN_KV_WIDTH
GATE_WIDTH = 2 * D_MODEL
WKV_PACK_A = 4 * RWKV_WIDTH
WKV_PACK_B = 5 * RWKV_WIDTH

V7X_LANES = 128
V7X_SUBLANES = 8
V7X_VMEM_LIMIT_BYTES = 56 * 1024 * 1024

INPROJ_ROWS = 512
INPROJ_SUB_ROWS = 256
WKV_CHUNK = 64
WKV_GROUP_ROWS = 256
WKV_ROWS = 512
WKV_STAGE_SKEW = 4
ATTN_ROWS = 1024
TAIL_ROWS = 512
TAIL_SUB_ROWS = 256


def _dot(a, b):
    return jnp.dot(a, b, preferred_element_type=F32)


def _dot_nt(a, b):
    return lax.dot_general(a, b, (((1,), (1,)), ((), ())), preferred_element_type=F32)


def _dot_tn(a, b):
    return lax.dot_general(a, b, (((0,), (0,)), ((), ())), preferred_element_type=F32)


def _head_mean_matrix(width):
    head = jnp.arange(width) // HEAD_DIM
    return jnp.where(head[:, None] == head[None, :], 1.0 / HEAD_DIM, 0.0).astype(BF16)


def _mm(a, b):
    return _dot(a.astype(BF16), b.astype(BF16))


def _mm_nt(a, b):
    return _dot_nt(a.astype(BF16), b.astype(BF16))


def _ada_kernel(c_ref, w_ref, b_ref, o_ref):
    o_ref[...] = jnp.dot(c_ref[...], w_ref[...], precision=lax.Precision.HIGHEST,
                         preferred_element_type=F32) + b_ref[...]


def _ada(c, ada_w, ada_b):
    batch = c.shape[0]
    n_out = ada_w.shape[1]
    return pl.pallas_call(
        _ada_kernel,
        grid=(n_out // D_MODEL,),
        in_specs=[
            pl.BlockSpec((batch, D_MODEL), lambda j: (0, 0)),
            pl.BlockSpec((D_MODEL, D_MODEL), lambda j: (0, j)),
            pl.BlockSpec((1, D_MODEL), lambda j: (0, j)),
        ],
        out_specs=pl.BlockSpec((batch, D_MODEL), lambda j: (0, j)),
        out_shape=jax.ShapeDtypeStruct((batch, n_out), F32),
        name="ada",
    )(c, ada_w, ada_b.reshape(1, n_out))


def _rope_kernel(pos_ref, freq_ref, sgn_ref, cos_ref, sin_ref):
    per_row = V7X_LANES // ROPE_DIM
    dense_rows = pos_ref.shape[1]
    ang = pos_ref[0].astype(F32) * freq_ref[...]
    cos_d = jnp.cos(ang)
    sin_d = jnp.sin(ang) * sgn_ref[...]
    lane = lax.broadcasted_iota(jnp.int32, ang.shape, 1)
    rotary0 = lane < ROPE_DIM
    rotary1 = (lane >= HEAD_DIM) & (lane < HEAD_DIM + ROPE_DIM)
    for i in range(per_row):
        shift = (V7X_LANES - ROPE_DIM * i) % V7X_LANES
        for dense, fill, out_ref in ((cos_d, 1.0, cos_ref), (sin_d, 0.0, sin_ref)):
            head0 = pltpu.roll(dense, shift, axis=1) if shift else dense
            head1 = pltpu.roll(head0, HEAD_DIM, axis=1)
            row = jnp.where(rotary0, head0, jnp.where(rotary1, head1, fill))
            out_ref[0, pl.ds(i, dense_rows, stride=per_row), :] = row


def _rope_tables(positions):
    batch, seq = positions.shape
    per_row = V7X_LANES // ROPE_DIM
    inv_freq = ROPE_THETA ** (-jnp.arange(ROPE_HALF, dtype=F32) / ROPE_HALF)
    dim = jnp.arange(V7X_LANES) % ROPE_DIM
    freq = inv_freq[dim % ROPE_HALF].reshape(1, V7X_LANES)
    sgn = jnp.where(dim < ROPE_HALF, -1.0, 1.0).astype(F32).reshape(1, V7X_LANES)
    pos = jnp.repeat(positions.reshape(batch, seq // per_row, per_row), ROPE_DIM, axis=-1)
    vec_spec = pl.BlockSpec((1, V7X_LANES), lambda b: (0, 0))
    tab_spec = pl.BlockSpec((1, seq, V7X_LANES), lambda b: (b, 0, 0))
    tab = jax.ShapeDtypeStruct((batch, seq, V7X_LANES), F32)
    return pl.pallas_call(
        _rope_kernel,
        grid=(batch,),
        in_specs=[pl.BlockSpec((1, seq // per_row, V7X_LANES), lambda b: (b, 0, 0)),
                  vec_spec, vec_spec],
        out_specs=[tab_spec, tab_spec],
        out_shape=[tab, tab],
        name="rope",
    )(pos, freq, sgn)


def _inproj_kernel(x_ref, ada_ref, gain_ref, w_ref, mu_ref, gb_ref, w0_ref, dup_ref, a0_ref,
                   aup_ref, gup_ref, kk_ref, ka_ref, rk_ref,
                   pa_ref, pb_ref, wl_ref, qkv_ref, gt_ref, carry_ref):
    rows = INPROJ_SUB_ROWS
    w_rows = rows // WKV_CHUNK * V7X_SUBLANES
    ada = ada_ref[0]
    shift1 = ada[:, 0:D_MODEL]
    mod1 = gain_ref[...] * (1.0 + ada[:, D_MODEL:2 * D_MODEL])

    @pl.when(pl.program_id(1) == 0)
    def _():
        carry_ref[...] = jnp.zeros_like(carry_ref)

    last = carry_ref[...]
    subs = [slice(i * rows, (i + 1) * rows) for i in range(x_ref.shape[1] // rows)]
    hs, cols = [], []
    for rs in subs:
        x = x_ref[0, rs, :]
        inv = lax.rsqrt(jnp.mean(x * x, axis=-1, keepdims=True) + RMS_EPS)
        h = ((x * inv) * mod1 + shift1).astype(BF16)
        p = _dot(h, w_ref[:, 0:RWKV_SHIFT_WIDTH])
        prev = pltpu.roll(p, 1, axis=0)
        row = lax.broadcasted_iota(jnp.int32, p.shape, 0)
        prev = jnp.where(row == 0, last, prev)
        last = p[rows - 1:rows, :]
        hs.append(h)
        cols.append(p + (prev - p) * mu_ref[...])
    carry_ref[...] = last
    for i, rs in enumerate(subs):
        pack_a, pack_b, w_pack = _wkv_prologue(cols[i], w0_ref, dup_ref, a0_ref, aup_ref,
                                                    gup_ref, kk_ref, ka_ref, rk_ref)
        pa_ref[0, rs, :] = pack_a
        pb_ref[0, rs, :] = pack_b
        wl_ref[0, i * w_rows:(i + 1) * w_rows, :] = w_pack
        qkv_ref[0, rs, :] = _dot(hs[i], w_ref[:, RWKV_SHIFT_WIDTH:RWKV_SHIFT_WIDTH + QKV_WIDTH]).astype(BF16)
        gl = _dot(hs[i], w_ref[:, RWKV_SHIFT_WIDTH + QKV_WIDTH:])
        gt_ref[0, rs, :] = (gl + gb_ref[...]).astype(BF16)


def _inproj(x, ada3, gain, w_in_bf, mu, gate_b, decay_w0, decay_up, iclr_a0, iclr_up, gate_up,
            k_k, k_a, r_k):
    batch, seq, _ = x.shape
    in_width = w_in_bf.shape[1]
    tm = INPROJ_ROWS
    W = RWKV_WIDTH
    const = lambda b, j: (0, 0)
    blk = lambda b, j: (b, j, 0)
    vec = pl.BlockSpec((1, W), const)
    w_rows = tm // WKV_CHUNK * V7X_SUBLANES
    return pl.pallas_call(
        _inproj_kernel,
        grid=(batch, seq // tm),
        in_specs=[
            pl.BlockSpec((1, tm, D_MODEL), blk),
            pl.BlockSpec((1, 1, 6 * D_MODEL), lambda b, j: (b, 0, 0)),
            pl.BlockSpec((1, D_MODEL), const),
            pl.BlockSpec((D_MODEL, in_width), const, pipeline_mode=pl.Buffered(1)),
            pl.BlockSpec((1, RWKV_SHIFT_WIDTH), const),
            pl.BlockSpec((1, GATE_WIDTH), const),
            vec,
            pl.BlockSpec((DECAY_LORA, W), const),
            vec,
            pl.BlockSpec((ICLR_LORA, W), const),
            pl.BlockSpec((GATE_LORA, W), const),
            vec, vec, vec,
        ],
        out_specs=[
            pl.BlockSpec((1, tm, WKV_PACK_A), blk),
            pl.BlockSpec((1, tm, WKV_PACK_B), blk),
            pl.BlockSpec((1, w_rows, W), blk),
            pl.BlockSpec((1, tm, QKV_WIDTH), blk),
            pl.BlockSpec((1, tm, GATE_WIDTH), blk),
        ],
        out_shape=[
            jax.ShapeDtypeStruct((batch, seq, WKV_PACK_A), BF16),
            jax.ShapeDtypeStruct((batch, seq, WKV_PACK_B), BF16),
            jax.ShapeDtypeStruct((batch, seq // WKV_CHUNK * V7X_SUBLANES, W), F32),
            jax.ShapeDtypeStruct((batch, seq, QKV_WIDTH), BF16),
            jax.ShapeDtypeStruct((batch, seq, GATE_WIDTH), BF16),
        ],
        scratch_shapes=[pltpu.VMEM((1, RWKV_SHIFT_WIDTH), F32)],
        compiler_params=pltpu.CompilerParams(
            dimension_semantics=("parallel", "arbitrary"),
            vmem_limit_bytes=V7X_VMEM_LIMIT_BYTES),
        name="inproj",
    )(x, ada3, gain, w_in_bf, mu, gate_b, decay_w0.reshape(1, W), decay_up.astype(BF16),
      iclr_a0.reshape(1, W), iclr_up.astype(BF16), gate_up.astype(BF16), k_k.reshape(1, W),
      k_a.reshape(1, W), r_k.reshape(1, W))


def _cumsum_rows(x):
    n = x.shape[0]
    row = lax.broadcasted_iota(jnp.int32, x.shape, 0)
    s = 1
    while s < min(n, V7X_SUBLANES):
        x = x + jnp.where(row >= s, pltpu.roll(x, s, axis=0), 0.0)
        s *= 2
    while s < n:
        x = jnp.concatenate([x[:s], x[s:] + x[:n - s]], axis=0)
        s *= 2
    return x


def _wkv_prologue(cols, w0_ref, dup_ref, a0_ref, aup_ref, gup_ref, kk_ref, ka_ref, rk_ref):
    L, W = WKV_CHUNK, RWKV_WIDTH
    n_chunks = cols.shape[0] // L
    r = cols[:, 0:W]
    k = cols[:, W:2 * W]
    v = cols[:, 2 * W:3 * W]
    o = 3 * W
    xw = cols[:, o:o + DECAY_LORA]
    xa = cols[:, o + DECAY_LORA:o + DECAY_LORA + ICLR_LORA]
    xg = cols[:, o + DECAY_LORA + ICLR_LORA:]

    lw = jax.nn.sigmoid(w0_ref[...] + _mm(jnp.tanh(xw), dup_ref[...])) * (-math.exp(-0.5) * LOG2_E)
    a = jax.nn.sigmoid(a0_ref[...] + _mm(xa, aup_ref[...]))
    g = _mm(jax.nn.sigmoid(xg), gup_ref[...])
    kkp = k * kk_ref[...]
    kk = kkp * jnp.minimum(lax.rsqrt(_head_sum(kkp * kkp)), 1e12)
    k_mod = k * (a * ka_ref[...] + (1.0 - ka_ref[...]))
    bonus = _head_sum(r * k_mod * rk_ref[...]) * v

    cum = jnp.concatenate([_cumsum_rows(lw[c * L:(c + 1) * L]) for c in range(n_chunks)], axis=0)
    e_in = jnp.exp2(cum)
    e_neg = jnp.exp2(-cum)
    w_last = [e_in[(c + 1) * L - 1:(c + 1) * L, :] for c in range(n_chunks)]
    w_rows = jnp.concatenate([jnp.broadcast_to(w, (L, W)) for w in w_last], axis=0)
    bt = kk * a * e_neg
    kt = k_mod * e_neg
    pack_a = jnp.concatenate([-kk * jnp.exp2(cum - lw), r * e_in, g, bonus], axis=1).astype(BF16)
    pack_b = jnp.concatenate([bt, kt, bt * w_rows, kt * w_rows, v], axis=1).astype(BF16)
    w_pack = jnp.concatenate([jnp.broadcast_to(w, (V7X_SUBLANES, W)) for w in w_last], axis=0)
    return pack_a, pack_b, w_pack


def _wkv_chunks(pack_a, pack_b, w_pack, state_ref):
    L, W, HD = WKV_CHUNK, RWKV_WIDTH, HEAD_DIM
    n_chunks = pack_a.shape[0] // L
    row2 = lax.broadcasted_iota(jnp.int32, (2 * L, 2 * L), 0)
    col2 = lax.broadcasted_iota(jnp.int32, (2 * L, 2 * L), 1) % L
    lower2 = col2 < jnp.where(row2 < L, row2, row2 - L + 1)
    lane3 = lax.broadcasted_iota(jnp.int32, (L, 3 * HD), 1)
    zeros_b = jnp.zeros((L, HD), BF16)

    units = [(c, h) for c in range(n_chunks) for h in range(RWKV_HEADS)]

    def pick(arr, c, h):
        return arr[c * L:(c + 1) * L, h * HD:(h + 1) * HD]

    at_b, rt_b = pack_a[:, 0:W], pack_a[:, W:2 * W]
    bt_b, kt_b = pack_b[:, 0:W], pack_b[:, W:2 * W]
    bth_b, kth_b = pack_b[:, 2 * W:3 * W], pack_b[:, 3 * W:4 * W]
    v_b = pack_b[:, 4 * W:5 * W]
    ar_b = {u: (pick(at_b, *u), pick(rt_b, *u)) for u in units}
    a_t = {u: ar_b[u][0].astype(F32) for u in units}
    r_t = {u: ar_b[u][1].astype(F32) for u in units}
    v_h = {u: pick(v_b, *u) for u in units}
    w_l = {(c, h): w_pack[c * V7X_SUBLANES:c * V7X_SUBLANES + 1, h * HD:(h + 1) * HD]
           for c, h in units}
    bk_t = {u: jnp.concatenate([pick(bt_b, *u), pick(kt_b, *u)], axis=0) for u in units}
    bk_hat = {u: jnp.concatenate([pick(bth_b, *u), pick(kth_b, *u)], axis=0) for u in units}

    sc = {u: jnp.where(lower2, _dot_nt(jnp.concatenate(ar_b[u], axis=0), bk_t[u]), 0.0)
          for u in units}
    yield None
    sc_b = {u: sc[u].astype(BF16) for u in units}
    top = {u: sc_b[u][0:L] for u in units}
    bot = {u: sc_b[u][L:2 * L] for u in units}
    vz = {u: jnp.concatenate([zeros_b, v_h[u]], axis=1) for u in units}
    zeros_w = jnp.zeros((L, 2 * HD), BF16)
    akv = {u: _dot(top[u], jnp.concatenate([zeros_w, vz[u]], axis=0)) for u in units}
    yield None

    zeros_f = jnp.zeros((L, HD), F32)
    wx = {u: jnp.concatenate([akv[u] + jnp.concatenate([a_t[u], zeros_f], axis=1),
                              sc[u][0:L, 0:HD]], axis=1) for u in units}
    levels = L.bit_length() - 1
    for _ in range(levels):
        wx_b = {u: wx[u].astype(BF16) for u in units}
        wx = {u: _dot(wx_b[u][:, 2 * HD:3 * HD], wx_b[u]) + jnp.where(lane3 < 2 * HD, wx[u], 0.0)
              for u in units}
        yield None
    x2 = {u: jnp.concatenate([wx[u][:, 0:2 * HD].astype(BF16), vz[u]], axis=0) for u in units}
    ry = {u: _dot(bot[u], x2[u]) for u in units}
    yield None
    gs = {u: _dot_tn(x2[u], bk_hat[u]) for u in units}
    yield None

    y_rows = []
    for c in range(n_chunks):
        y_heads = []
        for h in range(RWKV_HEADS):
            u = (c, h)
            s0 = state_ref[h]
            s0_b = s0.astype(BF16)
            y_heads.append(_dot_nt((r_t[u] + ry[u][:, 0:HD]).astype(BF16), s0_b) + ry[u][:, HD:2 * HD])
            state_ref[h] = s0 * w_l[u] + _dot(s0_b, gs[u][0:HD].astype(BF16)) + gs[u][HD:2 * HD]
        y_rows.append(jnp.concatenate(y_heads, axis=1))
    yield jnp.concatenate(y_rows, axis=0)


def _head_sum(x):
    slabs = []
    for s in range(x.shape[1] // V7X_LANES):
        xs = x[:, s * V7X_LANES:(s + 1) * V7X_LANES]
        lo = lax.broadcasted_iota(jnp.int32, xs.shape, 1) < HEAD_DIM
        lo_sum = jnp.sum(jnp.where(lo, xs, 0.0), axis=-1, keepdims=True)
        hi_sum = jnp.sum(jnp.where(lo, 0.0, xs), axis=-1, keepdims=True)
        slabs.append(jnp.where(lo, lo_sum, hi_sum))
    return jnp.concatenate(slabs, axis=1)


def _head_mean(x):
    return _head_sum(x) * (1.0 / HEAD_DIM)


def _wkv_epilogue(y, pack_a, lng_ref, lnb_ref, head_mean):
    W = RWKV_WIDTH
    g, bonus = pack_a[:, 2 * W:3 * W].astype(F32), pack_a[:, 3 * W:4 * W].astype(F32)
    yc = y - head_mean(y)
    yn = yc * lax.rsqrt(head_mean(yc * yc) + GN_EPS) * lng_ref[...] + lnb_ref[...]
    return ((yn + bonus) * g).astype(BF16)


def _wkv_kernel(pa_ref, pb_ref, w_ref, lng_ref, lnb_ref, hsum_ref, y_ref, state_ref):
    rows = WKV_GROUP_ROWS
    w_rows = rows // WKV_CHUNK * V7X_SUBLANES
    n_groups = pa_ref.shape[1] // rows

    @pl.when(pl.program_id(1) == 0)
    def _():
        state_ref[...] = jnp.zeros_like(state_ref)

    hsum = hsum_ref[...]
    scans = [_wkv_chunks(pa_ref[0, gi * rows:(gi + 1) * rows, :],
                         pb_ref[0, gi * rows:(gi + 1) * rows, :],
                         w_ref[0, gi * w_rows:(gi + 1) * w_rows, :], state_ref) for gi in range(n_groups)]
    n_stages = WKV_CHUNK.bit_length() - 1 + 5
    ys = [None] * n_groups
    for t in range(n_stages + WKV_STAGE_SKEW * (n_groups - 1)):
        for gi in range(n_groups):
            stage = t - WKV_STAGE_SKEW * gi
            if 0 <= stage < n_stages:
                ys[gi] = next(scans[gi])
    for gi in range(n_groups):
        mean = (lambda t: _mm(t, hsum)) if gi == n_groups - 1 else _head_mean
        y_ref[0, gi * rows:(gi + 1) * rows, :] = _wkv_epilogue(
            ys[gi], pa_ref[0, gi * rows:(gi + 1) * rows, :], lng_ref, lnb_ref, mean)


def _wkv(pack_a, pack_b, w_pack, lnx_gain, lnx_bias):
    batch, seq, _ = pack_a.shape
    rows = WKV_ROWS
    W = RWKV_WIDTH
    const = lambda b, j: (0, 0)
    blk = lambda b, j: (b, j, 0)
    vec = pl.BlockSpec((1, W), const)
    hsum = _head_mean_matrix(W)
    return pl.pallas_call(
        _wkv_kernel,
        grid=(batch, seq // rows),
        in_specs=[
            pl.BlockSpec((1, rows, WKV_PACK_A), blk),
            pl.BlockSpec((1, rows, WKV_PACK_B), blk),
            pl.BlockSpec((1, rows // WKV_CHUNK * V7X_SUBLANES, W), blk),
            vec, vec,
            pl.BlockSpec((W, W), const),
        ],
        out_specs=pl.BlockSpec((1, rows, W), blk),
        out_shape=jax.ShapeDtypeStruct((batch, seq, W), BF16),
        scratch_shapes=[pltpu.VMEM((RWKV_HEADS, HEAD_DIM, HEAD_DIM), F32)],
        compiler_params=pltpu.CompilerParams(
            dimension_semantics=("parallel", "arbitrary")),
        name="wkv",
    )(pack_a, pack_b, w_pack, lnx_gain.reshape(1, W), lnx_bias.reshape(1, W), hsum)


def _attn_kernel(sink_ref, q_ref, kv_ref, cos_ref, sin_ref, bias_ref, qg_ref, kg_ref, hmean_ref,
                 perm_ref, o_ref, kprev_ref, vprev_ref):
    n_blk = q_ref.shape[1] // BLOCK
    n_slab = ATTN_Q_WIDTH // V7X_LANES
    slab_per_kv = n_slab // ATTN_KV_HEADS
    kvs = range(ATTN_KV_HEADS)
    pars = range(2 * ATTN_KV_HEADS)

    @pl.when(pl.program_id(1) == 0)
    def _():
        kprev_ref[...] = jnp.zeros_like(kprev_ref)
        vprev_ref[...] = jnp.zeros_like(vprev_ref)

    cos, sin = cos_ref[0], sin_ref[0]
    q_gain = qg_ref[...] * (HEAD_DIM ** -0.5 * LOG2_E)
    q_all = q_ref[0].astype(F32)
    kv = kv_ref[0].astype(F32)
    slabs = [q_all[:, s * V7X_LANES:(s + 1) * V7X_LANES] for s in range(n_slab)]
    slabs.append(kv[:, 0:ATTN_KV_WIDTH])
    gains = [q_gain] * n_slab + [kg_ref[...]]
    n_rows = n_blk * BLOCK

    ms_all = _mm(jnp.concatenate([x * x for x in slabs], axis=0), hmean_ref[...])
    xn = [x * lax.rsqrt(ms_all[i * n_rows:(i + 1) * n_rows] + RMS_EPS) * gains[i]
          for i, x in enumerate(slabs)]
    partner = _mm(jnp.concatenate(xn, axis=0), perm_ref[0])
    normed = [xn[i] * cos + partner[i * n_rows:(i + 1) * n_rows] * sin for i in range(len(slabs))]
    qn, k_cur = normed[:n_slab], normed[n_slab]
    v_cur = kv[:, ATTN_KV_WIDTH:]
    swapped = _mm(jnp.concatenate([k_cur, v_cur], axis=0), perm_ref[1])
    k_swap, v_swap = swapped[0:n_rows], swapped[n_rows:]
    lo = lax.broadcasted_iota(jnp.int32, (n_rows, V7X_LANES), 1) < HEAD_DIM
    kdup_cur = [jnp.where(lo, k_cur, k_swap).astype(BF16), jnp.where(lo, k_swap, k_cur).astype(BF16)]
    vpar_cur = [jnp.where(lo, v_cur, 0.0).astype(BF16), jnp.where(lo, 0.0, v_swap).astype(BF16),
                jnp.where(lo, v_swap, 0.0).astype(BF16), jnp.where(lo, 0.0, v_cur).astype(BF16)]

    def band(prev_ref, cur, j, i):
        if i == 0:
            return jnp.concatenate([prev_ref[j], cur[j][0:BLOCK]], axis=0)
        return cur[j][(i - 1) * BLOCK:(i + 1) * BLOCK]

    units = [(i, hk) for i in range(n_blk) for hk in kvs]
    kband = {(i, hk): band(kprev_ref, kdup_cur, hk, i) for i, hk in units}
    ones_b = jnp.ones((2 * BLOCK, V7X_LANES), BF16)
    vaug = {(i, j): jnp.concatenate([band(vprev_ref, vpar_cur, j, i), ones_b], axis=1)
            for i in range(n_blk) for j in pars}
    for hk in kvs:
        kprev_ref[hk] = kdup_cur[hk][n_rows - BLOCK:]
    for j in pars:
        vprev_ref[j] = vpar_cur[j][n_rows - BLOCK:]

    lo1 = lax.broadcasted_iota(jnp.int32, (BLOCK, V7X_LANES), 1) < HEAD_DIM
    stack = 2 * slab_per_kv
    first_bias = bias_ref[jnp.minimum(pl.program_id(1), 1)]
    bias = [jnp.concatenate([first_bias if i == 0 else bias_ref[1]] * stack, axis=0)
            for i in range(n_blk)]
    heads = [[2 * (hk * slab_per_kv + j) + p for p in range(2) for j in range(slab_per_kv)]
             for hk in kvs]
    lhs = {}
    for i, hk in units:
        mine = [qn[hk * slab_per_kv + j][i * BLOCK:(i + 1) * BLOCK] for j in range(slab_per_kv)]
        lhs[(i, hk)] = jnp.concatenate([jnp.where(lo1, x, 0.0) for x in mine]
                                       + [jnp.where(lo1, 0.0, x) for x in mine], axis=0).astype(BF16)
    s = {u: _mm_nt(lhs[u], kband[u]) + bias[u[0]] for u in units}
    rmax = {u: jnp.max(s[u], axis=-1, keepdims=True) for u in units}
    sink2 = [sink_ref[h] * LOG2_E for h in range(ATTN_Q_HEADS)]
    m = {(u, t): jnp.maximum(rmax[u][t * BLOCK:(t + 1) * BLOCK], sink2[heads[u[1]][t]])
         for u in units for t in range(stack)}
    e = {u: jnp.concatenate([jnp.exp2(s[u][t * BLOCK:(t + 1) * BLOCK] - m[(u, t)])
                             for t in range(stack)], axis=0).astype(BF16) for u in units}
    half = stack * BLOCK // 2
    pv = {(i, hk, p): _mm(e[(i, hk)][p * half:(p + 1) * half], vaug[(i, 2 * hk + p)])
          for i, hk in units for p in range(2)}
    extra = {(u, t): jnp.exp2(sink2[heads[u[1]][t]] - m[(u, t)]) for u in units for t in range(stack)}
    for i, hk in units:
        for j in range(slab_per_kv):
            r0 = slice(j * BLOCK, (j + 1) * BLOCK)
            even, odd = pv[(i, hk, 0)][r0], pv[(i, hk, 1)][r0]
            num = even[:, 0:V7X_LANES] + odd[:, 0:V7X_LANES]
            den = jnp.where(lo1, even[:, V7X_LANES:] + extra[((i, hk), j)],
                            odd[:, V7X_LANES:] + extra[((i, hk), slab_per_kv + j)])
            slab = hk * slab_per_kv + j
            o_ref[0, i * BLOCK:(i + 1) * BLOCK, slab * V7X_LANES:(slab + 1) * V7X_LANES] = (
                num * (1.0 / den)).astype(BF16)


def _attn(qkv, cos_tab, sin_tab, q_gain, k_gain, sinks):
    batch, seq, _ = qkv.shape
    rows = ATTN_ROWS
    cur = lambda b, n: (b, n, 0)
    const = lambda b, n: (0, 0)
    const3 = lambda b, n: (0, 0, 0)
    kv_blk = ATTN_Q_WIDTH // (2 * ATTN_KV_WIDTH)
    gain2 = lambda gn: jnp.tile(gn.reshape(1, HEAD_DIM), (1, V7X_LANES // HEAD_DIM))
    lane = jnp.arange(V7X_LANES)
    hmean = _head_mean_matrix(V7X_LANES)
    dim = lane % HEAD_DIM
    src = jnp.where(dim < ROPE_HALF, lane + ROPE_HALF, jnp.where(dim < ROPE_DIM, lane - ROPE_HALF, -1))
    partner_p = lane[:, None] == src[None, :]
    swap_p = lane[:, None] == ((lane + HEAD_DIM) % V7X_LANES)[None, :]
    perm = jnp.stack([partner_p, swap_p]).astype(BF16)
    dist = jnp.arange(BLOCK)[:, None] + BLOCK - jnp.arange(2 * BLOCK)[None, :]
    in_band = (dist >= 0) & (dist < WINDOW)
    own = (jnp.arange(2 * BLOCK) >= BLOCK)[None, :]
    bias = jnp.where(jnp.stack([in_band & own, in_band]), 0.0, NEG_INF).astype(F32)
    return pl.pallas_call(
        _attn_kernel,
        grid=(batch, seq // rows),
        in_specs=[
            pl.BlockSpec(memory_space=pltpu.SMEM),
            pl.BlockSpec((1, rows, ATTN_Q_WIDTH), cur),
            pl.BlockSpec((1, rows, 2 * ATTN_KV_WIDTH), lambda b, n: (b, n, kv_blk)),
            pl.BlockSpec((1, rows, V7X_LANES), cur),
            pl.BlockSpec((1, rows, V7X_LANES), cur),
            pl.BlockSpec((2, BLOCK, 2 * BLOCK), const3),
            pl.BlockSpec((1, V7X_LANES), const),
            pl.BlockSpec((1, V7X_LANES), const),
            pl.BlockSpec((V7X_LANES, V7X_LANES), const),
            pl.BlockSpec((2, V7X_LANES, V7X_LANES), const3),
        ],
        out_specs=pl.BlockSpec((1, rows, ATTN_Q_WIDTH), cur),
        out_shape=jax.ShapeDtypeStruct((batch, seq, ATTN_Q_WIDTH), BF16),
        scratch_shapes=[
            pltpu.VMEM((ATTN_KV_HEADS, BLOCK, V7X_LANES), BF16),
            pltpu.VMEM((2 * ATTN_KV_HEADS, BLOCK, V7X_LANES), BF16),
        ],
        compiler_params=pltpu.CompilerParams(
            dimension_semantics=("parallel", "arbitrary")),
        name="attn",
    )(sinks, qkv, qkv, cos_tab, sin_tab, bias, gain2(q_gain), gain2(k_gain), hmean, perm)


def _tail_kernel(x_ref, ada_ref, ya_ref, yb_ref, gt_ref, wa_ref, wb_ref, wo_ref, gain_ref,
                 w1_ref, w3_ref, w2_ref, o_ref):
    ada = ada_ref[0]
    gate1 = ada[:, 2 * D_MODEL:3 * D_MODEL]
    shift2 = ada[:, 3 * D_MODEL:4 * D_MODEL]
    scale2 = ada[:, 4 * D_MODEL:5 * D_MODEL]
    gate2 = ada[:, 5 * D_MODEL:6 * D_MODEL]
    mod2 = gain_ref[...] * (1.0 + scale2)

    rows = TAIL_SUB_ROWS
    subs = [slice(i * rows, (i + 1) * rows) for i in range(x_ref.shape[1] // rows)]
    ma = [_dot(ya_ref[0, rs, :], wa_ref[...]) for rs in subs]
    mb = [_dot(yb_ref[0, rs, :], wb_ref[...]) for rs in subs]
    merged = []
    for i, rs in enumerate(subs):
        gates = jax.nn.sigmoid(gt_ref[0, rs, :].astype(F32))
        merged.append((gates[:, 0:D_MODEL] * ma[i] + gates[:, D_MODEL:] * mb[i]).astype(BF16))
    x1 = [x_ref[0, rs, :] + gate1 * _dot(merged[i], wo_ref[...]) for i, rs in enumerate(subs)]
    h2 = []
    for x1_i in x1:
        inv = lax.rsqrt(jnp.mean(x1_i * x1_i, axis=-1, keepdims=True) + RMS_EPS)
        h2.append(((x1_i * inv) * mod2 + shift2).astype(BF16))
    a1 = [_dot(h, w1_ref[...]) for h in h2]
    a3 = [_dot(h, w3_ref[...]) for h in h2]
    z = [(jax.nn.silu(a1[i]) * a3[i]).astype(BF16) for i in range(len(subs))]
    for i, rs in enumerate(subs):
        o_ref[0, rs, :] = x1[i] + gate2 * _dot(z[i], w2_ref[...])


def _tail(x, ada3, ya, yb, gt, wa, wb, wo, gain2, w1, w3, w2):
    batch, seq, _ = x.shape
    tm = TAIL_ROWS
    d_ff = w1.shape[1]
    const = lambda b, j: (0, 0)
    rows = lambda width: pl.BlockSpec((1, tm, width), lambda b, j: (b, j, 0))
    weight = lambda shape: pl.BlockSpec(shape, const, pipeline_mode=pl.Buffered(1))
    return pl.pallas_call(
        _tail_kernel,
        grid=(batch, seq // tm),
        in_specs=[
            rows(D_MODEL),
            pl.BlockSpec((1, 1, 6 * D_MODEL), lambda b, j: (b, 0, 0)),
            rows(RWKV_WIDTH),
            rows(ATTN_Q_WIDTH),
            rows(GATE_WIDTH),
            weight((RWKV_WIDTH, D_MODEL)),
            weight((ATTN_Q_WIDTH, D_MODEL)),
            weight((D_MODEL, D_MODEL)),
            pl.BlockSpec((1, D_MODEL), const),
            weight((D_MODEL, d_ff)),
            weight((D_MODEL, d_ff)),
            weight((d_ff, D_MODEL)),
        ],
        out_specs=rows(D_MODEL),
        out_shape=jax.ShapeDtypeStruct((batch, seq, D_MODEL), F32),
        compiler_params=pltpu.CompilerParams(
            dimension_semantics=("parallel", "parallel"),
            vmem_limit_bytes=V7X_VMEM_LIMIT_BYTES),
        name="tail",
    )(x, ada3, ya, yb, gt, wa, wb, wo, gain2, w1, w3, w2)


def kernel(x, c, positions, ada_w, ada_b, norm1_gain, norm2_gain, w_in, tshift_mu, decay_w0,
           decay_up, iclr_a0, iclr_up, gate_up, k_k, k_a, r_k, lnx_gain, lnx_bias, q_norm_gain,
           k_norm_gain, attn_sinks, branch_gate_b, w_branch_a, w_branch_b, w_out, ffn_w1, ffn_w3,
           ffn_w2):
    depth = ada_w.shape[0]
    batch = x.shape[0]
    cos_tab, sin_tab = _rope_tables(positions)
    for l in range(depth):
        ada3 = _ada(c, ada_w[l], ada_b[l]).reshape(batch, 1, 6 * D_MODEL)
        pack_a, pack_b, w_pack, qkv, gt = _inproj(
            x, ada3, norm1_gain[l].reshape(1, D_MODEL), w_in[l].astype(BF16),
            tshift_mu[l].reshape(1, RWKV_SHIFT_WIDTH), branch_gate_b[l].reshape(1, GATE_WIDTH),
            decay_w0[l], decay_up[l], iclr_a0[l], iclr_up[l], gate_up[l], k_k[l], k_a[l], r_k[l])
        ya = _wkv(pack_a, pack_b, w_pack, lnx_gain[l], lnx_bias[l])
        yb = _attn(qkv, cos_tab, sin_tab, q_norm_gain[l], k_norm_gain[l], attn_sinks[l])
        x = _tail(x, ada3, ya, yb, gt, w_branch_a[l].astype(BF16), w_branch_b[l].astype(BF16),
                  w_out[l].astype(BF16), norm2_gain[l].reshape(1, D_MODEL),
                  ffn_w1[l].astype(BF16), ffn_w3[l].astype(BF16), ffn_w2[l].astype(BF16))
    return x
```

```python
import math

import jax
import jax.numpy as jnp
from jax import lax
from jax.experimental import pallas as pl
from jax.experimental.pallas import tpu as pltpu

F32 = jnp.float32
BF16 = jnp.bfloat16

D_MODEL = 1024
HEAD_DIM = 64
RWKV_HEADS = 8
RWKV_WIDTH = RWKV_HEADS * HEAD_DIM
DECAY_LORA = 64
ICLR_LORA = 64
GATE_LORA = 128
ATTN_Q_HEADS = 8
ATTN_KV_HEADS = 2
ATTN_GROUPS = ATTN_Q_HEADS // ATTN_KV_HEADS
ATTN_Q_WIDTH = ATTN_Q_HEADS * HEAD_DIM
ATTN_KV_WIDTH = ATTN_KV_HEADS * HEAD_DIM
WINDOW = 128
BLOCK = 128
ROPE_THETA = 500000.0
ROPE_DIM = HEAD_DIM // 4
ROPE_HALF = ROPE_DIM // 2
RMS_EPS = 1e-6
GN_EPS = 64e-5
NEG_INF = -1e30
LOG2_E = math.log2(math.e)
RWKV_SHIFT_WIDTH = 3 * RWKV_WIDTH + DECAY_LORA + ICLR_LORA + GATE_LORA
QKV_WIDTH = ATTN_Q_WIDTH + 2 * ATTN_KV_WIDTH
GATE_WIDTH = 2 * D_MODEL
WKV_PACK_A = 4 * RWKV_WIDTH
WKV_PACK_B = 5 * RWKV_WIDTH

V7X_LANES = 128
V7X_SUBLANES = 8
V7X_VMEM_LIMIT_BYTES = 56 * 1024 * 1024

INPROJ_ROWS = 512
INPROJ_SUB_ROWS = 256
WKV_CHUNK = 64
WKV_GROUP_ROWS = 256
WKV_ROWS = 512
WKV_STAGE_SKEW = 4
ATTN_ROWS = 2048
TAIL_ROWS = 512
TAIL_SUB_ROWS = 256


def _dot(a, b):
    return jnp.dot(a, b, preferred_element_type=F32)


def _dot_nt(a, b):
    return lax.dot_general(a, b, (((1,), (1,)), ((), ())), preferred_element_type=F32)


def _dot_tn(a, b):
    return lax.dot_general(a, b, (((0,), (0,)), ((), ())), preferred_element_type=F32)


def _head_mean_matrix(width):
    head = jnp.arange(width) // HEAD_DIM
    return jnp.where(head[:, None] == head[None, :], 1.0 / HEAD_DIM, 0.0).astype(BF16)


def _mm(a, b):
    return _dot(a.astype(BF16), b.astype(BF16))


def _mm_nt(a, b):
    return _dot_nt(a.astype(BF16), b.astype(BF16))


def _ada_kernel(c_ref, w_ref, b_ref, o_ref):
    o_ref[...] = jnp.dot(c_ref[...], w_ref[...], precision=lax.Precision.HIGHEST,
                         preferred_element_type=F32) + b_ref[...]


def _ada(c, ada_w, ada_b):
    batch = c.shape[0]
    n_out = ada_w.shape[1]
    return pl.pallas_call(
        _ada_kernel,
        grid=(n_out // D_MODEL,),
        in_specs=[
            pl.BlockSpec((batch, D_MODEL), lambda j: (0, 0)),
            pl.BlockSpec((D_MODEL, D_MODEL), lambda j: (0, j)),
            pl.BlockSpec((1, D_MODEL), lambda j: (0, j)),
        ],
        out_specs=pl.BlockSpec((batch, D_MODEL), lambda j: (0, j)),
        out_shape=jax.ShapeDtypeStruct((batch, n_out), F32),
        name="ada",
    )(c, ada_w, ada_b.reshape(1, n_out))


def _rope_kernel(pos_ref, freq_ref, sgn_ref, cos_ref, sin_ref):
    per_row = V7X_LANES // ROPE_DIM
    dense_rows = pos_ref.shape[1]
    ang = pos_ref[0].astype(F32) * freq_ref[...]
    cos_d = jnp.cos(ang)
    sin_d = jnp.sin(ang) * sgn_ref[...]
    lane = lax.broadcasted_iota(jnp.int32, ang.shape, 1)
    rotary0 = lane < ROPE_DIM
    rotary1 = (lane >= HEAD_DIM) & (lane < HEAD_DIM + ROPE_DIM)
    for i in range(per_row):
        shift = (V7X_LANES - ROPE_DIM * i) % V7X_LANES
        for dense, fill, out_ref in ((cos_d, 1.0, cos_ref), (sin_d, 0.0, sin_ref)):
            head0 = pltpu.roll(dense, shift, axis=1) if shift else dense
            head1 = pltpu.roll(head0, HEAD_DIM, axis=1)
            row = jnp.where(rotary0, head0, jnp.where(rotary1, head1, fill))
            out_ref[0, pl.ds(i, dense_rows, stride=per_row), :] = row


def _rope_tables(positions):
    batch, seq = positions.shape
    per_row = V7X_LANES // ROPE_DIM
    inv_freq = ROPE_THETA ** (-jnp.arange(ROPE_HALF, dtype=F32) / ROPE_HALF)
    dim = jnp.arange(V7X_LANES) % ROPE_DIM
    freq = inv_freq[dim % ROPE_HALF].reshape(1, V7X_LANES)
    sgn = jnp.where(dim < ROPE_HALF, -1.0, 1.0).astype(F32).reshape(1, V7X_LANES)
    pos = jnp.repeat(positions.reshape(batch, seq // per_row, per_row), ROPE_DIM, axis=-1)
    vec_spec = pl.BlockSpec((1, V7X_LANES), lambda b: (0, 0))
    tab_spec = pl.BlockSpec((1, seq, V7X_LANES), lambda b: (b, 0, 0))
    tab = jax.ShapeDtypeStruct((batch, seq, V7X_LANES), F32)
    return pl.pallas_call(
        _rope_kernel,
        grid=(batch,),
        in_specs=[pl.BlockSpec((1, seq // per_row, V7X_LANES), lambda b: (b, 0, 0)),
                  vec_spec, vec_spec],
        out_specs=[tab_spec, tab_spec],
        out_shape=[tab, tab],
        name="rope",
    )(pos, freq, sgn)


def _inproj_kernel(x_ref, ada_ref, gain_ref, w_ref, mu_ref, gb_ref, w0_ref, dup_ref, a0_ref,
                   aup_ref, gup_ref, kk_ref, ka_ref, rk_ref,
                   pa_ref, pb_ref, wl_ref, qkv_ref, gt_ref, carry_ref):
    rows = INPROJ_SUB_ROWS
    w_rows = rows // WKV_CHUNK * V7X_SUBLANES
    ada = ada_ref[0]
    shift1 = ada[:, 0:D_MODEL]
    mod1 = gain_ref[...] * (1.0 + ada[:, D_MODEL:2 * D_MODEL])

    @pl.when(pl.program_id(1) == 0)
    def _():
        carry_ref[...] = jnp.zeros_like(carry_ref)

    last = carry_ref[...]
    subs = [slice(i * rows, (i + 1) * rows) for i in range(x_ref.shape[1] // rows)]
    hs, cols = [], []
    for rs in subs:
        x = x_ref[0, rs, :]
        inv = lax.rsqrt(jnp.mean(x * x, axis=-1, keepdims=True) + RMS_EPS)
        h = ((x * inv) * mod1 + shift1).astype(BF16)
        p = _dot(h, w_ref[:, 0:RWKV_SHIFT_WIDTH])
        prev = pltpu.roll(p, 1, axis=0)
        row = lax.broadcasted_iota(jnp.int32, p.shape, 0)
        prev = jnp.where(row == 0, last, prev)
        last = p[rows - 1:rows, :]
        hs.append(h)
        cols.append(p + (prev - p) * mu_ref[...])
    carry_ref[...] = last
    for i, rs in enumerate(subs):
        pack_a, pack_b, w_pack = _wkv_prologue(cols[i], w0_ref, dup_ref, a0_ref, aup_ref,
                                                    gup_ref, kk_ref, ka_ref, rk_ref)
        pa_ref[0, rs, :] = pack_a
        pb_ref[0, rs, :] = pack_b
        wl_ref[0, i * w_rows:(i + 1) * w_rows, :] = w_pack
        qkv_ref[0, rs, :] = _dot(hs[i], w_ref[:, RWKV_SHIFT_WIDTH:RWKV_SHIFT_WIDTH + QKV_WIDTH]).astype(BF16)
        gl = _dot(hs[i], w_ref[:, RWKV_SHIFT_WIDTH + QKV_WIDTH:])
        gt_ref[0, rs, :] = (gl + gb_ref[...]).astype(BF16)


def _inproj(x, ada3, gain, w_in_bf, mu, gate_b, decay_w0, decay_up, iclr_a0, iclr_up, gate_up,
            k_k, k_a, r_k):
    batch, seq, _ = x.shape
    in_width = w_in_bf.shape[1]
    tm = INPROJ_ROWS
    W = RWKV_WIDTH
    const = lambda b, j: (0, 0)
    blk = lambda b, j: (b, j, 0)
    vec = pl.BlockSpec((1, W), const)
    w_rows = tm // WKV_CHUNK * V7X_SUBLANES
    return pl.pallas_call(
        _inproj_kernel,
        grid=(batch, seq // tm),
        in_specs=[
            pl.BlockSpec((1, tm, D_MODEL), blk),
            pl.BlockSpec((1, 1, 6 * D_MODEL), lambda b, j: (b, 0, 0)),
            pl.BlockSpec((1, D_MODEL), const),
            pl.BlockSpec((D_MODEL, in_width), const, pipeline_mode=pl.Buffered(1)),
            pl.BlockSpec((1, RWKV_SHIFT_WIDTH), const),
            pl.BlockSpec((1, GATE_WIDTH), const),
            vec,
            pl.BlockSpec((DECAY_LORA, W), const),
            vec,
            pl.BlockSpec((ICLR_LORA, W), const),
            pl.BlockSpec((GATE_LORA, W), const),
            vec, vec, vec,
        ],
        out_specs=[
            pl.BlockSpec((1, tm, WKV_PACK_A), blk),
            pl.BlockSpec((1, tm, WKV_PACK_B), blk),
            pl.BlockSpec((1, w_rows, W), blk),
            pl.BlockSpec((1, tm, QKV_WIDTH), blk),
            pl.BlockSpec((1, tm, GATE_WIDTH), blk),
        ],
        out_shape=[
            jax.ShapeDtypeStruct((batch, seq, WKV_PACK_A), BF16),
            jax.ShapeDtypeStruct((batch, seq, WKV_PACK_B), BF16),
            jax.ShapeDtypeStruct((batch, seq // WKV_CHUNK * V7X_SUBLANES, W), F32),
            jax.ShapeDtypeStruct((batch, seq, QKV_WIDTH), BF16),
            jax.ShapeDtypeStruct((batch, seq, GATE_WIDTH), BF16),
        ],
        scratch_shapes=[pltpu.VMEM((1, RWKV_SHIFT_WIDTH), F32)],
        compiler_params=pltpu.CompilerParams(
            dimension_semantics=("parallel", "arbitrary"),
            vmem_limit_bytes=V7X_VMEM_LIMIT_BYTES),
        name="inproj",
    )(x, ada3, gain, w_in_bf, mu, gate_b, decay_w0.reshape(1, W), decay_up.astype(BF16),
      iclr_a0.reshape(1, W), iclr_up.astype(BF16), gate_up.astype(BF16), k_k.reshape(1, W),
      k_a.reshape(1, W), r_k.reshape(1, W))


def _cumsum_rows(x):
    n = x.shape[0]
    row = lax.broadcasted_iota(jnp.int32, x.shape, 0)
    s = 1
    while s < min(n, V7X_SUBLANES):
        x = x + jnp.where(row >= s, pltpu.roll(x, s, axis=0), 0.0)
        s *= 2
    while s < n:
        x = jnp.concatenate([x[:s], x[s:] + x[:n - s]], axis=0)
        s *= 2
    return x


def _wkv_prologue(cols, w0_ref, dup_ref, a0_ref, aup_ref, gup_ref, kk_ref, ka_ref, rk_ref):
    L, W = WKV_CHUNK, RWKV_WIDTH
    n_chunks = cols.shape[0] // L
    r = cols[:, 0:W]
    k = cols[:, W:2 * W]
    v = cols[:, 2 * W:3 * W]
    o = 3 * W
    xw = cols[:, o:o + DECAY_LORA]
    xa = cols[:, o + DECAY_LORA:o + DECAY_LORA + ICLR_LORA]
    xg = cols[:, o + DECAY_LORA + ICLR_LORA:]

    lw = jax.nn.sigmoid(w0_ref[...] + _mm(jnp.tanh(xw), dup_ref[...])) * (-math.exp(-0.5) * LOG2_E)
    a = jax.nn.sigmoid(a0_ref[...] + _mm(xa, aup_ref[...]))
    g = _mm(jax.nn.sigmoid(xg), gup_ref[...])
    kkp = k * kk_ref[...]
    kk = kkp * jnp.minimum(lax.rsqrt(_head_sum(kkp * kkp)), 1e12)
    k_mod = k * (a * ka_ref[...] + (1.0 - ka_ref[...]))
    bonus = _head_sum(r * k_mod * rk_ref[...]) * v

    cum = jnp.concatenate([_cumsum_rows(lw[c * L:(c + 1) * L]) for c in range(n_chunks)], axis=0)
    e_in = jnp.exp2(cum)
    e_neg = jnp.exp2(-cum)
    w_last = [e_in[(c + 1) * L - 1:(c + 1) * L, :] for c in range(n_chunks)]
    w_rows = jnp.concatenate([jnp.broadcast_to(w, (L, W)) for w in w_last], axis=0)
    bt = kk * a * e_neg
    kt = k_mod * e_neg
    pack_a = jnp.concatenate([-kk * jnp.exp2(cum - lw), r * e_in, g, bonus], axis=1).astype(BF16)
    pack_b = jnp.concatenate([bt, kt, bt * w_rows, kt * w_rows, v], axis=1).astype(BF16)
    w_pack = jnp.concatenate([jnp.broadcast_to(w, (V7X_SUBLANES, W)) for w in w_last], axis=0)
    return pack_a, pack_b, w_pack


def _wkv_chunks(pack_a, pack_b, w_pack, state_ref):
    L, W, HD = WKV_CHUNK, RWKV_WIDTH, HEAD_DIM
    n_chunks = pack_a.shape[0] // L
    row2 = lax.broadcasted_iota(jnp.int32, (2 * L, 2 * L), 0)
    col2 = lax.broadcasted_iota(jnp.int32, (2 * L, 2 * L), 1) % L
    lower2 = col2 < jnp.where(row2 < L, row2, row2 - L + 1)
    lane3 = lax.broadcasted_iota(jnp.int32, (L, 3 * HD), 1)
    zeros_b = jnp.zeros((L, HD), BF16)

    units = [(c, h) for c in range(n_chunks) for h in range(RWKV_HEADS)]

    def pick(arr, c, h):
        return arr[c * L:(c + 1) * L, h * HD:(h + 1) * HD]

    at_b, rt_b = pack_a[:, 0:W], pack_a[:, W:2 * W]
    bt_b, kt_b = pack_b[:, 0:W], pack_b[:, W:2 * W]
    bth_b, kth_b = pack_b[:, 2 * W:3 * W], pack_b[:, 3 * W:4 * W]
    v_b = pack_b[:, 4 * W:5 * W]
    ar_b = {u: (pick(at_b, *u), pick(rt_b, *u)) for u in units}
    a_t = {u: ar_b[u][0].astype(F32) for u in units}
    r_t = {u: ar_b[u][1].astype(F32) for u in units}
    v_h = {u: pick(v_b, *u) for u in units}
    w_l = {(c, h): w_pack[c * V7X_SUBLANES:c * V7X_SUBLANES + 1, h * HD:(h + 1) * HD]
           for c, h in units}
    bk_t = {u: jnp.concatenate([pick(bt_b, *u), pick(kt_b, *u)], axis=0) for u in units}
    bk_hat = {u: jnp.concatenate([pick(bth_b, *u), pick(kth_b, *u)], axis=0) for u in units}

    sc = {u: jnp.where(lower2, _dot_nt(jnp.concatenate(ar_b[u], axis=0), bk_t[u]), 0.0)
          for u in units}
    yield None
    sc_b = {u: sc[u].astype(BF16) for u in units}
    top = {u: sc_b[u][0:L] for u in units}
    bot = {u: sc_b[u][L:2 * L] for u in units}
    vz = {u: jnp.concatenate([zeros_b, v_h[u]], axis=1) for u in units}
    zeros_w = jnp.zeros((L, 2 * HD), BF16)
    akv = {u: _dot(top[u], jnp.concatenate([zeros_w, vz[u]], axis=0)) for u in units}
    yield None

    zeros_f = jnp.zeros((L, HD), F32)
    wx = {u: jnp.concatenate([akv[u] + jnp.concatenate([a_t[u], zeros_f], axis=1),
                              sc[u][0:L, 0:HD]], axis=1) for u in units}
    levels = L.bit_length() - 1
    for _ in range(levels):
        wx_b = {u: wx[u].astype(BF16) for u in units}
        wx = {u: _dot(wx_b[u][:, 2 * HD:3 * HD], wx_b[u]) + jnp.where(lane3 < 2 * HD, wx[u], 0.0)
              for u in units}
        yield None
    x2 = {u: jnp.concatenate([wx[u][:, 0:2 * HD].astype(BF16), vz[u]], axis=0) for u in units}
    ry = {u: _dot(bot[u], x2[u]) for u in units}
    yield None
    gs = {u: _dot_tn(x2[u], bk_hat[u]) for u in units}
    yield None

    y_rows = []
    for c in range(n_chunks):
        y_heads = []
        for h in range(RWKV_HEADS):
            u = (c, h)
            s0 = state_ref[h]
            s0_b = s0.astype(BF16)
            y_heads.append(_dot_nt((r_t[u] + ry[u][:, 0:HD]).astype(BF16), s0_b) + ry[u][:, HD:2 * HD])
            state_ref[h] = s0 * w_l[u] + _dot(s0_b, gs[u][0:HD].astype(BF16)) + gs[u][HD:2 * HD]
        y_rows.append(jnp.concatenate(y_heads, axis=1))
    yield jnp.concatenate(y_rows, axis=0)


def _head_sum(x):
    slabs = []
    for s in range(x.shape[1] // V7X_LANES):
        xs = x[:, s * V7X_LANES:(s + 1) * V7X_LANES]
        lo = lax.broadcasted_iota(jnp.int32, xs.shape, 1) < HEAD_DIM
        lo_sum = jnp.sum(jnp.where(lo, xs, 0.0), axis=-1, keepdims=True)
        hi_sum = jnp.sum(jnp.where(lo, 0.0, xs), axis=-1, keepdims=True)
        slabs.append(jnp.where(lo, lo_sum, hi_sum))
    return jnp.concatenate(slabs, axis=1)


def _head_mean(x):
    return _head_sum(x) * (1.0 / HEAD_DIM)


def _wkv_epilogue(y, pack_a, lng_ref, lnb_ref, head_mean):
    W = RWKV_WIDTH
    g, bonus = pack_a[:, 2 * W:3 * W].astype(F32), pack_a[:, 3 * W:4 * W].astype(F32)
    yc = y - head_mean(y)
    yn = yc * lax.rsqrt(head_mean(yc * yc) + GN_EPS) * lng_ref[...] + lnb_ref[...]
    return ((yn + bonus) * g).astype(BF16)


def _wkv_kernel(pa_ref, pb_ref, w_ref, lng_ref, lnb_ref, hsum_ref, y_ref, state_ref):
    rows = WKV_GROUP_ROWS
    w_rows = rows // WKV_CHUNK * V7X_SUBLANES
    n_groups = pa_ref.shape[1] // rows

    @pl.when(pl.program_id(1) == 0)
    def _():
        state_ref[...] = jnp.zeros_like(state_ref)

    hsum = hsum_ref[...]
    scans = [_wkv_chunks(pa_ref[0, gi * rows:(gi + 1) * rows, :],
                         pb_ref[0, gi * rows:(gi + 1) * rows, :],
                         w_ref[0, gi * w_rows:(gi + 1) * w_rows, :], state_ref) for gi in range(n_groups)]
    n_stages = WKV_CHUNK.bit_length() - 1 + 5
    ys = [None] * n_groups
    for t in range(n_stages + WKV_STAGE_SKEW * (n_groups - 1)):
        for gi in range(n_groups):
            stage = t - WKV_STAGE_SKEW * gi
            if 0 <= stage < n_stages:
                ys[gi] = next(scans[gi])
    for gi in range(n_groups):
        mean = (lambda t: _mm(t, hsum)) if gi == n_groups - 1 else _head_mean
        y_ref[0, gi * rows:(gi + 1) * rows, :] = _wkv_epilogue(
            ys[gi], pa_ref[0, gi * rows:(gi + 1) * rows, :], lng_ref, lnb_ref, mean)


def _wkv(pack_a, pack_b, w_pack, lnx_gain, lnx_bias):
    batch, seq, _ = pack_a.shape
    rows = WKV_ROWS
    W = RWKV_WIDTH
    const = lambda b, j: (0, 0)
    blk = lambda b, j: (b, j, 0)
    vec = pl.BlockSpec((1, W), const)
    hsum = _head_mean_matrix(W)
    return pl.pallas_call(
        _wkv_kernel,
        grid=(batch, seq // rows),
        in_specs=[
            pl.BlockSpec((1, rows, WKV_PACK_A), blk),
            pl.BlockSpec((1, rows, WKV_PACK_B), blk),
            pl.BlockSpec((1, rows // WKV_CHUNK * V7X_SUBLANES, W), blk),
            vec, vec,
            pl.BlockSpec((W, W), const),
        ],
        out_specs=pl.BlockSpec((1, rows, W), blk),
        out_shape=jax.ShapeDtypeStruct((batch, seq, W), BF16),
        scratch_shapes=[pltpu.VMEM((RWKV_HEADS, HEAD_DIM, HEAD_DIM), F32)],
        compiler_params=pltpu.CompilerParams(
            dimension_semantics=("parallel", "arbitrary")),
        name="wkv",
    )(pack_a, pack_b, w_pack, lnx_gain.reshape(1, W), lnx_bias.reshape(1, W), hsum)


def _attn_kernel(sink_ref, q_ref, kv_ref, cos_ref, sin_ref, bias_ref, qg_ref, kg_ref, hmean_ref,
                 perm_ref, o_ref, kprev_ref, vprev_ref):
    n_blk = q_ref.shape[1] // BLOCK
    n_slab = ATTN_Q_WIDTH // V7X_LANES
    slab_per_kv = n_slab // ATTN_KV_HEADS
    kvs = range(ATTN_KV_HEADS)
    pars = range(2 * ATTN_KV_HEADS)

    @pl.when(pl.program_id(1) == 0)
    def _():
        kprev_ref[...] = jnp.zeros_like(kprev_ref)
        vprev_ref[...] = jnp.zeros_like(vprev_ref)

    cos, sin = cos_ref[0], sin_ref[0]
    q_gain = qg_ref[...] * (HEAD_DIM ** -0.5 * LOG2_E)
    q_all = q_ref[0].astype(F32)
    kv = kv_ref[0].astype(F32)
    slabs = [q_all[:, s * V7X_LANES:(s + 1) * V7X_LANES] for s in range(n_slab)]
    slabs.append(kv[:, 0:ATTN_KV_WIDTH])
    gains = [q_gain] * n_slab + [kg_ref[...]]
    n_rows = n_blk * BLOCK

    ms_all = _mm(jnp.concatenate([x * x for x in slabs], axis=0), hmean_ref[...])
    xn = [x * lax.rsqrt(ms_all[i * n_rows:(i + 1) * n_rows] + RMS_EPS) * gains[i]
          for i, x in enumerate(slabs)]
    partner = _mm(jnp.concatenate(xn, axis=0), perm_ref[0])
    normed = [xn[i] * cos + partner[i * n_rows:(i + 1) * n_rows] * sin for i in range(len(slabs))]
    qn, k_cur = normed[:n_slab], normed[n_slab]
    v_cur = kv[:, ATTN_KV_WIDTH:]
    swapped = _mm(jnp.concatenate([k_cur, v_cur], axis=0), perm_ref[1])
    k_swap, v_swap = swapped[0:n_rows], swapped[n_rows:]
    lo = lax.broadcasted_iota(jnp.int32, (n_rows, V7X_LANES), 1) < HEAD_DIM
    kdup_cur = [jnp.where(lo, k_cur, k_swap).astype(BF16), jnp.where(lo, k_swap, k_cur).astype(BF16)]
    vpar_cur = [jnp.where(lo, v_cur, 0.0).astype(BF16), jnp.where(lo, 0.0, v_swap).astype(BF16),
                jnp.where(lo, v_swap, 0.0).astype(BF16), jnp.where(lo, 0.0, v_cur).astype(BF16)]

    def band(prev_ref, cur, j, i):
        if i == 0:
            return jnp.concatenate([prev_ref[j], cur[j][0:BLOCK]], axis=0)
        return cur[j][(i - 1) * BLOCK:(i + 1) * BLOCK]

    units = [(i, hk) for i in range(n_blk) for hk in kvs]
    kband = {(i, hk): band(kprev_ref, kdup_cur, hk, i) for i, hk in units}
    ones_b = jnp.ones((2 * BLOCK, V7X_LANES), BF16)
    vaug = {(i, j): jnp.concatenate([band(vprev_ref, vpar_cur, j, i), ones_b], axis=1)
            for i in range(n_blk) for j in pars}
    for hk in kvs:
        kprev_ref[hk] = kdup_cur[hk][n_rows - BLOCK:]
    for j in pars:
        vprev_ref[j] = vpar_cur[j][n_rows - BLOCK:]

    lo1 = lax.broadcasted_iota(jnp.int32, (BLOCK, V7X_LANES), 1) < HEAD_DIM
    stack = 2 * slab_per_kv
    first_bias = bias_ref[jnp.minimum(pl.program_id(1), 1)]
    bias = [jnp.concatenate([first_bias if i == 0 else bias_ref[1]] * stack, axis=0)
            for i in range(n_blk)]
    heads = [[2 * (hk * slab_per_kv + j) + p for p in range(2) for j in range(slab_per_kv)]
             for hk in kvs]
    lhs = {}
    for i, hk in units:
        mine = [qn[hk * slab_per_kv + j][i * BLOCK:(i + 1) * BLOCK] for j in range(slab_per_kv)]
        lhs[(i, hk)] = jnp.concatenate([jnp.where(lo1, x, 0.0) for x in mine]
                                       + [jnp.where(lo1, 0.0, x) for x in mine], axis=0).astype(BF16)
    s = {u: _mm_nt(lhs[u], kband[u]) + bias[u[0]] for u in units}
    rmax = {u: jnp.max(s[u], axis=-1, keepdims=True) for u in units}
    sink2 = [sink_ref[h] * LOG2_E for h in range(ATTN_Q_HEADS)]
    m = {(u, t): jnp.maximum(rmax[u][t * BLOCK:(t + 1) * BLOCK], sink2[heads[u[1]][t]])
         for u in units for t in range(stack)}
    e = {u: jnp.concatenate([jnp.exp2(s[u][t * BLOCK:(t + 1) * BLOCK] - m[(u, t)])
                             for t in range(stack)], axis=0).astype(BF16) for u in units}
    half = stack * BLOCK // 2
    pv = {(i, hk, p): _mm(e[(i, hk)][p * half:(p + 1) * half], vaug[(i, 2 * hk + p)])
          for i, hk in units for p in range(2)}
    extra = {(u, t): jnp.exp2(sink2[heads[u[1]][t]] - m[(u, t)]) for u in units for t in range(stack)}
    for i, hk in units:
        for j in range(slab_per_kv):
            r0 = slice(j * BLOCK, (j + 1) * BLOCK)
            even, odd = pv[(i, hk, 0)][r0], pv[(i, hk, 1)][r0]
            num = even[:, 0:V7X_LANES] + odd[:, 0:V7X_LANES]
            den = jnp.where(lo1, even[:, V7X_LANES:] + extra[((i, hk), j)],
                            odd[:, V7X_LANES:] + extra[((i, hk), slab_per_kv + j)])
            slab = hk * slab_per_kv + j
            o_ref[0, i * BLOCK:(i + 1) * BLOCK, slab * V7X_LANES:(slab + 1) * V7X_LANES] = (
                num * (1.0 / den)).astype(BF16)


def _attn(qkv, cos_tab, sin_tab, q_gain, k_gain, sinks):
    batch, seq, _ = qkv.shape
    rows = ATTN_ROWS
    cur = lambda b, n: (b, n, 0)
    const = lambda b, n: (0, 0)
    const3 = lambda b, n: (0, 0, 0)
    kv_blk = ATTN_Q_WIDTH // (2 * ATTN_KV_WIDTH)
    gain2 = lambda gn: jnp.tile(gn.reshape(1, HEAD_DIM), (1, V7X_LANES // HEAD_DIM))
    lane = jnp.arange(V7X_LANES)
    hmean = _head_mean_matrix(V7X_LANES)
    dim = lane % HEAD_DIM
    src = jnp.where(dim < ROPE_HALF, lane + ROPE_HALF, jnp.where(dim < ROPE_DIM, lane - ROPE_HALF, -1))
    partner_p = lane[:, None] == src[None, :]
    swap_p = lane[:, None] == ((lane + HEAD_DIM) % V7X_LANES)[None, :]
    perm = jnp.stack([partner_p, swap_p]).astype(BF16)
    dist = jnp.arange(BLOCK)[:, None] + BLOCK - jnp.arange(2 * BLOCK)[None, :]
    in_band = (dist >= 0) & (dist < WINDOW)
    own = (jnp.arange(2 * BLOCK) >= BLOCK)[None, :]
    bias = jnp.where(jnp.stack([in_band & own, in_band]), 0.0, NEG_INF).astype(F32)
    return pl.pallas_call(
        _attn_kernel,
        grid=(batch, seq // rows),
        in_specs=[
            pl.BlockSpec(memory_space=pltpu.SMEM),
            pl.BlockSpec((1, rows, ATTN_Q_WIDTH), cur),
            pl.BlockSpec((1, rows, 2 * ATTN_KV_WIDTH), lambda b, n: (b, n, kv_blk)),
            pl.BlockSpec((1, rows, V7X_LANES), cur),
            pl.BlockSpec((1, rows, V7X_LANES), cur),
            pl.BlockSpec((2, BLOCK, 2 * BLOCK), const3),
            pl.BlockSpec((1, V7X_LANES), const),
            pl.BlockSpec((1, V7X_LANES), const),
            pl.BlockSpec((V7X_LANES, V7X_LANES), const),
            pl.BlockSpec((2, V7X_LANES, V7X_LANES), const3),
        ],
        out_specs=pl.BlockSpec((1, rows, ATTN_Q_WIDTH), cur),
        out_shape=jax.ShapeDtypeStruct((batch, seq, ATTN_Q_WIDTH), BF16),
        scratch_shapes=[
            pltpu.VMEM((ATTN_KV_HEADS, BLOCK, V7X_LANES), BF16),
            pltpu.VMEM((2 * ATTN_KV_HEADS, BLOCK, V7X_LANES), BF16),
        ],
        compiler_params=pltpu.CompilerParams(
            dimension_semantics=("parallel", "arbitrary")),
        name="attn",
    )(sinks, qkv, qkv, cos_tab, sin_tab, bias, gain2(q_gain), gain2(k_gain), hmean, perm)


def _tail_kernel(x_ref, ada_ref, ya_ref, yb_ref, gt_ref, wa_ref, wb_ref, wo_ref, gain_ref,
                 w1_ref, w3_ref, w2_ref, o_ref):
    ada = ada_ref[0]
    gate1 = ada[:, 2 * D_MODEL:3 * D_MODEL]
    shift2 = ada[:, 3 * D_MODEL:4 * D_MODEL]
    scale2 = ada[:, 4 * D_MODEL:5 * D_MODEL]
    gate2 = ada[:, 5 * D_MODEL:6 * D_MODEL]
    mod2 = gain_ref[...] * (1.0 + scale2)

    rows = TAIL_SUB_ROWS
    subs = [slice(i * rows, (i + 1) * rows) for i in range(x_ref.shape[1] // rows)]
    ma = [_dot(ya_ref[0, rs, :], wa_ref[...]) for rs in subs]
    mb = [_dot(yb_ref[0, rs, :], wb_ref[...]) for rs in subs]
    merged = []
    for i, rs in enumerate(subs):
        gates = jax.nn.sigmoid(gt_ref[0, rs, :].astype(F32))
        merged.append((gates[:, 0:D_MODEL] * ma[i] + gates[:, D_MODEL:] * mb[i]).astype(BF16))
    x1 = [x_ref[0, rs, :] + gate1 * _dot(merged[i], wo_ref[...]) for i, rs in enumerate(subs)]
    h2 = []
    for x1_i in x1:
        inv = lax.rsqrt(jnp.mean(x1_i * x1_i, axis=-1, keepdims=True) + RMS_EPS)
        h2.append(((x1_i * inv) * mod2 + shift2).astype(BF16))
    a1 = [_dot(h, w1_ref[...]) for h in h2]
    a3 = [_dot(h, w3_ref[...]) for h in h2]
    z = [(jax.nn.silu(a1[i]) * a3[i]).astype(BF16) for i in range(len(subs))]
    for i, rs in enumerate(subs):
        o_ref[0, rs, :] = x1[i] + gate2 * _dot(z[i], w2_ref[...])


def _tail(x, ada3, ya, yb, gt, wa, wb, wo, gain2, w1, w3, w2):
    batch, seq, _ = x.shape
    tm = TAIL_ROWS
    d_ff = w1.shape[1]
    const = lambda b, j: (0, 0)
    rows = lambda width: pl.BlockSpec((1, tm, width), lambda b, j: (b, j, 0))
    weight = lambda shape: pl.BlockSpec(shape, const, pipeline_mode=pl.Buffered(1))
    return pl.pallas_call(
        _tail_kernel,
        grid=(batch, seq // tm),
        in_specs=[
            rows(D_MODEL),
            pl.BlockSpec((1, 1, 6 * D_MODEL), lambda b, j: (b, 0, 0)),
            rows(RWKV_WIDTH),
            rows(ATTN_Q_WIDTH),
            rows(GATE_WIDTH),
            weight((RWKV_WIDTH, D_MODEL)),
            weight((ATTN_Q_WIDTH, D_MODEL)),
            weight((D_MODEL, D_MODEL)),
            pl.BlockSpec((1, D_MODEL), const),
            weight((D_MODEL, d_ff)),
            weight((D_MODEL, d_ff)),
            weight((d_ff, D_MODEL)),
        ],
        out_specs=rows(D_MODEL),
        out_shape=jax.ShapeDtypeStruct((batch, seq, D_MODEL), F32),
        compiler_params=pltpu.CompilerParams(
            dimension_semantics=("parallel", "parallel"),
            vmem_limit_bytes=V7X_VMEM_LIMIT_BYTES),
        name="tail",
    )(x, ada3, ya, yb, gt, wa, wb, wo, gain2, w1, w3, w2)


def kernel(x, c, positions, ada_w, ada_b, norm1_gain, norm2_gain, w_in, tshift_mu, decay_w0,
           decay_up, iclr_a0, iclr_up, gate_up, k_k, k_a, r_k, lnx_gain, lnx_bias, q_norm_gain,
           k_norm_gain, attn_sinks, branch_gate_b, w_branch_a, w_branch_b, w_out, ffn_w1, ffn_w3,
           ffn_w2):
    depth = ada_w.shape[0]
    batch = x.shape[0]
    cos_tab, sin_tab = _rope_tables(positions)
    for l in range(depth):
        ada3 = _ada(c, ada_w[l], ada_b[l]).reshape(batch, 1, 6 * D_MODEL)
        pack_a, pack_b, w_pack, qkv, gt = _inproj(
            x, ada3, norm1_gain[l].reshape(1, D_MODEL), w_in[l].astype(BF16),
            tshift_mu[l].reshape(1, RWKV_SHIFT_WIDTH), branch_gate_b[l].reshape(1, GATE_WIDTH),
            decay_w0[l], decay_up[l], iclr_a0[l], iclr_up[l], gate_up[l], k_k[l], k_a[l], r_k[l])
        ya = _wkv(pack_a, pack_b, w_pack, lnx_gain[l], lnx_bias[l])
        yb = _attn(qkv, cos_tab, sin_tab, q_norm_gain[l], k_norm_gain[l], attn_sinks[l])
        x = _tail(x, ada3, ya, yb, gt, w_branch_a[l].astype(BF16), w_branch_b[l].astype(BF16),
                  w_out[l].astype(BF16), norm2_gain[l].reshape(1, D_MODEL),
                  ffn_w1[l].astype(BF16), ffn_w3[l].astype(BF16), ffn_w2[l].astype(BF16))
    return x
```

```python
import math

import jax
import jax.numpy as jnp
from jax import lax
from jax.experimental import pallas as pl
from jax.experimental.pallas import tpu as pltpu

F32 = jnp.float32
BF16 = jnp.bfloat16

D_MODEL = 1024
HEAD_DIM = 64
RWKV_HEADS = 8
RWKV_WIDTH = RWKV_HEADS * HEAD_DIM
DECAY_LORA = 64
ICLR_LORA = 64
GATE_LORA = 128
ATTN_Q_HEADS = 8
ATTN_KV_HEADS = 2
ATTN_GROUPS = ATTN_Q_HEADS // ATTN_KV_HEADS
ATTN_Q_WIDTH = ATTN_Q_HEADS * HEAD_DIM
ATTN_KV_WIDTH = ATTN_KV_HEADS * HEAD_DIM
WINDOW = 128
BLOCK = 128
ROPE_THETA = 500000.0
ROPE_DIM = HEAD_DIM // 4
ROPE_HALF = ROPE_DIM // 2
RMS_EPS = 1e-6
GN_EPS = 64e-5
NEG_INF = -1e30
LOG2_E = math.log2(math.e)
RWKV_SHIFT_WIDTH = 3 * RWKV_WIDTH + DECAY_LORA + ICLR_LORA + GATE_LORA
QKV_WIDTH = ATTN_Q_WIDTH + 2 * ATTN_KV_WIDTH
GATE_WIDTH = 2 * D_MODEL
WKV_PACK_A = 4 * RWKV_WIDTH
WKV_PACK_B = 5 * RWKV_WIDTH

V7X_LANES = 128
V7X_SUBLANES = 8
V7X_VMEM_LIMIT_BYTES = 56 * 1024 * 1024

INPROJ_ROWS = 512
INPROJ_SUB_ROWS = 256
WKV_CHUNK = 64
WKV_GROUP_ROWS = 256
WKV_ROWS = 512
WKV_STAGE_SKEW = 4
ATTN_ROWS = 2048
TAIL_ROWS = 512
TAIL_SUB_ROWS = 256


def _dot(a, b):
    return jnp.dot(a, b, preferred_element_type=F32)


def _dot_nt(a, b):
    return lax.dot_general(a, b, (((1,), (1,)), ((), ())), preferred_element_type=F32)


def _dot_tn(a, b):
    return lax.dot_general(a, b, (((0,), (0,)), ((), ())), preferred_element_type=F32)


def _head_mean_matrix(width):
    head = jnp.arange(width) // HEAD_DIM
    return jnp.where(head[:, None] == head[None, :], 1.0 / HEAD_DIM, 0.0).astype(BF16)


def _mm(a, b):
    return _dot(a.astype(BF16), b.astype(BF16))


def _mm_nt(a, b):
    return _dot_nt(a.astype(BF16), b.astype(BF16))


def _ada_kernel(c_ref, w_ref, b_ref, o_ref):
    c, w = c_ref[...], w_ref[...]
    c_hi, w_hi = c.astype(BF16), w.astype(BF16)
    c_lo = (c - c_hi.astype(F32)).astype(BF16)
    w_lo = (w - w_hi.astype(F32)).astype(BF16)
    small = _dot(c_hi, w_lo) + _dot(c_lo, w_hi) + _dot(c_lo, w_lo)
    o_ref[...] = _dot(c_hi, w_hi) + small + b_ref[...]


def _ada(c, ada_w, ada_b):
    batch = c.shape[0]
    n_out = ada_w.shape[1]
    return pl.pallas_call(
        _ada_kernel,
        grid=(n_out // D_MODEL,),
        in_specs=[
            pl.BlockSpec((batch, D_MODEL), lambda j: (0, 0)),
            pl.BlockSpec((D_MODEL, D_MODEL), lambda j: (0, j)),
            pl.BlockSpec((1, D_MODEL), lambda j: (0, j)),
        ],
        out_specs=pl.BlockSpec((batch, D_MODEL), lambda j: (0, j)),
        out_shape=jax.ShapeDtypeStruct((batch, n_out), F32),
        name="ada",
    )(c, ada_w, ada_b.reshape(1, n_out))


def _rope_kernel(pos_ref, freq_ref, sgn_ref, cos_ref, sin_ref):
    per_row = V7X_LANES // ROPE_DIM
    dense_rows = pos_ref.shape[1]
    ang = pos_ref[0].astype(F32) * freq_ref[...]
    cos_d = jnp.cos(ang)
    sin_d = jnp.sin(ang) * sgn_ref[...]
    lane = lax.broadcasted_iota(jnp.int32, ang.shape, 1)
    rotary0 = lane < ROPE_DIM
    rotary1 = (lane >= HEAD_DIM) & (lane < HEAD_DIM + ROPE_DIM)
    for i in range(per_row):
        shift = (V7X_LANES - ROPE_DIM * i) % V7X_LANES
        for dense, fill, out_ref in ((cos_d, 1.0, cos_ref), (sin_d, 0.0, sin_ref)):
            head0 = pltpu.roll(dense, shift, axis=1) if shift else dense
            head1 = pltpu.roll(head0, HEAD_DIM, axis=1)
            row = jnp.where(rotary0, head0, jnp.where(rotary1, head1, fill))
            out_ref[0, pl.ds(i, dense_rows, stride=per_row), :] = row


def _rope_tables(positions):
    batch, seq = positions.shape
    per_row = V7X_LANES // ROPE_DIM
    inv_freq = ROPE_THETA ** (-jnp.arange(ROPE_HALF, dtype=F32) / ROPE_HALF)
    dim = jnp.arange(V7X_LANES) % ROPE_DIM
    freq = inv_freq[dim % ROPE_HALF].reshape(1, V7X_LANES)
    sgn = jnp.where(dim < ROPE_HALF, -1.0, 1.0).astype(F32).reshape(1, V7X_LANES)
    pos = jnp.repeat(positions.reshape(batch, seq // per_row, per_row), ROPE_DIM, axis=-1)
    vec_spec = pl.BlockSpec((1, V7X_LANES), lambda b: (0, 0))
    tab_spec = pl.BlockSpec((1, seq, V7X_LANES), lambda b: (b, 0, 0))
    tab = jax.ShapeDtypeStruct((batch, seq, V7X_LANES), F32)
    return pl.pallas_call(
        _rope_kernel,
        grid=(batch,),
        in_specs=[pl.BlockSpec((1, seq // per_row, V7X_LANES), lambda b: (b, 0, 0)),
                  vec_spec, vec_spec],
        out_specs=[tab_spec, tab_spec],
        out_shape=[tab, tab],
        name="rope",
    )(pos, freq, sgn)


def _inproj_kernel(x_ref, ada_ref, gain_ref, w_ref, mu_ref, gb_ref, w0_ref, dup_ref, a0_ref,
                   aup_ref, gup_ref, kk_ref, ka_ref, rk_ref,
                   pa_ref, pb_ref, wl_ref, qkv_ref, gt_ref, carry_ref):
    rows = INPROJ_SUB_ROWS
    w_rows = rows // WKV_CHUNK * V7X_SUBLANES
    ada = ada_ref[0]
    shift1 = ada[:, 0:D_MODEL]
    mod1 = gain_ref[...] * (1.0 + ada[:, D_MODEL:2 * D_MODEL])

    @pl.when(pl.program_id(1) == 0)
    def _():
        carry_ref[...] = jnp.zeros_like(carry_ref)

    last = carry_ref[...]
    subs = [slice(i * rows, (i + 1) * rows) for i in range(x_ref.shape[1] // rows)]
    hs, cols = [], []
    for rs in subs:
        x = x_ref[0, rs, :]
        inv = lax.rsqrt(jnp.mean(x * x, axis=-1, keepdims=True) + RMS_EPS)
        h = ((x * inv) * mod1 + shift1).astype(BF16)
        p = _dot(h, w_ref[:, 0:RWKV_SHIFT_WIDTH])
        prev = pltpu.roll(p, 1, axis=0)
        row = lax.broadcasted_iota(jnp.int32, p.shape, 0)
        prev = jnp.where(row == 0, last, prev)
        last = p[rows - 1:rows, :]
        hs.append(h)
        cols.append(p + (prev - p) * mu_ref[...])
    carry_ref[...] = last
    for i, rs in enumerate(subs):
        pack_a, pack_b, w_pack = _wkv_prologue(cols[i], w0_ref, dup_ref, a0_ref, aup_ref,
                                                    gup_ref, kk_ref, ka_ref, rk_ref)
        pa_ref[0, rs, :] = pack_a
        pb_ref[0, rs, :] = pack_b
        wl_ref[0, i * w_rows:(i + 1) * w_rows, :] = w_pack
        qkv_ref[0, rs, :] = _dot(hs[i], w_ref[:, RWKV_SHIFT_WIDTH:RWKV_SHIFT_WIDTH + QKV_WIDTH]).astype(BF16)
        gl = _dot(hs[i], w_ref[:, RWKV_SHIFT_WIDTH + QKV_WIDTH:])
        gt_ref[0, rs, :] = (gl + gb_ref[...]).astype(BF16)


def _inproj(x, ada3, gain, w_in_bf, mu, gate_b, decay_w0, decay_up, iclr_a0, iclr_up, gate_up,
            k_k, k_a, r_k):
    batch, seq, _ = x.shape
    in_width = w_in_bf.shape[1]
    tm = INPROJ_ROWS
    W = RWKV_WIDTH
    const = lambda b, j: (0, 0)
    blk = lambda b, j: (b, j, 0)
    vec = pl.BlockSpec((1, W), const)
    w_rows = tm // WKV_CHUNK * V7X_SUBLANES
    return pl.pallas_call(
        _inproj_kernel,
        grid=(batch, seq // tm),
        in_specs=[
            pl.BlockSpec((1, tm, D_MODEL), blk),
            pl.BlockSpec((1, 1, 6 * D_MODEL), lambda b, j: (b, 0, 0)),
            pl.BlockSpec((1, D_MODEL), const),
            pl.BlockSpec((D_MODEL, in_width), const, pipeline_mode=pl.Buffered(1)),
            pl.BlockSpec((1, RWKV_SHIFT_WIDTH), const),
            pl.BlockSpec((1, GATE_WIDTH), const),
            vec,
            pl.BlockSpec((DECAY_LORA, W), const),
            vec,
            pl.BlockSpec((ICLR_LORA, W), const),
            pl.BlockSpec((GATE_LORA, W), const),
            vec, vec, vec,
        ],
        out_specs=[
            pl.BlockSpec((1, tm, WKV_PACK_A), blk),
            pl.BlockSpec((1, tm, WKV_PACK_B), blk),
            pl.BlockSpec((1, w_rows, W), blk),
            pl.BlockSpec((1, tm, QKV_WIDTH), blk),
            pl.BlockSpec((1, tm, GATE_WIDTH), blk),
        ],
        out_shape=[
            jax.ShapeDtypeStruct((batch, seq, WKV_PACK_A), BF16),
            jax.ShapeDtypeStruct((batch, seq, WKV_PACK_B), BF16),
            jax.ShapeDtypeStruct((batch, seq // WKV_CHUNK * V7X_SUBLANES, W), F32),
            jax.ShapeDtypeStruct((batch, seq, QKV_WIDTH), BF16),
            jax.ShapeDtypeStruct((batch, seq, GATE_WIDTH), BF16),
        ],
        scratch_shapes=[pltpu.VMEM((1, RWKV_SHIFT_WIDTH), F32)],
        compiler_params=pltpu.CompilerParams(
            dimension_semantics=("parallel", "arbitrary"),
            vmem_limit_bytes=V7X_VMEM_LIMIT_BYTES),
        name="inproj",
    )(x, ada3, gain, w_in_bf, mu, gate_b, decay_w0.reshape(1, W), decay_up.astype(BF16),
      iclr_a0.reshape(1, W), iclr_up.astype(BF16), gate_up.astype(BF16), k_k.reshape(1, W),
      k_a.reshape(1, W), r_k.reshape(1, W))


def _cumsum_rows(x):
    n = x.shape[0]
    row = lax.broadcasted_iota(jnp.int32, x.shape, 0)
    s = 1
    while s < min(n, V7X_SUBLANES):
        x = x + jnp.where(row >= s, pltpu.roll(x, s, axis=0), 0.0)
        s *= 2
    while s < n:
        x = jnp.concatenate([x[:s], x[s:] + x[:n - s]], axis=0)
        s *= 2
    return x


def _wkv_prologue(cols, w0_ref, dup_ref, a0_ref, aup_ref, gup_ref, kk_ref, ka_ref, rk_ref):
    L, W = WKV_CHUNK, RWKV_WIDTH
    n_chunks = cols.shape[0] // L
    r = cols[:, 0:W]
    k = cols[:, W:2 * W]
    v = cols[:, 2 * W:3 * W]
    o = 3 * W
    xw = cols[:, o:o + DECAY_LORA]
    xa = cols[:, o + DECAY_LORA:o + DECAY_LORA + ICLR_LORA]
    xg = cols[:, o + DECAY_LORA + ICLR_LORA:]

    lw = jax.nn.sigmoid(w0_ref[...] + _mm(jnp.tanh(xw), dup_ref[...])) * (-math.exp(-0.5) * LOG2_E)
    a = jax.nn.sigmoid(a0_ref[...] + _mm(xa, aup_ref[...]))
    g = _mm(jax.nn.sigmoid(xg), gup_ref[...])
    kkp = k * kk_ref[...]
    kk = kkp * jnp.minimum(lax.rsqrt(_head_sum(kkp * kkp)), 1e12)
    k_mod = k * (a * ka_ref[...] + (1.0 - ka_ref[...]))
    bonus = _head_sum(r * k_mod * rk_ref[...]) * v

    cum = jnp.concatenate([_cumsum_rows(lw[c * L:(c + 1) * L]) for c in range(n_chunks)], axis=0)
    e_in = jnp.exp2(cum)
    e_neg = jnp.exp2(-cum)
    w_last = [e_in[(c + 1) * L - 1:(c + 1) * L, :] for c in range(n_chunks)]
    w_rows = jnp.concatenate([jnp.broadcast_to(w, (L, W)) for w in w_last], axis=0)
    bt = kk * a * e_neg
    kt = k_mod * e_neg
    pack_a = jnp.concatenate([-kk * jnp.exp2(cum - lw), r * e_in, g, bonus], axis=1).astype(BF16)
    pack_b = jnp.concatenate([bt, kt, bt * w_rows, kt * w_rows, v], axis=1).astype(BF16)
    w_pack = jnp.concatenate([jnp.broadcast_to(w, (V7X_SUBLANES, W)) for w in w_last], axis=0)
    return pack_a, pack_b, w_pack


def _wkv_chunks(pack_a, pack_b, w_pack, state_ref):
    L, W, HD = WKV_CHUNK, RWKV_WIDTH, HEAD_DIM
    n_chunks = pack_a.shape[0] // L
    row2 = lax.broadcasted_iota(jnp.int32, (2 * L, 2 * L), 0)
    col2 = lax.broadcasted_iota(jnp.int32, (2 * L, 2 * L), 1) % L
    lower2 = col2 < jnp.where(row2 < L, row2, row2 - L + 1)
    lane3 = lax.broadcasted_iota(jnp.int32, (L, 3 * HD), 1)
    zeros_b = jnp.zeros((L, HD), BF16)

    units = [(c, h) for c in range(n_chunks) for h in range(RWKV_HEADS)]

    def pick(arr, c, h):
        return arr[c * L:(c + 1) * L, h * HD:(h + 1) * HD]

    at_b, rt_b = pack_a[:, 0:W], pack_a[:, W:2 * W]
    bt_b, kt_b = pack_b[:, 0:W], pack_b[:, W:2 * W]
    bth_b, kth_b = pack_b[:, 2 * W:3 * W], pack_b[:, 3 * W:4 * W]
    v_b = pack_b[:, 4 * W:5 * W]
    ar_b = {u: (pick(at_b, *u), pick(rt_b, *u)) for u in units}
    a_t = {u: ar_b[u][0].astype(F32) for u in units}
    r_t = {u: ar_b[u][1].astype(F32) for u in units}
    v_h = {u: pick(v_b, *u) for u in units}
    w_l = {(c, h): w_pack[c * V7X_SUBLANES:c * V7X_SUBLANES + 1, h * HD:(h + 1) * HD]
           for c, h in units}
    bk_t = {u: jnp.concatenate([pick(bt_b, *u), pick(kt_b, *u)], axis=0) for u in units}
    bk_hat = {u: jnp.concatenate([pick(bth_b, *u), pick(kth_b, *u)], axis=0) for u in units}

    sc = {u: jnp.where(lower2, _dot_nt(jnp.concatenate(ar_b[u], axis=0), bk_t[u]), 0.0)
          for u in units}
    yield None
    sc_b = {u: sc[u].astype(BF16) for u in units}
    top = {u: sc_b[u][0:L] for u in units}
    bot = {u: sc_b[u][L:2 * L] for u in units}
    vz = {u: jnp.concatenate([zeros_b, v_h[u]], axis=1) for u in units}
    zeros_w = jnp.zeros((L, 2 * HD), BF16)
    akv = {u: _dot(top[u], jnp.concatenate([zeros_w, vz[u]], axis=0)) for u in units}
    yield None

    zeros_f = jnp.zeros((L, HD), F32)
    wx = {u: jnp.concatenate([akv[u] + jnp.concatenate([a_t[u], zeros_f], axis=1),
                              sc[u][0:L, 0:HD]], axis=1) for u in units}
    levels = L.bit_length() - 1
    for _ in range(levels):
        wx_b = {u: wx[u].astype(BF16) for u in units}
        wx = {u: _dot(wx_b[u][:, 2 * HD:3 * HD], wx_b[u]) + jnp.where(lane3 < 2 * HD, wx[u], 0.0)
              for u in units}
        yield None
    x2 = {u: jnp.concatenate([wx[u][:, 0:2 * HD].astype(BF16), vz[u]], axis=0) for u in units}
    ry = {u: _dot(bot[u], x2[u]) for u in units}
    yield None
    gs = {u: _dot_tn(x2[u], bk_hat[u]) for u in units}
    yield None

    y_rows = []
    for c in range(n_chunks):
        y_heads = []
        for h in range(RWKV_HEADS):
            u = (c, h)
            s0 = state_ref[h]
            s0_b = s0.astype(BF16)
            y_heads.append(_dot_nt((r_t[u] + ry[u][:, 0:HD]).astype(BF16), s0_b) + ry[u][:, HD:2 * HD])
            state_ref[h] = s0 * w_l[u] + _dot(s0_b, gs[u][0:HD].astype(BF16)) + gs[u][HD:2 * HD]
        y_rows.append(jnp.concatenate(y_heads, axis=1))
    yield jnp.concatenate(y_rows, axis=0)


def _head_sum(x):
    slabs = []
    for s in range(x.shape[1] // V7X_LANES):
        xs = x[:, s * V7X_LANES:(s + 1) * V7X_LANES]
        lo = lax.broadcasted_iota(jnp.int32, xs.shape, 1) < HEAD_DIM
        lo_sum = jnp.sum(jnp.where(lo, xs, 0.0), axis=-1, keepdims=True)
        hi_sum = jnp.sum(jnp.where(lo, 0.0, xs), axis=-1, keepdims=True)
        slabs.append(jnp.where(lo, lo_sum, hi_sum))
    return jnp.concatenate(slabs, axis=1)


def _head_mean(x):
    return _head_sum(x) * (1.0 / HEAD_DIM)


def _wkv_epilogue(y, pack_a, lng_ref, lnb_ref, head_mean):
    W = RWKV_WIDTH
    g, bonus = pack_a[:, 2 * W:3 * W].astype(F32), pack_a[:, 3 * W:4 * W].astype(F32)
    yc = y - head_mean(y)
    yn = yc * lax.rsqrt(head_mean(yc * yc) + GN_EPS) * lng_ref[...] + lnb_ref[...]
    return ((yn + bonus) * g).astype(BF16)


def _wkv_kernel(pa_ref, pb_ref, w_ref, lng_ref, lnb_ref, hsum_ref, y_ref, state_ref):
    rows = WKV_GROUP_ROWS
    w_rows = rows // WKV_CHUNK * V7X_SUBLANES
    n_groups = pa_ref.shape[1] // rows

    @pl.when(pl.program_id(1) == 0)
    def _():
        state_ref[...] = jnp.zeros_like(state_ref)

    hsum = hsum_ref[...]
    scans = [_wkv_chunks(pa_ref[0, gi * rows:(gi + 1) * rows, :],
                         pb_ref[0, gi * rows:(gi + 1) * rows, :],
                         w_ref[0, gi * w_rows:(gi + 1) * w_rows, :], state_ref) for gi in range(n_groups)]
    n_stages = WKV_CHUNK.bit_length() - 1 + 5
    ys = [None] * n_groups
    for t in range(n_stages + WKV_STAGE_SKEW * (n_groups - 1)):
        for gi in range(n_groups):
            stage = t - WKV_STAGE_SKEW * gi
            if 0 <= stage < n_stages:
                ys[gi] = next(scans[gi])
    for gi in range(n_groups):
        mean = (lambda t: _mm(t, hsum)) if gi == n_groups - 1 else _head_mean
        y_ref[0, gi * rows:(gi + 1) * rows, :] = _wkv_epilogue(
            ys[gi], pa_ref[0, gi * rows:(gi + 1) * rows, :], lng_ref, lnb_ref, mean)


def _wkv(pack_a, pack_b, w_pack, lnx_gain, lnx_bias):
    batch, seq, _ = pack_a.shape
    rows = WKV_ROWS
    W = RWKV_WIDTH
    const = lambda b, j: (0, 0)
    blk = lambda b, j: (b, j, 0)
    vec = pl.BlockSpec((1, W), const)
    hsum = _head_mean_matrix(W)
    return pl.pallas_call(
        _wkv_kernel,
        grid=(batch, seq // rows),
        in_specs=[
            pl.BlockSpec((1, rows, WKV_PACK_A), blk),
            pl.BlockSpec((1, rows, WKV_PACK_B), blk),
            pl.BlockSpec((1, rows // WKV_CHUNK * V7X_SUBLANES, W), blk),
            vec, vec,
            pl.BlockSpec((W, W), const),
        ],
        out_specs=pl.BlockSpec((1, rows, W), blk),
        out_shape=jax.ShapeDtypeStruct((batch, seq, W), BF16),
        scratch_shapes=[pltpu.VMEM((RWKV_HEADS, HEAD_DIM, HEAD_DIM), F32)],
        compiler_params=pltpu.CompilerParams(
            dimension_semantics=("parallel", "arbitrary")),
        name="wkv",
    )(pack_a, pack_b, w_pack, lnx_gain.reshape(1, W), lnx_bias.reshape(1, W), hsum)


def _attn_kernel(sink_ref, q_ref, kv_ref, cos_ref, sin_ref, bias_ref, qg_ref, kg_ref, hmean_ref,
                 perm_ref, o_ref, kprev_ref, vprev_ref):
    n_blk = q_ref.shape[1] // BLOCK
    n_slab = ATTN_Q_WIDTH // V7X_LANES
    slab_per_kv = n_slab // ATTN_KV_HEADS
    kvs = range(ATTN_KV_HEADS)
    pars = range(2 * ATTN_KV_HEADS)

    @pl.when(pl.program_id(1) == 0)
    def _():
        kprev_ref[...] = jnp.zeros_like(kprev_ref)
        vprev_ref[...] = jnp.zeros_like(vprev_ref)

    cos, sin = cos_ref[0], sin_ref[0]
    q_gain = qg_ref[...] * (HEAD_DIM ** -0.5 * LOG2_E)
    q_all = q_ref[0].astype(F32)
    kv = kv_ref[0].astype(F32)
    slabs = [q_all[:, s * V7X_LANES:(s + 1) * V7X_LANES] for s in range(n_slab)]
    slabs.append(kv[:, 0:ATTN_KV_WIDTH])
    gains = [q_gain] * n_slab + [kg_ref[...]]
    n_rows = n_blk * BLOCK

    ms_all = _mm(jnp.concatenate([x * x for x in slabs], axis=0), hmean_ref[...])
    xn = [x * lax.rsqrt(ms_all[i * n_rows:(i + 1) * n_rows] + RMS_EPS) * gains[i]
          for i, x in enumerate(slabs)]
    partner = _mm(jnp.concatenate(xn, axis=0), perm_ref[0])
    normed = [xn[i] * cos + partner[i * n_rows:(i + 1) * n_rows] * sin for i in range(len(slabs))]
    qn, k_cur = normed[:n_slab], normed[n_slab]
    v_cur = kv[:, ATTN_KV_WIDTH:]
    swapped = _mm(jnp.concatenate([k_cur, v_cur], axis=0), perm_ref[1])
    k_swap, v_swap = swapped[0:n_rows], swapped[n_rows:]
    lo = lax.broadcasted_iota(jnp.int32, (n_rows, V7X_LANES), 1) < HEAD_DIM
    kdup_cur = [jnp.where(lo, k_cur, k_swap).astype(BF16), jnp.where(lo, k_swap, k_cur).astype(BF16)]
    vpar_cur = [jnp.where(lo, v_cur, 0.0).astype(BF16), jnp.where(lo, 0.0, v_swap).astype(BF16),
                jnp.where(lo, v_swap, 0.0).astype(BF16), jnp.where(lo, 0.0, v_cur).astype(BF16)]

    def band(prev_ref, cur, j, i):
        if i == 0:
            return jnp.concatenate([prev_ref[j], cur[j][0:BLOCK]], axis=0)
        return cur[j][(i - 1) * BLOCK:(i + 1) * BLOCK]

    units = [(i, hk) for i in range(n_blk) for hk in kvs]
    kband = {(i, hk): band(kprev_ref, kdup_cur, hk, i) for i, hk in units}
    ones_b = jnp.ones((2 * BLOCK, V7X_LANES), BF16)
    vaug = {(i, j): jnp.concatenate([band(vprev_ref, vpar_cur, j, i), ones_b], axis=1)
            for i in range(n_blk) for j in pars}
    for hk in kvs:
        kprev_ref[hk] = kdup_cur[hk][n_rows - BLOCK:]
    for j in pars:
        vprev_ref[j] = vpar_cur[j][n_rows - BLOCK:]

    lo1 = lax.broadcasted_iota(jnp.int32, (BLOCK, V7X_LANES), 1) < HEAD_DIM
    stack = 2 * slab_per_kv
    first_bias = bias_ref[jnp.minimum(pl.program_id(1), 1)]
    bias = [jnp.concatenate([first_bias if i == 0 else bias_ref[1]] * stack, axis=0)
            for i in range(n_blk)]
    heads = [[2 * (hk * slab_per_kv + j) + p for p in range(2) for j in range(slab_per_kv)]
             for hk in kvs]
    lhs = {}
    for i, hk in units:
        mine = [qn[hk * slab_per_kv + j][i * BLOCK:(i + 1) * BLOCK] for j in range(slab_per_kv)]
        lhs[(i, hk)] = jnp.concatenate([jnp.where(lo1, x, 0.0) for x in mine]
                                       + [jnp.where(lo1, 0.0, x) for x in mine], axis=0).astype(BF16)
    s = {u: _mm_nt(lhs[u], kband[u]) + bias[u[0]] for u in units}
    rmax = {u: jnp.max(s[u], axis=-1, keepdims=True) for u in units}
    sink2 = [sink_ref[h] * LOG2_E for h in range(ATTN_Q_HEADS)]
    m = {(u, t): jnp.maximum(rmax[u][t * BLOCK:(t + 1) * BLOCK], sink2[heads[u[1]][t]])
         for u in units for t in range(stack)}
    e = {u: jnp.concatenate([jnp.exp2(s[u][t * BLOCK:(t + 1) * BLOCK] - m[(u, t)])
                             for t in range(stack)], axis=0).astype(BF16) for u in units}
    half = stack * BLOCK // 2
    pv = {(i, hk, p): _mm(e[(i, hk)][p * half:(p + 1) * half], vaug[(i, 2 * hk + p)])
          for i, hk in units for p in range(2)}
    extra = {(u, t): jnp.exp2(sink2[heads[u[1]][t]] - m[(u, t)]) for u in units for t in range(stack)}
    for i, hk in units:
        for j in range(slab_per_kv):
            r0 = slice(j * BLOCK, (j + 1) * BLOCK)
            even, odd = pv[(i, hk, 0)][r0], pv[(i, hk, 1)][r0]
            num = even[:, 0:V7X_LANES] + odd[:, 0:V7X_LANES]
            den = jnp.where(lo1, even[:, V7X_LANES:] + extra[((i, hk), j)],
                            odd[:, V7X_LANES:] + extra[((i, hk), slab_per_kv + j)])
            slab = hk * slab_per_kv + j
            o_ref[0, i * BLOCK:(i + 1) * BLOCK, slab * V7X_LANES:(slab + 1) * V7X_LANES] = (
                num * (1.0 / den)).astype(BF16)


def _attn(qkv, cos_tab, sin_tab, q_gain, k_gain, sinks):
    batch, seq, _ = qkv.shape
    rows = ATTN_ROWS
    cur = lambda b, n: (b, n, 0)
    const = lambda b, n: (0, 0)
    const3 = lambda b, n: (0, 0, 0)
    kv_blk = ATTN_Q_WIDTH // (2 * ATTN_KV_WIDTH)
    gain2 = lambda gn: jnp.tile(gn.reshape(1, HEAD_DIM), (1, V7X_LANES // HEAD_DIM))
    lane = jnp.arange(V7X_LANES)
    hmean = _head_mean_matrix(V7X_LANES)
    dim = lane % HEAD_DIM
    src = jnp.where(dim < ROPE_HALF, lane + ROPE_HALF, jnp.where(dim < ROPE_DIM, lane - ROPE_HALF, -1))
    partner_p = lane[:, None] == src[None, :]
    swap_p = lane[:, None] == ((lane + HEAD_DIM) % V7X_LANES)[None, :]
    perm = jnp.stack([partner_p, swap_p]).astype(BF16)
    dist = jnp.arange(BLOCK)[:, None] + BLOCK - jnp.arange(2 * BLOCK)[None, :]
    in_band = (dist >= 0) & (dist < WINDOW)
    own = (jnp.arange(2 * BLOCK) >= BLOCK)[None, :]
    bias = jnp.where(jnp.stack([in_band & own, in_band]), 0.0, NEG_INF).astype(F32)
    return pl.pallas_call(
        _attn_kernel,
        grid=(batch, seq // rows),
        in_specs=[
            pl.BlockSpec(memory_space=pltpu.SMEM),
            pl.BlockSpec((1, rows, ATTN_Q_WIDTH), cur),
            pl.BlockSpec((1, rows, 2 * ATTN_KV_WIDTH), lambda b, n: (b, n, kv_blk)),
            pl.BlockSpec((1, rows, V7X_LANES), cur),
            pl.BlockSpec((1, rows, V7X_LANES), cur),
            pl.BlockSpec((2, BLOCK, 2 * BLOCK), const3),
            pl.BlockSpec((1, V7X_LANES), const),
            pl.BlockSpec((1, V7X_LANES), const),
            pl.BlockSpec((V7X_LANES, V7X_LANES), const),
            pl.BlockSpec((2, V7X_LANES, V7X_LANES), const3),
        ],
        out_specs=pl.BlockSpec((1, rows, ATTN_Q_WIDTH), cur),
        out_shape=jax.ShapeDtypeStruct((batch, seq, ATTN_Q_WIDTH), BF16),
        scratch_shapes=[
            pltpu.VMEM((ATTN_KV_HEADS, BLOCK, V7X_LANES), BF16),
            pltpu.VMEM((2 * ATTN_KV_HEADS, BLOCK, V7X_LANES), BF16),
        ],
        compiler_params=pltpu.CompilerParams(
            dimension_semantics=("parallel", "arbitrary")),
        name="attn",
    )(sinks, qkv, qkv, cos_tab, sin_tab, bias, gain2(q_gain), gain2(k_gain), hmean, perm)


def _tail_kernel(x_ref, ada_ref, ya_ref, yb_ref, gt_ref, wa_ref, wb_ref, wo_ref, gain_ref,
                 w1_ref, w3_ref, w2_ref, o_ref):
    ada = ada_ref[0]
    gate1 = ada[:, 2 * D_MODEL:3 * D_MODEL]
    shift2 = ada[:, 3 * D_MODEL:4 * D_MODEL]
    scale2 = ada[:, 4 * D_MODEL:5 * D_MODEL]
    gate2 = ada[:, 5 * D_MODEL:6 * D_MODEL]
    mod2 = gain_ref[...] * (1.0 + scale2)

    rows = TAIL_SUB_ROWS
    subs = [slice(i * rows, (i + 1) * rows) for i in range(x_ref.shape[1] // rows)]
    ma = [_dot(ya_ref[0, rs, :], wa_ref[...]) for rs in subs]
    mb = [_dot(yb_ref[0, rs, :], wb_ref[...]) for rs in subs]
    merged = []
    for i, rs in enumerate(subs):
        gates = jax.nn.sigmoid(gt_ref[0, rs, :].astype(F32))
        merged.append((gates[:, 0:D_MODEL] * ma[i] + gates[:, D_MODEL:] * mb[i]).astype(BF16))
    x1 = [x_ref[0, rs, :] + gate1 * _dot(merged[i], wo_ref[...]) for i, rs in enumerate(subs)]
    h2 = []
    for x1_i in x1:
        inv = lax.rsqrt(jnp.mean(x1_i * x1_i, axis=-1, keepdims=True) + RMS_EPS)
        h2.append(((x1_i * inv) * mod2 + shift2).astype(BF16))
    a1 = [_dot(h, w1_ref[...]) for h in h2]
    a3 = [_dot(h, w3_ref[...]) for h in h2]
    z = [(jax.nn.silu(a1[i]) * a3[i]).astype(BF16) for i in range(len(subs))]
    for i, rs in enumerate(subs):
        o_ref[0, rs, :] = x1[i] + gate2 * _dot(z[i], w2_ref[...])


def _tail(x, ada3, ya, yb, gt, wa, wb, wo, gain2, w1, w3, w2):
    batch, seq, _ = x.shape
    tm = TAIL_ROWS
    d_ff = w1.shape[1]
    const = lambda b, j: (0, 0)
    rows = lambda width: pl.BlockSpec((1, tm, width), lambda b, j: (b, j, 0))
    weight = lambda shape: pl.BlockSpec(shape, const, pipeline_mode=pl.Buffered(1))
    return pl.pallas_call(
        _tail_kernel,
        grid=(batch, seq // tm),
        in_specs=[
            rows(D_MODEL),
            pl.BlockSpec((1, 1, 6 * D_MODEL), lambda b, j: (b, 0, 0)),
            rows(RWKV_WIDTH),
            rows(ATTN_Q_WIDTH),
            rows(GATE_WIDTH),
            weight((RWKV_WIDTH, D_MODEL)),
            weight((ATTN_Q_WIDTH, D_MODEL)),
            weight((D_MODEL, D_MODEL)),
            pl.BlockSpec((1, D_MODEL), const),
            weight((D_MODEL, d_ff)),
            weight((D_MODEL, d_ff)),
            weight((d_ff, D_MODEL)),
        ],
        out_specs=rows(D_MODEL),
        out_shape=jax.ShapeDtypeStruct((batch, seq, D_MODEL), F32),
        compiler_params=pltpu.CompilerParams(
            dimension_semantics=("parallel", "parallel"),
            vmem_limit_bytes=V7X_VMEM_LIMIT_BYTES),
        name="tail",
    )(x, ada3, ya, yb, gt, wa, wb, wo, gain2, w1, w3, w2)


def kernel(x, c, positions, ada_w, ada_b, norm1_gain, norm2_gain, w_in, tshift_mu, decay_w0,
           decay_up, iclr_a0, iclr_up, gate_up, k_k, k_a, r_k, lnx_gain, lnx_bias, q_norm_gain,
           k_norm_gain, attn_sinks, branch_gate_b, w_branch_a, w_branch_b, w_out, ffn_w1, ffn_w3,
           ffn_w2):
    depth = ada_w.shape[0]
    batch = x.shape[0]
    cos_tab, sin_tab = _rope_tables(positions)
    for l in range(depth):
        ada3 = _ada(c, ada_w[l], ada_b[l]).reshape(batch, 1, 6 * D_MODEL)
        pack_a, pack_b, w_pack, qkv, gt = _inproj(
            x, ada3, norm1_gain[l].reshape(1, D_MODEL), w_in[l].astype(BF16),
            tshift_mu[l].reshape(1, RWKV_SHIFT_WIDTH), branch_gate_b[l].reshape(1, GATE_WIDTH),
            decay_w0[l], decay_up[l], iclr_a0[l], iclr_up[l], gate_up[l], k_k[l], k_a[l], r_k[l])
        ya = _wkv(pack_a, pack_b, w_pack, lnx_gain[l], lnx_bias[l])
        yb = _attn(qkv, cos_tab, sin_tab, q_norm_gain[l], k_norm_gain[l], attn_sinks[l])
        x = _tail(x, ada3, ya, yb, gt, w_branch_a[l].astype(BF16), w_branch_b[l].astype(BF16),
                  w_out[l].astype(BF16), norm2_gain[l].reshape(1, D_MODEL),
                  ffn_w1[l].astype(BF16), ffn_w3[l].astype(BF16), ffn_w2[l].astype(BF16))
    return x
```
